```python
import functools
import jax, jax.numpy as jnp
from jax import lax
import numpy as np

D_MODEL = 2048
BATCH = 4
SEQ = 2048
DEPTH = 1
DEC_BATCH = 128
DEC_SEQ = 8
PAST_LEN = 16384
PAGE_SIZE = 128

ATTN_WIDTH = D_MODEL // 2
CONV_CH = D_MODEL - ATTN_WIDTH
HEAD_DIM = 64
N_HEADS = ATTN_WIDTH // HEAD_DIM
N_KV_HEADS = max(1, N_HEADS // 4)
GROUP = N_HEADS // N_KV_HEADS
KV_WIDTH = N_KV_HEADS * HEAD_DIM
WINDOW = 128
BLOCK = 128
CONV_K = 3
D_FF = ((8 * D_MODEL // 3 + 255) // 256) * 256
IN_WIDTH = ATTN_WIDTH + 2 * KV_WIDTH + 3 * CONV_CH
SPLITS = [ATTN_WIDTH, ATTN_WIDTH + KV_WIDTH, ATTN_WIDTH + 2 * KV_WIDTH,
          ATTN_WIDTH + 2 * KV_WIDTH + CONV_CH, ATTN_WIDTH + 2 * KV_WIDTH + 2 * CONV_CH]
EPS = 1e-6
NEG = -1e30

kernel_name = "hymba_swa_sink_shortconv_convffn_step"


def _rmsnorm(x, g):
    xf = x.astype(jnp.float32)
    y = xf * lax.rsqrt(jnp.mean(xf * xf, axis=-1, keepdims=True) + EPS) * g.astype(jnp.float32)
    return y.astype(x.dtype)


def _alibi_slopes():
    h = jnp.arange(1, N_HEADS + 1, dtype=jnp.float32)
    return jnp.exp2(-8.0 * h / N_HEADS).reshape(N_KV_HEADS, GROUP)


def _sink_attend(q, k, v, dist, valid, sinks):
    s = jnp.einsum('...qkgd,...skd->...kgqs', q, k,
                   preferred_element_type=jnp.float32) * (HEAD_DIM ** -0.5)
    slopes = _alibi_slopes()[:, :, None, None]
    s = jnp.where(valid, s - slopes * dist, NEG)
    sink = sinks.astype(jnp.float32).reshape(N_KV_HEADS, GROUP)[:, :, None]
    m = jnp.maximum(jnp.max(s, axis=-1), sink)
    p = jnp.exp(s - m[..., None])
    denom = jnp.sum(p, axis=-1) + jnp.exp(sink - m)
    w = (p / denom[..., None]).astype(v.dtype)
    return jnp.einsum('...kgqs,...skd->...qkgd', w, v)


def _attend_prompt(q, k, v, sinks):
    b, t = q.shape[:2]
    nb = t // BLOCK
    qb = q.reshape(b, nb, BLOCK, N_KV_HEADS, GROUP, HEAD_DIM)

    def with_prev(a):
        a = a.reshape(b, nb, BLOCK, N_KV_HEADS, HEAD_DIM)
        prev = jnp.concatenate([jnp.zeros_like(a[:, :1]), a[:, :-1]], axis=1)
        return jnp.concatenate([prev, a], axis=2)

    qi = jnp.arange(BLOCK)[:, None]
    kj = jnp.arange(2 * BLOCK)[None, :]
    dist = qi - kj + BLOCK
    key_pos = jnp.arange(nb)[:, None, None] * BLOCK + kj[None] - BLOCK
    valid = (dist >= 0) & (dist <= WINDOW) & (key_pos >= 0)
    out = _sink_attend(qb, with_prev(k), with_prev(v), dist.astype(jnp.float32),
                       valid[:, None, None], sinks)
    return out.reshape(b, t, ATTN_WIDTH)


def _attend_sample(q, k, v, sinks, k_buf, v_buf):
    b, t = q.shape[:2]
    kk = jnp.concatenate([k_buf.astype(k.dtype), k], axis=1)
    vv = jnp.concatenate([v_buf.astype(v.dtype), v], axis=1)
    qi = jnp.arange(t)[:, None]
    kj = jnp.arange(WINDOW + t)[None, :]
    dist = qi + WINDOW - kj
    valid = (dist >= 0) & (dist <= WINDOW)
    out = _sink_attend(q.reshape(b, t, N_KV_HEADS, GROUP, HEAD_DIM), kk, vv,
                       dist.astype(jnp.float32), valid, sinks)
    return out.reshape(b, t, ATTN_WIDTH)


def _causal_dwconv(u, prev, w):
    t = u.shape[1]
    ext = jnp.concatenate([prev.astype(u.dtype), u], axis=1)
    out = w[0] * ext[:, 0:t]
    for j in range(1, CONV_K):
        out = out + w[j] * ext[:, j:j + t]
    return out, ext[:, t:]


def _layer(x, attend_fn, conv_prev, ffn_prev, g_attn_norm, w_in, attn_sinks, conv_w,
           g_out_attn, g_out_conv, w_out, g_ffn_norm, w_gate, w_up, ffn_conv_w, ffn_conv_b, w_down):
    b, t, _ = x.shape
    h = _rmsnorm(x, g_attn_norm)
    q, k, v, gate_b, gate_c, u_in = jnp.split(h @ w_in, SPLITS, axis=-1)
    k = k.reshape(b, t, N_KV_HEADS, HEAD_DIM)
    v = v.reshape(b, t, N_KV_HEADS, HEAD_DIM)
    attn = attend_fn(q, k, v, attn_sinks)
    conv_out, conv_state = _causal_dwconv(gate_c * u_in, conv_prev, conv_w)
    sconv = gate_b * conv_out
    mixed = jnp.concatenate([_rmsnorm(attn, g_out_attn), _rmsnorm(sconv, g_out_conv)], axis=-1)
    x = x + mixed @ w_out
    h = _rmsnorm(x, g_ffn_norm)
    a, ffn_state = _causal_dwconv(h @ w_gate, ffn_prev, ffn_conv_w)
    x = x + (jax.nn.silu(a + ffn_conv_b) * (h @ w_up)) @ w_down
    return x, k, v, conv_state, ffn_state


def setup_inputs(seed: int = 0) -> dict:
    key = jax.random.key(seed)
    ks = jax.random.split(key, 24)
    f32 = jnp.float32
    nrm = lambda k, shape, s=1.0: (jax.random.normal(k, shape, f32) * s).astype(f32)
    gain = lambda k, shape: 1.0 + 0.02 * jax.random.normal(k, shape, f32)
    return {
        "x_prompt": nrm(ks[0], (BATCH, SEQ, D_MODEL)),
        "x_sample": nrm(ks[1], (DEC_BATCH, DEC_SEQ, D_MODEL)),
        "cache_k_window": nrm(ks[2], (DEPTH, DEC_BATCH, WINDOW, N_KV_HEADS, HEAD_DIM)),
        "cache_v_window": nrm(ks[3], (DEPTH, DEC_BATCH, WINDOW, N_KV_HEADS, HEAD_DIM)),
        "state_conv": nrm(ks[4], (DEPTH, DEC_BATCH, CONV_K - 1, CONV_CH)),
        "state_ffn_conv": nrm(ks[5], (DEPTH, DEC_BATCH, CONV_K - 1, D_FF)),
        "g_attn_norm": gain(ks[6], (DEPTH, D_MODEL)),
        "w_in": nrm(ks[7], (DEPTH, D_MODEL, IN_WIDTH), D_MODEL ** -0.5),
        "attn_sinks": nrm(ks[8], (DEPTH, N_HEADS), 0.5),
        "conv_w": nrm(ks[9], (DEPTH, CONV_K, CONV_CH), CONV_K ** -0.5),
        "g_out_attn": gain(ks[10], (DEPTH, ATTN_WIDTH)),
        "g_out_conv": gain(ks[11], (DEPTH, CONV_CH)),
        "w_out": nrm(ks[12], (DEPTH, D_MODEL, D_MODEL), D_MODEL ** -0.5),
        "g_ffn_norm": gain(ks[13], (DEPTH, D_MODEL)),
        "w_gate": nrm(ks[14], (DEPTH, D_MODEL, D_FF), D_MODEL ** -0.5),
        "w_up": nrm(ks[15], (DEPTH, D_MODEL, D_FF), D_MODEL ** -0.5),
        "ffn_conv_w": nrm(ks[16], (DEPTH, CONV_K, D_FF), CONV_K ** -0.5),
        "ffn_conv_b": nrm(ks[17], (DEPTH, D_FF), 0.01),
        "w_down": nrm(ks[18], (DEPTH, D_FF, D_MODEL), D_FF ** -0.5),
        "g_final": gain(ks[19], (D_MODEL,)),
    }


def reference(x_prompt, x_sample, cache_k_window, cache_v_window, state_conv, state_ffn_conv,
              g_attn_norm, w_in, attn_sinks, conv_w, g_out_attn, g_out_conv, w_out,
              g_ffn_norm, w_gate, w_up, ffn_conv_w, ffn_conv_b, w_down, g_final):
    yp, ys = x_prompt, x_sample
    bp = x_prompt.shape[0]
    kwp, vwp, cvp, ffp = [], [], [], []
    kws, vws, cvs, ffs = [], [], [], []
    for l in range(DEPTH):
        w = (g_attn_norm[l], w_in[l], attn_sinks[l], conv_w[l], g_out_attn[l], g_out_conv[l],
             w_out[l], g_ffn_norm[l], w_gate[l], w_up[l], ffn_conv_w[l], ffn_conv_b[l], w_down[l])
        conv0 = jnp.zeros((bp, CONV_K - 1, CONV_CH), yp.dtype)
        ffn0 = jnp.zeros((bp, CONV_K - 1, D_FF), yp.dtype)
        yp, kp, vp, cp, fp = _layer(yp, _attend_prompt, conv0, ffn0, *w)
        kwp.append(kp[:, -WINDOW:])
        vwp.append(vp[:, -WINDOW:])
        cvp.append(cp)
        ffp.append(fp)
        att_s = functools.partial(_attend_sample, k_buf=cache_k_window[l], v_buf=cache_v_window[l])
        ys, k_s, v_s, c_s, f_s = _layer(ys, att_s, state_conv[l], state_ffn_conv[l], *w)
        kws.append(jnp.concatenate([cache_k_window[l].astype(k_s.dtype), k_s], axis=1)[:, -WINDOW:])
        vws.append(jnp.concatenate([cache_v_window[l].astype(v_s.dtype), v_s], axis=1)[:, -WINDOW:])
        cvs.append(c_s)
        ffs.append(f_s)
    y_prompt = _rmsnorm(yp, g_final)
    y_sample = _rmsnorm(ys, g_final)
    return (y_prompt, y_sample,
            jnp.stack(kwp), jnp.stack(vwp), jnp.stack(cvp), jnp.stack(ffp),
            jnp.stack(kws), jnp.stack(vws), jnp.stack(cvs), jnp.stack(ffs))
```

```python
import functools

import jax
import jax.numpy as jnp
from jax import lax
from jax.experimental import pallas as pl
from jax.experimental.pallas import tpu as pltpu

D_MODEL = 2048
ATTN_WIDTH = 1024
CONV_CH = 1024
HEAD_DIM = 64
N_HEADS = 16
N_KV_HEADS = 4
GROUP = 4
KV_WIDTH = 256
WINDOW = 128
BLOCK = 128
D_FF = 5632
EPS = 1e-6
NEG = -1e30

SUBLANES = 8
ROW_TILE = 512
COL_TILE = 512
Q_STEPS = ATTN_WIDTH // COL_TILE
KV_STEP = Q_STEPS
CONV_STEP0 = KV_STEP + 1
CONV_STEPS = CONV_CH // COL_TILE
IN_STEPS = CONV_STEP0 + CONV_STEPS
FF_STEPS = D_FF // COL_TILE
B_BLK0 = (ATTN_WIDTH + 2 * KV_WIDTH) // COL_TILE
C_BLK0 = B_BLK0 + CONV_STEPS
U_BLK0 = C_BLK0 + CONV_STEPS
VMEM_LIMIT = 56 * 1024 * 1024

_SLOPES = [2.0 ** (-8.0 * (h + 1) / N_HEADS) for h in range(N_HEADS)]
_BF = jnp.bfloat16
_F32 = jnp.float32


def _rms(x, g):
    return x * lax.rsqrt(jnp.mean(x * x, axis=-1, keepdims=True) + EPS) * g


def _dot(a, b):
    return jnp.dot(a, b, preferred_element_type=_F32)


def _dot_nt(a, b):
    return lax.dot_general(a, b, (((1,), (1,)), ((), ())), preferred_element_type=_F32)


def _conv3_carry(cur, w_ref, ext_ref, carry_ref, is_seq_start):
    rows = cur.shape[0]
    prev = jnp.where(is_seq_start, 0.0, carry_ref[...])
    ext_ref[0:SUBLANES, :] = prev
    ext_ref[SUBLANES:SUBLANES + rows, :] = cur
    carry_ref[...] = cur[rows - SUBLANES:, :]
    p1 = ext_ref[SUBLANES - 1:SUBLANES - 1 + rows, :]
    p2 = ext_ref[SUBLANES - 2:SUBLANES - 2 + rows, :]
    return w_ref[2:3, :] * cur + w_ref[1:2, :] * p1 + w_ref[0:1, :] * p2


def _conv3_seqs(cur, w_ref, state_ref):
    rows, ch = cur.shape
    nseq = rows // SUBLANES
    cur3 = cur.reshape(nseq, SUBLANES, ch)
    st = state_ref[...]
    t = lax.broadcasted_iota(jnp.int32, cur3.shape, 1)
    s0 = jnp.broadcast_to(st[:, 0:1, :], cur3.shape)
    s1 = jnp.broadcast_to(st[:, 1:2, :], cur3.shape)
    p1 = jnp.where(t == 0, s1, pltpu.roll(cur3, 1, 1))
    p2 = jnp.where(t == 0, s0, jnp.where(t == 1, s1, pltpu.roll(cur3, 2, 1)))
    w = w_ref[...]
    out = w[2:3, :][None] * cur3 + w[1:2, :][None] * p1 + w[0:1, :][None] * p2
    return out.reshape(rows, ch), cur3[:, SUBLANES - 2:, :]


def _inproj_kernel(*refs, seq_tiles, q_dtype):
    carry_mode = seq_tiles is not None
    if carry_mode:
        (x_ref, g_ref, wa_ref, wb_ref, wc_ref, cw_ref, gc_ref,
         q_ref, kv_ref, nb_ref, st_ref, h_scr, y_scr, ss_scr, ext_scr, carry_scr) = refs
    else:
        (x_ref, g_ref, wa_ref, wb_ref, wc_ref, cw_ref, gc_ref, state_ref,
         q_ref, kv_ref, nb_ref, st_ref, h_scr, y_scr, ss_scr) = refs
    i = pl.program_id(0)
    j = pl.program_id(1)

    @pl.when(j == 0)
    def _():
        h_scr[...] = _rms(x_ref[...], g_ref[...]).astype(_BF)

    @pl.when(j < Q_STEPS)
    def _():
        q_ref[...] = (_dot(h_scr[...], wa_ref[...]) * (HEAD_DIM ** -0.5)).astype(q_dtype)

    @pl.when(j == KV_STEP)
    def _():
        kv_ref[...] = _dot(h_scr[...], wa_ref[...])

    for c in range(CONV_STEPS):
        @pl.when(j == CONV_STEP0 + c)
        def _(c=c):
            h = h_scr[...]
            cu = _dot(h, wb_ref[...]) * _dot(h, wc_ref[...])
            if carry_mode:
                conv = _conv3_carry(cu, cw_ref, ext_scr, carry_scr.at[c], i % seq_tiles == 0)
                st_ref[0] = cu[cu.shape[0] - 2:, :]
            else:
                conv, last2 = _conv3_seqs(cu, cw_ref, state_ref)
                st_ref[...] = last2
            y = _dot(h, wa_ref[...]) * conv
            y_scr[c] = y
            ss = jnp.sum(y * y, axis=-1, keepdims=True)
            if c == 0:
                ss_scr[...] = ss
            else:
                ss_scr[...] += ss
            if c == CONV_STEPS - 1:
                inv = lax.rsqrt(ss_scr[...] * (1.0 / CONV_CH) + EPS)
                for cc in range(CONV_STEPS):
                    sl = slice(cc * COL_TILE, (cc + 1) * COL_TILE)
                    nb_ref[:, sl] = (y_scr[cc] * inv * gc_ref[:, sl]).astype(_BF)


def _inproj(x, g, w_in, conv_w, g_conv, state, *, seq_len, q_dtype):
    n = x.shape[0]
    tm = ROW_TILE
    nt = n // tm
    carry_mode = seq_len >= tm
    seq_tiles = seq_len // tm if carry_mode else None

    def wa_map(i, j):
        return (0, j)

    def wb_map(i, j):
        return (0, C_BLK0 + jnp.maximum(j - CONV_STEP0, 0))

    def wc_map(i, j):
        return (0, U_BLK0 + jnp.maximum(j - CONV_STEP0, 0))

    def cmap(i, j):
        return (0, jnp.maximum(j - CONV_STEP0, 0))

    in_specs = [
        pl.BlockSpec((tm, D_MODEL), lambda i, j: (i, 0)),
        pl.BlockSpec((1, D_MODEL), lambda i, j: (0, 0)),
        pl.BlockSpec((D_MODEL, COL_TILE), wa_map),
        pl.BlockSpec((D_MODEL, COL_TILE), wb_map),
        pl.BlockSpec((D_MODEL, COL_TILE), wc_map),
        pl.BlockSpec((3, COL_TILE), cmap),
        pl.BlockSpec((1, CONV_CH), lambda i, j: (0, 0)),
    ]
    args = [x, g, w_in, w_in, w_in, conv_w, g_conv]
    scratch = [
        pltpu.VMEM((tm, D_MODEL), _BF),
        pltpu.VMEM((CONV_STEPS, tm, COL_TILE), _F32),
        pltpu.VMEM((tm, 1), _F32),
    ]
    if carry_mode:
        st_shape = jax.ShapeDtypeStruct((nt, 2, CONV_CH), _F32)
        st_spec = pl.BlockSpec((1, 2, COL_TILE), lambda i, j: (i, 0, jnp.maximum(j - CONV_STEP0, 0)))
        scratch += [pltpu.VMEM((tm + SUBLANES, COL_TILE), _F32),
                    pltpu.VMEM((CONV_STEPS, SUBLANES, COL_TILE), _F32)]
    else:
        nseq = tm // seq_len
        st_shape = jax.ShapeDtypeStruct((n // seq_len, 2, CONV_CH), _F32)
        st_spec = pl.BlockSpec((nseq, 2, COL_TILE), lambda i, j: (i, 0, jnp.maximum(j - CONV_STEP0, 0)))
        in_specs.append(st_spec)
        args.append(state)
    out_shape = [
        jax.ShapeDtypeStruct((n, ATTN_WIDTH), q_dtype),
        jax.ShapeDtypeStruct((n, 2 * KV_WIDTH), _F32),
        jax.ShapeDtypeStruct((n, CONV_CH), _BF),
        st_shape,
    ]
    out_specs = [
        pl.BlockSpec((tm, COL_TILE), lambda i, j: (i, jnp.minimum(j, Q_STEPS - 1))),
        pl.BlockSpec((tm, 2 * KV_WIDTH), lambda i, j: (i, 0)),
        pl.BlockSpec((tm, CONV_CH), lambda i, j: (i, 0)),
        st_spec,
    ]
    return pl.pallas_call(
        functools.partial(_inproj_kernel, seq_tiles=seq_tiles, q_dtype=q_dtype),
        grid=(nt, IN_STEPS),
        in_specs=in_specs,
        out_specs=out_specs,
        out_shape=out_shape,
        scratch_shapes=scratch,
        compiler_params=pltpu.CompilerParams(
            dimension_semantics=("arbitrary", "arbitrary"), vmem_limit_bytes=VMEM_LIMIT),
        name="inproj_carry" if carry_mode else "inproj_seqs",
    )(*args)


def _softmax_sink(s, sink):
    m = jnp.maximum(jnp.max(s, axis=-1, keepdims=True), sink)
    p = jnp.exp(s - m)
    return p, jnp.sum(p, axis=-1, keepdims=True) + jnp.exp(sink - m)


def _attn_prompt_kernel(sinks_ref, q_ref, kvc_ref, kvp_ref, g_ref, o_ref, acc_ref):
    i = pl.program_id(1)
    qi = lax.broadcasted_iota(jnp.int32, (BLOCK, 2 * BLOCK), 0)
    kj = lax.broadcasted_iota(jnp.int32, (BLOCK, 2 * BLOCK), 1)
    dist_i = qi - kj + BLOCK
    valid = (dist_i >= 0) & (dist_i <= WINDOW) & ((kj >= BLOCK) | (i > 0))
    dist = dist_i.astype(_F32)
    for kvh in range(N_KV_HEADS):
        ks = slice(kvh * HEAD_DIM, (kvh + 1) * HEAD_DIM)
        vs = slice(KV_WIDTH + kvh * HEAD_DIM, KV_WIDTH + (kvh + 1) * HEAD_DIM)
        k = jnp.concatenate([kvp_ref[:, ks], kvc_ref[:, ks]], axis=0).astype(_BF)
        v = jnp.concatenate([kvp_ref[:, vs], kvc_ref[:, vs]], axis=0).astype(_BF)
        for gq in range(GROUP):
            h = kvh * GROUP + gq
            hs = slice(h * HEAD_DIM, (h + 1) * HEAD_DIM)
            s = _dot_nt(q_ref[:, hs], k)
            s = jnp.where(valid, s - _SLOPES[h] * dist, NEG)
            p, denom = _softmax_sink(s, sinks_ref[0, h])
            acc_ref[:, hs] = _dot(p.astype(_BF), v) / denom
    o_ref[...] = _rms(acc_ref[...], g_ref[...]).astype(_BF)


def _attn_prompt(q, kv, sinks, g, *, batch, seq):
    nb = seq // BLOCK
    return pl.pallas_call(
        _attn_prompt_kernel,
        grid=(batch, nb),
        in_specs=[
            pl.BlockSpec(memory_space=pltpu.SMEM),
            pl.BlockSpec((BLOCK, ATTN_WIDTH), lambda b, i: (b * nb + i, 0)),
            pl.BlockSpec((BLOCK, 2 * KV_WIDTH), lambda b, i: (b * nb + i, 0)),
            pl.BlockSpec((BLOCK, 2 * KV_WIDTH), lambda b, i: (b * nb + jnp.maximum(i - 1, 0), 0)),
            pl.BlockSpec((1, ATTN_WIDTH), lambda b, i: (0, 0)),
        ],
        out_specs=pl.BlockSpec((BLOCK, ATTN_WIDTH), lambda b, i: (b * nb + i, 0)),
        out_shape=jax.ShapeDtypeStruct((batch * seq, ATTN_WIDTH), _BF),
        scratch_shapes=[pltpu.VMEM((BLOCK, ATTN_WIDTH), _F32)],
        compiler_params=pltpu.CompilerParams(dimension_semantics=("arbitrary", "arbitrary")),
        name="attn_prompt",
    )(sinks, q, kv, kv, g)


def _round_bf16(x):
    return x.astype(_BF).astype(_F32)


def _attn_sample_kernel(sinks_ref, q_ref, kvn_ref, ck_ref, cv_ref, g_ref,
                        o_ref, kwin_ref, vwin_ref, acc_ref, *, nseq, t_new):
    rows = GROUP * t_new
    r = lax.broadcasted_iota(jnp.int32, (rows, 1), 0)
    t = lax.broadcasted_iota(jnp.int32, (rows, WINDOW + t_new), 0) % t_new
    kj = lax.broadcasted_iota(jnp.int32, (rows, WINDOW + t_new), 1)
    dist_i = t + WINDOW - kj
    valid = (dist_i >= 0) & (dist_i <= WINDOW)
    dist = dist_i.astype(_F32)
    for kvh in range(N_KV_HEADS):
        slope = jnp.full((rows, 1), _SLOPES[kvh * GROUP], _F32)
        sink = jnp.full((rows, 1), sinks_ref[0, kvh * GROUP], _F32)
        for gq in range(1, GROUP):
            slope = jnp.where(r >= gq * t_new, _SLOPES[kvh * GROUP + gq], slope)
            sink = jnp.where(r >= gq * t_new, sinks_ref[0, kvh * GROUP + gq], sink)
        bias = slope * dist
        ks = slice(kvh * HEAD_DIM, (kvh + 1) * HEAD_DIM)
        vs = slice(KV_WIDTH + kvh * HEAD_DIM, KV_WIDTH + (kvh + 1) * HEAD_DIM)
        for b in range(nseq):
            rs = slice(b * t_new, (b + 1) * t_new)
            k = _round_bf16(jnp.concatenate([ck_ref[b, :, ks], kvn_ref[rs, ks]], axis=0))
            v = _round_bf16(jnp.concatenate([cv_ref[b, :, ks], kvn_ref[rs, vs]], axis=0))
            qs = jnp.concatenate(
                [q_ref[rs, (kvh * GROUP + gq) * HEAD_DIM:(kvh * GROUP + gq + 1) * HEAD_DIM]
                 for gq in range(GROUP)], axis=0)
            s = _dot_nt(_round_bf16(qs), k)
            s = jnp.where(valid, s - bias, NEG)
            p, denom = _softmax_sink(s, sink)
            o = _dot(_round_bf16(p), v) / denom
            for gq in range(GROUP):
                h = kvh * GROUP + gq
                acc_ref[rs, h * HEAD_DIM:(h + 1) * HEAD_DIM] = o[gq * t_new:(gq + 1) * t_new, :]
    o_ref[...] = _rms(acc_ref[...], g_ref[...]).astype(_BF)
    kwin_ref[:, 0:WINDOW - t_new, :] = ck_ref[:, t_new:, :]
    vwin_ref[:, 0:WINDOW - t_new, :] = cv_ref[:, t_new:, :]
    kwin_ref[:, WINDOW - t_new:, :] = kvn_ref[:, 0:KV_WIDTH].reshape(nseq, t_new, KV_WIDTH)
    vwin_ref[:, WINDOW - t_new:, :] = kvn_ref[:, KV_WIDTH:].reshape(nseq, t_new, KV_WIDTH)


def _attn_sample(q, kv, cache_k, cache_v, sinks, g, *, nseq_total, t_new, nseq=8):
    rows = nseq * t_new
    cache_spec = pl.BlockSpec((nseq, WINDOW, KV_WIDTH), lambda i: (i, 0, 0))
    return pl.pallas_call(
        functools.partial(_attn_sample_kernel, nseq=nseq, t_new=t_new),
        grid=(nseq_total // nseq,),
        in_specs=[
            pl.BlockSpec(memory_space=pltpu.SMEM),
            pl.BlockSpec((rows, ATTN_WIDTH), lambda i: (i, 0)),
            pl.BlockSpec((rows, 2 * KV_WIDTH), lambda i: (i, 0)),
            cache_spec,
            cache_spec,
            pl.BlockSpec((1, ATTN_WIDTH), lambda i: (0, 0)),
        ],
        out_specs=[pl.BlockSpec((rows, ATTN_WIDTH), lambda i: (i, 0)), cache_spec, cache_spec],
        out_shape=[
            jax.ShapeDtypeStruct((nseq_total * t_new, ATTN_WIDTH), _BF),
            jax.ShapeDtypeStruct(cache_k.shape, _F32),
            jax.ShapeDtypeStruct(cache_v.shape, _F32),
        ],
        scratch_shapes=[pltpu.VMEM((rows, ATTN_WIDTH), _F32)],
        compiler_params=pltpu.CompilerParams(dimension_semantics=("arbitrary",)),
        name="attn_sample",
    )(sinks, q, kv, cache_k, cache_v, g)


def _outproj_kernel(x_ref, na_ref, nb_ref, wa_ref, wb_ref, o_ref):
    o_ref[...] = x_ref[...] + _dot(na_ref[...], wa_ref[...]) + _dot(nb_ref[...], wb_ref[...])


def _outproj(x, na, nb, w_out):
    n = x.shape[0]
    tm = ROW_TILE
    return pl.pallas_call(
        _outproj_kernel,
        grid=(n // tm,),
        in_specs=[
            pl.BlockSpec((tm, D_MODEL), lambda i: (i, 0)),
            pl.BlockSpec((tm, ATTN_WIDTH), lambda i: (i, 0)),
            pl.BlockSpec((tm, CONV_CH), lambda i: (i, 0)),
            pl.BlockSpec((ATTN_WIDTH, D_MODEL), lambda i: (0, 0)),
            pl.BlockSpec((CONV_CH, D_MODEL), lambda i: (1, 0)),
        ],
        out_specs=pl.BlockSpec((tm, D_MODEL), lambda i: (i, 0)),
        out_shape=jax.ShapeDtypeStruct((n, D_MODEL), _F32),
        compiler_params=pltpu.CompilerParams(
            dimension_semantics=("arbitrary",), vmem_limit_bytes=VMEM_LIMIT),
        name="outproj",
    )(x, na, nb, w_out, w_out)


def _ffn_kernel(*refs, seq_tiles):
    carry_mode = seq_tiles is not None
    if carry_mode:
        (x_ref, g_ref, wg_ref, wu_ref, cw_ref, cb_ref, wd_ref, gf_ref,
         o_ref, st_ref, h_scr, acc_scr, ext_scr, carry_scr) = refs
    else:
        (x_ref, g_ref, wg_ref, wu_ref, cw_ref, cb_ref, wd_ref, gf_ref, state_ref,
         o_ref, st_ref, h_scr, acc_scr) = refs
    i = pl.program_id(0)
    j = pl.program_id(1)

    @pl.when(j == 0)
    def _():
        h_scr[...] = _rms(x_ref[...], g_ref[...]).astype(_BF)

    h = h_scr[...]
    gate = _dot(h, wg_ref[...])
    if carry_mode:
        conv = _conv3_carry(gate, cw_ref, ext_scr, carry_scr.at[j], i % seq_tiles == 0)
        st_ref[0] = gate[gate.shape[0] - 2:, :]
    else:
        conv, last2 = _conv3_seqs(gate, cw_ref, state_ref)
        st_ref[...] = last2
    a = conv + cb_ref[...]
    act = (a / (1.0 + jnp.exp(-a))) * _dot(h, wu_ref[...])
    down = _dot(act.astype(_BF), wd_ref[...])

    @pl.when(j == 0)
    def _():
        acc_scr[...] = down

    @pl.when(j > 0)
    def _():
        acc_scr[...] += down

    @pl.when(j == FF_STEPS - 1)
    def _():
        o_ref[...] = _rms(x_ref[...] + acc_scr[...], gf_ref[...])


def _ffn(x, g, w_gate, w_up, conv_w, conv_b, w_down, g_final, state, *, seq_len):
    n = x.shape[0]
    tm = ROW_TILE
    nt = n // tm
    carry_mode = seq_len >= tm
    seq_tiles = seq_len // tm if carry_mode else None
    in_specs = [
        pl.BlockSpec((tm, D_MODEL), lambda i, j: (i, 0)),
        pl.BlockSpec((1, D_MODEL), lambda i, j: (0, 0)),
        pl.BlockSpec((D_MODEL, COL_TILE), lambda i, j: (0, j)),
        pl.BlockSpec((D_MODEL, COL_TILE), lambda i, j: (0, j)),
        pl.BlockSpec((3, COL_TILE), lambda i, j: (0, j)),
        pl.BlockSpec((1, COL_TILE), lambda i, j: (0, j)),
        pl.BlockSpec((COL_TILE, D_MODEL), lambda i, j: (j, 0)),
        pl.BlockSpec((1, D_MODEL), lambda i, j: (0, 0)),
    ]
    args = [x, g, w_gate, w_up, conv_w, conv_b, w_down, g_final]
    scratch = [pltpu.VMEM((tm, D_MODEL), _BF), pltpu.VMEM((tm, D_MODEL), _F32)]
    if carry_mode:
        st_shape = jax.ShapeDtypeStruct((nt, 2, D_FF), _F32)
        st_spec = pl.BlockSpec((1, 2, COL_TILE), lambda i, j: (i, 0, j))
        scratch += [pltpu.VMEM((tm + SUBLANES, COL_TILE), _F32),
                    pltpu.VMEM((FF_STEPS, SUBLANES, COL_TILE), _F32)]
    else:
        nseq = tm // seq_len
        st_shape = jax.ShapeDtypeStruct((n // seq_len, 2, D_FF), _F32)
        st_spec = pl.BlockSpec((nseq, 2, COL_TILE), lambda i, j: (i, 0, j))
        in_specs.append(st_spec)
        args.append(state)
    return pl.pallas_call(
        functools.partial(_ffn_kernel, seq_tiles=seq_tiles),
        grid=(nt, FF_STEPS),
        in_specs=in_specs,
        out_specs=[pl.BlockSpec((tm, D_MODEL), lambda i, j: (i, 0)), st_spec],
        out_shape=[jax.ShapeDtypeStruct((n, D_MODEL), _F32), st_shape],
        scratch_shapes=scratch,
        compiler_params=pltpu.CompilerParams(
            dimension_semantics=("arbitrary", "arbitrary"), vmem_limit_bytes=VMEM_LIMIT),
        name="ffn_carry" if carry_mode else "ffn_seqs",
    )(*args)


def kernel(x_prompt, x_sample, cache_k_window, cache_v_window, state_conv, state_ffn_conv, g_attn_norm, w_in, attn_sinks, conv_w, g_out_attn, g_out_conv, w_out, g_ffn_norm, w_gate, w_up, ffn_conv_w, ffn_conv_b, w_down, g_final):
    assert w_in.shape[0] == 1, "single layer"
    bp, sp, _ = x_prompt.shape
    bs, ts, _ = x_sample.shape
    assert sp % ROW_TILE == 0 and (bs * ts) % ROW_TILE == 0 and ts == SUBLANES

    w_in_b = w_in[0].astype(_BF)
    w_out_b = w_out[0].astype(_BF)
    w_gate_b = w_gate[0].astype(_BF)
    w_up_b = w_up[0].astype(_BF)
    w_down_b = w_down[0].astype(_BF)
    g_attn = g_attn_norm[0].reshape(1, D_MODEL)
    g_ffn = g_ffn_norm[0].reshape(1, D_MODEL)
    g_fin = g_final.reshape(1, D_MODEL)
    g_oa = g_out_attn[0].reshape(1, ATTN_WIDTH)
    g_oc = g_out_conv[0].reshape(1, CONV_CH)
    sinks = attn_sinks[0].reshape(1, N_HEADS)
    cw = conv_w[0]
    fcw = ffn_conv_w[0]
    fcb = ffn_conv_b[0].reshape(1, D_FF)

    xp = x_prompt.reshape(bp * sp, D_MODEL)
    q_p, kv_p, nb_p, cst_p = _inproj(xp, g_attn, w_in_b, cw, g_oc, None, seq_len=sp, q_dtype=_BF)
    na_p = _attn_prompt(q_p, kv_p, sinks, g_oa, batch=bp, seq=sp)
    x1_p = _outproj(xp, na_p, nb_p, w_out_b)
    y_p, fst_p = _ffn(x1_p, g_ffn, w_gate_b, w_up_b, fcw, fcb, w_down_b, g_fin, None, seq_len=sp)

    seq_tiles = sp // ROW_TILE
    kv_p3 = kv_p.reshape(bp, sp, 2 * KV_WIDTH)[:, sp - WINDOW:, :]
    k_win_p = kv_p3[:, :, :KV_WIDTH].reshape(1, bp, WINDOW, N_KV_HEADS, HEAD_DIM)
    v_win_p = kv_p3[:, :, KV_WIDTH:].reshape(1, bp, WINDOW, N_KV_HEADS, HEAD_DIM)
    conv_p = cst_p[seq_tiles - 1::seq_tiles][None]
    ffn_conv_p = fst_p[seq_tiles - 1::seq_tiles][None]

    xs = x_sample.reshape(bs * ts, D_MODEL)
    q_s, kv_s, nb_s, cst_s = _inproj(xs, g_attn, w_in_b, cw, g_oc, state_conv[0], seq_len=ts, q_dtype=_F32)
    ck = cache_k_window[0].reshape(bs, WINDOW, KV_WIDTH)
    cv = cache_v_window[0].reshape(bs, WINDOW, KV_WIDTH)
    na_s, k_win_s, v_win_s = _attn_sample(q_s, kv_s, ck, cv, sinks, g_oa, nseq_total=bs, t_new=ts)
    x1_s = _outproj(xs, na_s, nb_s, w_out_b)
    y_s, fst_s = _ffn(x1_s, g_ffn, w_gate_b, w_up_b, fcw, fcb, w_down_b, g_fin, state_ffn_conv[0], seq_len=ts)

    win_shape = (1, bs, WINDOW, N_KV_HEADS, HEAD_DIM)
    return (y_p.reshape(bp, sp, D_MODEL), y_s.reshape(bs, ts, D_MODEL),
            k_win_p, v_win_p, conv_p, ffn_conv_p,
            k_win_s.reshape(win_shape), v_win_s.reshape(win_shape), cst_s[None], fst_s[None])
```

```python
import functools

import jax
import jax.numpy as jnp
from jax import lax
from jax.experimental import pallas as pl
from jax.experimental.pallas import tpu as pltpu

D_MODEL = 2048
ATTN_WIDTH = 1024
CONV_CH = 1024
HEAD_DIM = 64
N_HEADS = 16
N_KV_HEADS = 4
GROUP = 4
KV_WIDTH = 256
WINDOW = 128
BLOCK = 128
D_FF = 5632
EPS = 1e-6
NEG = -1e30

SUBLANES = 8
ROW_TILE = 512
COL_TILE = 512
Q_STEPS = ATTN_WIDTH // COL_TILE
KV_STEP = Q_STEPS
CONV_STEP0 = KV_STEP + 1
CONV_STEPS = CONV_CH // COL_TILE
IN_STEPS = CONV_STEP0 + CONV_STEPS
FF_STEPS = D_FF // COL_TILE
DOWN_STEPS = D_MODEL // COL_TILE
B_BLK0 = (ATTN_WIDTH + 2 * KV_WIDTH) // COL_TILE
C_BLK0 = B_BLK0 + CONV_STEPS
U_BLK0 = C_BLK0 + CONV_STEPS
VMEM_LIMIT = 56 * 1024 * 1024

_SLOPES = [2.0 ** (-8.0 * (h + 1) / N_HEADS) for h in range(N_HEADS)]
_BF = jnp.bfloat16
_F32 = jnp.float32


def _rms(x, g):
    return x * lax.rsqrt(jnp.mean(x * x, axis=-1, keepdims=True) + EPS) * g


def _dot(a, b):
    return jnp.dot(a, b, preferred_element_type=_F32)


def _dot_nt(a, b):
    return lax.dot_general(a, b, (((1,), (1,)), ((), ())), preferred_element_type=_F32)


def _conv3_carry(cur, w_ref, ext_ref, carry_ref, is_seq_start):
    rows = cur.shape[0]
    prev = jnp.where(is_seq_start, 0.0, carry_ref[...])
    ext_ref[0:SUBLANES, :] = prev
    ext_ref[SUBLANES:SUBLANES + rows, :] = cur
    carry_ref[...] = cur[rows - SUBLANES:, :]
    p1 = ext_ref[SUBLANES - 1:SUBLANES - 1 + rows, :]
    p2 = ext_ref[SUBLANES - 2:SUBLANES - 2 + rows, :]
    return w_ref[2:3, :] * cur + w_ref[1:2, :] * p1 + w_ref[0:1, :] * p2


def _conv3_seqs(cur, w_ref, state_ref):
    rows, ch = cur.shape
    nseq = rows // SUBLANES
    cur3 = cur.reshape(nseq, SUBLANES, ch)
    st = state_ref[...]
    t = lax.broadcasted_iota(jnp.int32, cur3.shape, 1)
    s0 = jnp.broadcast_to(st[:, 0:1, :], cur3.shape)
    s1 = jnp.broadcast_to(st[:, 1:2, :], cur3.shape)
    p1 = jnp.where(t == 0, s1, pltpu.roll(cur3, 1, 1))
    p2 = jnp.where(t == 0, s0, jnp.where(t == 1, s1, pltpu.roll(cur3, 2, 1)))
    w = w_ref[...]
    out = w[2:3, :][None] * cur3 + w[1:2, :][None] * p1 + w[0:1, :][None] * p2
    return out.reshape(rows, ch), cur3[:, SUBLANES - 2:, :]


def _inproj_kernel(*refs, seq_tiles, q_dtype):
    carry_mode = seq_tiles is not None
    if carry_mode:
        (x_ref, g_ref, wa_ref, wb_ref, wc_ref, cw_ref, gc_ref,
         q_ref, kv_ref, nb_ref, st_ref, h_scr, y_scr, ss_scr, ext_scr, carry_scr) = refs
    else:
        (x_ref, g_ref, wa_ref, wb_ref, wc_ref, cw_ref, gc_ref, state_ref,
         q_ref, kv_ref, nb_ref, st_ref, h_scr, y_scr, ss_scr) = refs
    i = pl.program_id(0)
    j = pl.program_id(1)

    @pl.when(j == 0)
    def _():
        h_scr[...] = _rms(x_ref[...], g_ref[...]).astype(_BF)

    @pl.when(j < Q_STEPS)
    def _():
        q_ref[...] = (_dot(h_scr[...], wa_ref[...]) * (HEAD_DIM ** -0.5)).astype(q_dtype)

    @pl.when(j == KV_STEP)
    def _():
        kv_ref[...] = _dot(h_scr[...], wa_ref[...])

    for c in range(CONV_STEPS):
        @pl.when(j == CONV_STEP0 + c)
        def _(c=c):
            h = h_scr[...]
            cu = _dot(h, wb_ref[...]) * _dot(h, wc_ref[...])
            if carry_mode:
                conv = _conv3_carry(cu, cw_ref, ext_scr, carry_scr.at[c], i % seq_tiles == 0)
                st_ref[0] = cu[cu.shape[0] - 2:, :]
            else:
                conv, last2 = _conv3_seqs(cu, cw_ref, state_ref)
                st_ref[...] = last2
            y = _dot(h, wa_ref[...]) * conv
            y_scr[c] = y
            ss = jnp.sum(y * y, axis=-1, keepdims=True)
            if c == 0:
                ss_scr[...] = ss
            else:
                ss_scr[...] += ss
            if c == CONV_STEPS - 1:
                inv = lax.rsqrt(ss_scr[...] * (1.0 / CONV_CH) + EPS)
                for cc in range(CONV_STEPS):
                    sl = slice(cc * COL_TILE, (cc + 1) * COL_TILE)
                    nb_ref[:, sl] = (y_scr[cc] * inv * gc_ref[:, sl]).astype(_BF)


def _inproj(x, g, w_in, conv_w, g_conv, state, *, seq_len, q_dtype):
    n = x.shape[0]
    tm = ROW_TILE
    nt = n // tm
    carry_mode = seq_len >= tm
    seq_tiles = seq_len // tm if carry_mode else None

    def wa_map(i, j):
        return (0, j)

    def wb_map(i, j):
        return (0, C_BLK0 + jnp.maximum(j - CONV_STEP0, 0))

    def wc_map(i, j):
        return (0, U_BLK0 + jnp.maximum(j - CONV_STEP0, 0))

    def cmap(i, j):
        return (0, jnp.maximum(j - CONV_STEP0, 0))

    in_specs = [
        pl.BlockSpec((tm, D_MODEL), lambda i, j: (i, 0)),
        pl.BlockSpec((1, D_MODEL), lambda i, j: (0, 0)),
        pl.BlockSpec((D_MODEL, COL_TILE), wa_map),
        pl.BlockSpec((D_MODEL, COL_TILE), wb_map),
        pl.BlockSpec((D_MODEL, COL_TILE), wc_map),
        pl.BlockSpec((3, COL_TILE), cmap),
        pl.BlockSpec((1, CONV_CH), lambda i, j: (0, 0)),
    ]
    args = [x, g, w_in, w_in, w_in, conv_w, g_conv]
    scratch = [
        pltpu.VMEM((tm, D_MODEL), _BF),
        pltpu.VMEM((CONV_STEPS, tm, COL_TILE), _F32),
        pltpu.VMEM((tm, 1), _F32),
    ]
    if carry_mode:
        st_shape = jax.ShapeDtypeStruct((nt, 2, CONV_CH), _F32)
        st_spec = pl.BlockSpec((1, 2, COL_TILE), lambda i, j: (i, 0, jnp.maximum(j - CONV_STEP0, 0)))
        scratch += [pltpu.VMEM((tm + SUBLANES, COL_TILE), _F32),
                    pltpu.VMEM((CONV_STEPS, SUBLANES, COL_TILE), _F32)]
    else:
        nseq = tm // seq_len
        st_shape = jax.ShapeDtypeStruct((n // seq_len, 2, CONV_CH), _F32)
        st_spec = pl.BlockSpec((nseq, 2, COL_TILE), lambda i, j: (i, 0, jnp.maximum(j - CONV_STEP0, 0)))
        in_specs.append(st_spec)
        args.append(state)
    out_shape = [
        jax.ShapeDtypeStruct((n, ATTN_WIDTH), q_dtype),
        jax.ShapeDtypeStruct((n, 2 * KV_WIDTH), _F32),
        jax.ShapeDtypeStruct((n, CONV_CH), _BF),
        st_shape,
    ]
    out_specs = [
        pl.BlockSpec((tm, COL_TILE), lambda i, j: (i, jnp.minimum(j, Q_STEPS - 1))),
        pl.BlockSpec((tm, 2 * KV_WIDTH), lambda i, j: (i, 0)),
        pl.BlockSpec((tm, CONV_CH), lambda i, j: (i, 0)),
        st_spec,
    ]
    return pl.pallas_call(
        functools.partial(_inproj_kernel, seq_tiles=seq_tiles, q_dtype=q_dtype),
        grid=(nt, IN_STEPS),
        in_specs=in_specs,
        out_specs=out_specs,
        out_shape=out_shape,
        scratch_shapes=scratch,
        compiler_params=pltpu.CompilerParams(
            dimension_semantics=("arbitrary", "arbitrary"), vmem_limit_bytes=VMEM_LIMIT),
        name="inproj_carry" if carry_mode else "inproj_seqs",
    )(*args)


def _softmax_sink(s, sink):
    m = jnp.maximum(jnp.max(s, axis=-1, keepdims=True), sink)
    p = jnp.exp(s - m)
    return p, jnp.sum(p, axis=-1, keepdims=True) + jnp.exp(sink - m)


def _attn_prompt_kernel(sinks_ref, q_ref, kvc_ref, kvp_ref, g_ref, o_ref, acc_ref):
    i = pl.program_id(1)
    qi = lax.broadcasted_iota(jnp.int32, (BLOCK, 2 * BLOCK), 0)
    kj = lax.broadcasted_iota(jnp.int32, (BLOCK, 2 * BLOCK), 1)
    dist_i = qi - kj + BLOCK
    valid = (dist_i >= 0) & (dist_i <= WINDOW) & ((kj >= BLOCK) | (i > 0))
    dist = dist_i.astype(_F32)
    for kvh in range(N_KV_HEADS):
        ks = slice(kvh * HEAD_DIM, (kvh + 1) * HEAD_DIM)
        vs = slice(KV_WIDTH + kvh * HEAD_DIM, KV_WIDTH + (kvh + 1) * HEAD_DIM)
        k = jnp.concatenate([kvp_ref[:, ks], kvc_ref[:, ks]], axis=0).astype(_BF)
        v = jnp.concatenate([kvp_ref[:, vs], kvc_ref[:, vs]], axis=0).astype(_BF)
        for gq in range(GROUP):
            h = kvh * GROUP + gq
            hs = slice(h * HEAD_DIM, (h + 1) * HEAD_DIM)
            s = _dot_nt(q_ref[:, hs], k)
            s = jnp.where(valid, s - _SLOPES[h] * dist, NEG)
            p, denom = _softmax_sink(s, sinks_ref[0, h])
            acc_ref[:, hs] = _dot(p.astype(_BF), v) / denom
    o_ref[...] = _rms(acc_ref[...], g_ref[...]).astype(_BF)


def _attn_prompt(q, kv, sinks, g, *, batch, seq):
    nb = seq // BLOCK
    return pl.pallas_call(
        _attn_prompt_kernel,
        grid=(batch, nb),
        in_specs=[
            pl.BlockSpec(memory_space=pltpu.SMEM),
            pl.BlockSpec((BLOCK, ATTN_WIDTH), lambda b, i: (b * nb + i, 0)),
            pl.BlockSpec((BLOCK, 2 * KV_WIDTH), lambda b, i: (b * nb + i, 0)),
            pl.BlockSpec((BLOCK, 2 * KV_WIDTH), lambda b, i: (b * nb + jnp.maximum(i - 1, 0), 0)),
            pl.BlockSpec((1, ATTN_WIDTH), lambda b, i: (0, 0)),
        ],
        out_specs=pl.BlockSpec((BLOCK, ATTN_WIDTH), lambda b, i: (b * nb + i, 0)),
        out_shape=jax.ShapeDtypeStruct((batch * seq, ATTN_WIDTH), _BF),
        scratch_shapes=[pltpu.VMEM((BLOCK, ATTN_WIDTH), _F32)],
        compiler_params=pltpu.CompilerParams(dimension_semantics=("arbitrary", "arbitrary")),
        name="attn_prompt",
    )(sinks, q, kv, kv, g)


def _round_bf16(x):
    return x.astype(_BF).astype(_F32)


def _attn_sample_kernel(sinks_ref, q_ref, kvn_ref, ck_ref, cv_ref, g_ref,
                        o_ref, kwin_ref, vwin_ref, acc_ref, *, nseq, t_new):
    rows = GROUP * t_new
    r = lax.broadcasted_iota(jnp.int32, (rows, 1), 0)
    t = lax.broadcasted_iota(jnp.int32, (rows, WINDOW + t_new), 0) % t_new
    kj = lax.broadcasted_iota(jnp.int32, (rows, WINDOW + t_new), 1)
    dist_i = t + WINDOW - kj
    valid = (dist_i >= 0) & (dist_i <= WINDOW)
    dist = dist_i.astype(_F32)
    for kvh in range(N_KV_HEADS):
        slope = jnp.full((rows, 1), _SLOPES[kvh * GROUP], _F32)
        sink = jnp.full((rows, 1), sinks_ref[0, kvh * GROUP], _F32)
        for gq in range(1, GROUP):
            slope = jnp.where(r >= gq * t_new, _SLOPES[kvh * GROUP + gq], slope)
            sink = jnp.where(r >= gq * t_new, sinks_ref[0, kvh * GROUP + gq], sink)
        bias = slope * dist
        ks = slice(kvh * HEAD_DIM, (kvh + 1) * HEAD_DIM)
        vs = slice(KV_WIDTH + kvh * HEAD_DIM, KV_WIDTH + (kvh + 1) * HEAD_DIM)
        for b in range(nseq):
            rs = slice(b * t_new, (b + 1) * t_new)
            k = _round_bf16(jnp.concatenate([ck_ref[b, :, ks], kvn_ref[rs, ks]], axis=0))
            v = _round_bf16(jnp.concatenate([cv_ref[b, :, ks], kvn_ref[rs, vs]], axis=0))
            qs = jnp.concatenate(
                [q_ref[rs, (kvh * GROUP + gq) * HEAD_DIM:(kvh * GROUP + gq + 1) * HEAD_DIM]
                 for gq in range(GROUP)], axis=0)
            s = _dot_nt(_round_bf16(qs), k)
            s = jnp.where(valid, s - bias, NEG)
            p, denom = _softmax_sink(s, sink)
            o = _dot(_round_bf16(p), v) / denom
            for gq in range(GROUP):
                h = kvh * GROUP + gq
                acc_ref[rs, h * HEAD_DIM:(h + 1) * HEAD_DIM] = o[gq * t_new:(gq + 1) * t_new, :]
    o_ref[...] = _rms(acc_ref[...], g_ref[...]).astype(_BF)
    kwin_ref[:, 0:WINDOW - t_new, :] = ck_ref[:, t_new:, :]
    vwin_ref[:, 0:WINDOW - t_new, :] = cv_ref[:, t_new:, :]
    kwin_ref[:, WINDOW - t_new:, :] = kvn_ref[:, 0:KV_WIDTH].reshape(nseq, t_new, KV_WIDTH)
    vwin_ref[:, WINDOW - t_new:, :] = kvn_ref[:, KV_WIDTH:].reshape(nseq, t_new, KV_WIDTH)


def _attn_sample(q, kv, cache_k, cache_v, sinks, g, *, nseq_total, t_new, nseq=8):
    rows = nseq * t_new
    cache_spec = pl.BlockSpec((nseq, WINDOW, KV_WIDTH), lambda i: (i, 0, 0))
    return pl.pallas_call(
        functools.partial(_attn_sample_kernel, nseq=nseq, t_new=t_new),
        grid=(nseq_total // nseq,),
        in_specs=[
            pl.BlockSpec(memory_space=pltpu.SMEM),
            pl.BlockSpec((rows, ATTN_WIDTH), lambda i: (i, 0)),
            pl.BlockSpec((rows, 2 * KV_WIDTH), lambda i: (i, 0)),
            cache_spec,
            cache_spec,
            pl.BlockSpec((1, ATTN_WIDTH), lambda i: (0, 0)),
        ],
        out_specs=[pl.BlockSpec((rows, ATTN_WIDTH), lambda i: (i, 0)), cache_spec, cache_spec],
        out_shape=[
            jax.ShapeDtypeStruct((nseq_total * t_new, ATTN_WIDTH), _BF),
            jax.ShapeDtypeStruct(cache_k.shape, _F32),
            jax.ShapeDtypeStruct(cache_v.shape, _F32),
        ],
        scratch_shapes=[pltpu.VMEM((rows, ATTN_WIDTH), _F32)],
        compiler_params=pltpu.CompilerParams(dimension_semantics=("arbitrary",)),
        name="attn_sample",
    )(sinks, q, kv, cache_k, cache_v, g)


def _outproj_kernel(x_ref, na_ref, nb_ref, wa_ref, wb_ref, o_ref):
    o_ref[...] = x_ref[...] + _dot(na_ref[...], wa_ref[...]) + _dot(nb_ref[...], wb_ref[...])


def _outproj(x, na, nb, w_out):
    n = x.shape[0]
    tm = ROW_TILE
    return pl.pallas_call(
        _outproj_kernel,
        grid=(n // tm,),
        in_specs=[
            pl.BlockSpec((tm, D_MODEL), lambda i: (i, 0)),
            pl.BlockSpec((tm, ATTN_WIDTH), lambda i: (i, 0)),
            pl.BlockSpec((tm, CONV_CH), lambda i: (i, 0)),
            pl.BlockSpec((ATTN_WIDTH, D_MODEL), lambda i: (0, 0)),
            pl.BlockSpec((CONV_CH, D_MODEL), lambda i: (1, 0)),
        ],
        out_specs=pl.BlockSpec((tm, D_MODEL), lambda i: (i, 0)),
        out_shape=jax.ShapeDtypeStruct((n, D_MODEL), _F32),
        compiler_params=pltpu.CompilerParams(
            dimension_semantics=("arbitrary",), vmem_limit_bytes=VMEM_LIMIT),
        name="outproj",
    )(x, na, nb, w_out, w_out)


def _ffn_kernel(*refs, seq_tiles):
    carry_mode = seq_tiles is not None
    if carry_mode:
        (x_ref, xc_ref, g_ref, wg_ref, wu_ref, cw_ref, cb_ref, wd_ref, gf_ref,
         o_ref, st_ref, h_scr, act_scr, y_scr, ss_scr, ext_scr, carry_scr) = refs
    else:
        (x_ref, xc_ref, g_ref, wg_ref, wu_ref, cw_ref, cb_ref, wd_ref, gf_ref, state_ref,
         o_ref, st_ref, h_scr, act_scr, y_scr, ss_scr) = refs
    i = pl.program_id(0)
    j = pl.program_id(1)

    @pl.when(j == 0)
    def _():
        h_scr[...] = _rms(x_ref[...], g_ref[...]).astype(_BF)

    @pl.when(j < FF_STEPS)
    def _():
        h = h_scr[...]
        gate = _dot(h, wg_ref[...])
        if carry_mode:
            conv = _conv3_carry(gate, cw_ref, ext_scr, carry_scr.at[j], i % seq_tiles == 0)
            st_ref[0] = gate[gate.shape[0] - 2:, :]
        else:
            conv, last2 = _conv3_seqs(gate, cw_ref, state_ref)
            st_ref[...] = last2
        a = conv + cb_ref[...]
        act_scr[j] = ((a / (1.0 + jnp.exp(-a))) * _dot(h, wu_ref[...])).astype(_BF)

    @pl.when(j >= FF_STEPS)
    def _():
        c = j - FF_STEPS
        act = jnp.concatenate([act_scr[k] for k in range(FF_STEPS)], axis=1)
        y = xc_ref[...] + _dot(act, wd_ref[...])
        y_scr[c] = y
        ss = jnp.sum(y * y, axis=-1, keepdims=True)

        @pl.when(c == 0)
        def _():
            ss_scr[...] = ss

        @pl.when(c > 0)
        def _():
            ss_scr[...] += ss

    @pl.when(j == FF_STEPS + DOWN_STEPS - 1)
    def _():
        inv = lax.rsqrt(ss_scr[...] * (1.0 / D_MODEL) + EPS)
        for cc in range(DOWN_STEPS):
            sl = slice(cc * COL_TILE, (cc + 1) * COL_TILE)
            o_ref[:, sl] = y_scr[cc] * inv * gf_ref[:, sl]


def _ffn(x, g, w_gate, w_up, conv_w, conv_b, w_down, g_final, state, *, seq_len):
    n = x.shape[0]
    tm = ROW_TILE
    nt = n // tm
    carry_mode = seq_len >= tm
    seq_tiles = seq_len // tm if carry_mode else None

    def up_map(i, j):
        return (0, jnp.minimum(j, FF_STEPS - 1))

    def down_map(i, j):
        return (0, jnp.maximum(j - FF_STEPS, 0))

    def st_map(i, j):
        return (i, 0, jnp.minimum(j, FF_STEPS - 1))

    in_specs = [
        pl.BlockSpec((tm, D_MODEL), lambda i, j: (i, 0)),
        pl.BlockSpec((tm, COL_TILE), lambda i, j: (i, jnp.maximum(j - FF_STEPS, 0))),
        pl.BlockSpec((1, D_MODEL), lambda i, j: (0, 0)),
        pl.BlockSpec((D_MODEL, COL_TILE), up_map),
        pl.BlockSpec((D_MODEL, COL_TILE), up_map),
        pl.BlockSpec((3, COL_TILE), up_map),
        pl.BlockSpec((1, COL_TILE), up_map),
        pl.BlockSpec((D_FF, COL_TILE), down_map),
        pl.BlockSpec((1, D_MODEL), lambda i, j: (0, 0)),
    ]
    args = [x, x, g, w_gate, w_up, conv_w, conv_b, w_down, g_final]
    scratch = [
        pltpu.VMEM((tm, D_MODEL), _BF),
        pltpu.VMEM((FF_STEPS, tm, COL_TILE), _BF),
        pltpu.VMEM((DOWN_STEPS, tm, COL_TILE), _F32),
        pltpu.VMEM((tm, 1), _F32),
    ]
    if carry_mode:
        st_shape = jax.ShapeDtypeStruct((nt, 2, D_FF), _F32)
        st_spec = pl.BlockSpec((1, 2, COL_TILE), st_map)
        scratch += [pltpu.VMEM((tm + SUBLANES, COL_TILE), _F32),
                    pltpu.VMEM((FF_STEPS, SUBLANES, COL_TILE), _F32)]
    else:
        nseq = tm // seq_len
        st_shape = jax.ShapeDtypeStruct((n // seq_len, 2, D_FF), _F32)
        st_spec = pl.BlockSpec((nseq, 2, COL_TILE), st_map)
        in_specs.append(st_spec)
        args.append(state)
    return pl.pallas_call(
        functools.partial(_ffn_kernel, seq_tiles=seq_tiles),
        grid=(nt, FF_STEPS + DOWN_STEPS),
        in_specs=in_specs,
        out_specs=[pl.BlockSpec((tm, D_MODEL), lambda i, j: (i, 0)), st_spec],
        out_shape=[jax.ShapeDtypeStruct((n, D_MODEL), _F32), st_shape],
        scratch_shapes=scratch,
        compiler_params=pltpu.CompilerParams(
            dimension_semantics=("arbitrary", "arbitrary"), vmem_limit_bytes=VMEM_LIMIT),
        name="ffn_carry" if carry_mode else "ffn_seqs",
    )(*args)


def kernel(x_prompt, x_sample, cache_k_window, cache_v_window, state_conv, state_ffn_conv, g_attn_norm, w_in, attn_sinks, conv_w, g_out_attn, g_out_conv, w_out, g_ffn_norm, w_gate, w_up, ffn_conv_w, ffn_conv_b, w_down, g_final):
    assert w_in.shape[0] == 1, "single layer"
    bp, sp, _ = x_prompt.shape
    bs, ts, _ = x_sample.shape
    assert sp % ROW_TILE == 0 and (bs * ts) % ROW_TILE == 0 and ts == SUBLANES

    w_in_b = w_in[0].astype(_BF)
    w_out_b = w_out[0].astype(_BF)
    w_gate_b = w_gate[0].astype(_BF)
    w_up_b = w_up[0].astype(_BF)
    w_down_b = w_down[0].astype(_BF)
    g_attn = g_attn_norm[0].reshape(1, D_MODEL)
    g_ffn = g_ffn_norm[0].reshape(1, D_MODEL)
    g_fin = g_final.reshape(1, D_MODEL)
    g_oa = g_out_attn[0].reshape(1, ATTN_WIDTH)
    g_oc = g_out_conv[0].reshape(1, CONV_CH)
    sinks = attn_sinks[0].reshape(1, N_HEADS)
    cw = conv_w[0]
    fcw = ffn_conv_w[0]
    fcb = ffn_conv_b[0].reshape(1, D_FF)

    xp = x_prompt.reshape(bp * sp, D_MODEL)
    q_p, kv_p, nb_p, cst_p = _inproj(xp, g_attn, w_in_b, cw, g_oc, None, seq_len=sp, q_dtype=_BF)
    na_p = _attn_prompt(q_p, kv_p, sinks, g_oa, batch=bp, seq=sp)
    x1_p = _outproj(xp, na_p, nb_p, w_out_b)
    y_p, fst_p = _ffn(x1_p, g_ffn, w_gate_b, w_up_b, fcw, fcb, w_down_b, g_fin, None, seq_len=sp)

    seq_tiles = sp // ROW_TILE
    kv_p3 = kv_p.reshape(bp, sp, 2 * KV_WIDTH)[:, sp - WINDOW:, :]
    k_win_p = kv_p3[:, :, :KV_WIDTH].reshape(1, bp, WINDOW, N_KV_HEADS, HEAD_DIM)
    v_win_p = kv_p3[:, :, KV_WIDTH:].reshape(1, bp, WINDOW, N_KV_HEADS, HEAD_DIM)
    conv_p = cst_p[seq_tiles - 1::seq_tiles][None]
    ffn_conv_p = fst_p[seq_tiles - 1::seq_tiles][None]

    xs = x_sample.reshape(bs * ts, D_MODEL)
    q_s, kv_s, nb_s, cst_s = _inproj(xs, g_attn, w_in_b, cw, g_oc, state_conv[0], seq_len=ts, q_dtype=_F32)
    ck = cache_k_window[0].reshape(bs, WINDOW, KV_WIDTH)
    cv = cache_v_window[0].reshape(bs, WINDOW, KV_WIDTH)
    na_s, k_win_s, v_win_s = _attn_sample(q_s, kv_s, ck, cv, sinks, g_oa, nseq_total=bs, t_new=ts)
    x1_s = _outproj(xs, na_s, nb_s, w_out_b)
    y_s, fst_s = _ffn(x1_s, g_ffn, w_gate_b, w_up_b, fcw, fcb, w_down_b, g_fin, state_ffn_conv[0], seq_len=ts)

    win_shape = (1, bs, WINDOW, N_KV_HEADS, HEAD_DIM)
    return (y_p.reshape(bp, sp, D_MODEL), y_s.reshape(bs, ts, D_MODEL),
            k_win_p, v_win_p, conv_p, ffn_conv_p,
            k_win_s.reshape(win_shape), v_win_s.reshape(win_shape), cst_s[None], fst_s[None])
```

```python
import functools

import jax
import jax.numpy as jnp
from jax import lax
from jax.experimental import pallas as pl
from jax.experimental.pallas import tpu as pltpu

D_MODEL = 2048
ATTN_WIDTH = 1024
CONV_CH = 1024
HEAD_DIM = 64
N_HEADS = 16
N_KV_HEADS = 4
GROUP = 4
KV_WIDTH = 256
WINDOW = 128
BLOCK = 128
D_FF = 5632
EPS = 1e-6
NEG = -1e30

SUBLANES = 8
ROW_TILE = 512
COL_TILE = 512
Q_STEPS = ATTN_WIDTH // COL_TILE
KV_STEP = Q_STEPS
CONV_STEP0 = KV_STEP + 1
CONV_STEPS = CONV_CH // COL_TILE
IN_STEPS = CONV_STEP0 + CONV_STEPS
FF_STEPS = D_FF // COL_TILE
DOWN_STEPS = D_MODEL // COL_TILE
B_BLK0 = (ATTN_WIDTH + 2 * KV_WIDTH) // COL_TILE
C_BLK0 = B_BLK0 + CONV_STEPS
U_BLK0 = C_BLK0 + CONV_STEPS
VMEM_LIMIT = 56 * 1024 * 1024

_SLOPES = [2.0 ** (-8.0 * (h + 1) / N_HEADS) for h in range(N_HEADS)]
_BF = jnp.bfloat16
_F32 = jnp.float32


def _rms(x, g):
    return x * lax.rsqrt(jnp.mean(x * x, axis=-1, keepdims=True) + EPS) * g


def _dot(a, b):
    return jnp.dot(a, b, preferred_element_type=_F32)


def _dot_nt(a, b):
    return lax.dot_general(a, b, (((1,), (1,)), ((), ())), preferred_element_type=_F32)


def _conv3_carry(cur, w_ref, ext_ref, carry_ref, is_seq_start):
    rows = cur.shape[0]
    prev = jnp.where(is_seq_start, 0.0, carry_ref[...])
    ext_ref[0:SUBLANES, :] = prev
    ext_ref[SUBLANES:SUBLANES + rows, :] = cur
    carry_ref[...] = cur[rows - SUBLANES:, :]
    p1 = ext_ref[SUBLANES - 1:SUBLANES - 1 + rows, :]
    p2 = ext_ref[SUBLANES - 2:SUBLANES - 2 + rows, :]
    return w_ref[2:3, :] * cur + w_ref[1:2, :] * p1 + w_ref[0:1, :] * p2


def _conv3_seqs(cur, w_ref, state_ref):
    rows, ch = cur.shape
    nseq = rows // SUBLANES
    cur3 = cur.reshape(nseq, SUBLANES, ch)
    st = state_ref[...]
    t = lax.broadcasted_iota(jnp.int32, cur3.shape, 1)
    s0 = jnp.broadcast_to(st[:, 0:1, :], cur3.shape)
    s1 = jnp.broadcast_to(st[:, 1:2, :], cur3.shape)
    p1 = jnp.where(t == 0, s1, pltpu.roll(cur3, 1, 1))
    p2 = jnp.where(t == 0, s0, jnp.where(t == 1, s1, pltpu.roll(cur3, 2, 1)))
    w = w_ref[...]
    out = w[2:3, :][None] * cur3 + w[1:2, :][None] * p1 + w[0:1, :][None] * p2
    return out.reshape(rows, ch), cur3[:, SUBLANES - 2:, :]


def _inproj_kernel(*refs, seq_tiles, q_dtype):
    carry_mode = seq_tiles is not None
    if carry_mode:
        (x_ref, g_ref, wa_ref, wb_ref, wc_ref, cw_ref, gc_ref,
         q_ref, kv_ref, nb_ref, st_ref, h_scr, y_scr, ss_scr, ext_scr, carry_scr) = refs
    else:
        (x_ref, g_ref, wa_ref, wb_ref, wc_ref, cw_ref, gc_ref, state_ref,
         q_ref, kv_ref, nb_ref, st_ref, h_scr, y_scr, ss_scr) = refs
    i = pl.program_id(0)
    j = pl.program_id(1)

    @pl.when(j == 0)
    def _():
        h_scr[...] = _rms(x_ref[...], g_ref[...]).astype(_BF)

    @pl.when(j < Q_STEPS)
    def _():
        q_ref[...] = (_dot(h_scr[...], wa_ref[...]) * (HEAD_DIM ** -0.5)).astype(q_dtype)

    @pl.when(j == KV_STEP)
    def _():
        kv_ref[...] = _dot(h_scr[...], wa_ref[...])

    for c in range(CONV_STEPS):
        @pl.when(j == CONV_STEP0 + c)
        def _(c=c):
            h = h_scr[...]
            cu = _dot(h, wb_ref[...]) * _dot(h, wc_ref[...])
            if carry_mode:
                conv = _conv3_carry(cu, cw_ref, ext_scr, carry_scr.at[c], i % seq_tiles == 0)
                st_ref[0] = cu[cu.shape[0] - 2:, :]
            else:
                conv, last2 = _conv3_seqs(cu, cw_ref, state_ref)
                st_ref[...] = last2
            y = _dot(h, wa_ref[...]) * conv
            y_scr[c] = y
            ss = jnp.sum(y * y, axis=-1, keepdims=True)
            if c == 0:
                ss_scr[...] = ss
            else:
                ss_scr[...] += ss
            if c == CONV_STEPS - 1:
                inv = lax.rsqrt(ss_scr[...] * (1.0 / CONV_CH) + EPS)
                for cc in range(CONV_STEPS):
                    sl = slice(cc * COL_TILE, (cc + 1) * COL_TILE)
                    nb_ref[:, sl] = (y_scr[cc] * inv * gc_ref[:, sl]).astype(_BF)


def _inproj(x, g, w_in, conv_w, g_conv, state, *, seq_len, q_dtype):
    n = x.shape[0]
    tm = ROW_TILE
    nt = n // tm
    carry_mode = seq_len >= tm
    seq_tiles = seq_len // tm if carry_mode else None

    def wa_map(i, j):
        return (j, 0, 0)

    def wb_map(i, j):
        return (C_BLK0 + jnp.maximum(j - CONV_STEP0, 0), 0, 0)

    def wc_map(i, j):
        return (U_BLK0 + jnp.maximum(j - CONV_STEP0, 0), 0, 0)

    def cmap(i, j):
        return (0, jnp.maximum(j - CONV_STEP0, 0))

    in_specs = [
        pl.BlockSpec((tm, D_MODEL), lambda i, j: (i, 0)),
        pl.BlockSpec((1, D_MODEL), lambda i, j: (0, 0)),
        pl.BlockSpec((None, D_MODEL, COL_TILE), wa_map),
        pl.BlockSpec((None, D_MODEL, COL_TILE), wb_map),
        pl.BlockSpec((None, D_MODEL, COL_TILE), wc_map),
        pl.BlockSpec((3, COL_TILE), cmap),
        pl.BlockSpec((1, CONV_CH), lambda i, j: (0, 0)),
    ]
    args = [x, g, w_in, w_in, w_in, conv_w, g_conv]
    scratch = [
        pltpu.VMEM((tm, D_MODEL), _BF),
        pltpu.VMEM((CONV_STEPS, tm, COL_TILE), _F32),
        pltpu.VMEM((tm, 1), _F32),
    ]
    if carry_mode:
        st_shape = jax.ShapeDtypeStruct((nt, 2, CONV_CH), _F32)
        st_spec = pl.BlockSpec((1, 2, COL_TILE), lambda i, j: (i, 0, jnp.maximum(j - CONV_STEP0, 0)))
        scratch += [pltpu.VMEM((tm + SUBLANES, COL_TILE), _F32),
                    pltpu.VMEM((CONV_STEPS, SUBLANES, COL_TILE), _F32)]
    else:
        nseq = tm // seq_len
        st_shape = jax.ShapeDtypeStruct((n // seq_len, 2, CONV_CH), _F32)
        st_spec = pl.BlockSpec((nseq, 2, COL_TILE), lambda i, j: (i, 0, jnp.maximum(j - CONV_STEP0, 0)))
        in_specs.append(st_spec)
        args.append(state)
    out_shape = [
        jax.ShapeDtypeStruct((n, ATTN_WIDTH), q_dtype),
        jax.ShapeDtypeStruct((n, 2 * KV_WIDTH), _F32),
        jax.ShapeDtypeStruct((n, CONV_CH), _BF),
        st_shape,
    ]
    out_specs = [
        pl.BlockSpec((tm, COL_TILE), lambda i, j: (i, jnp.minimum(j, Q_STEPS - 1))),
        pl.BlockSpec((tm, 2 * KV_WIDTH), lambda i, j: (i, 0)),
        pl.BlockSpec((tm, CONV_CH), lambda i, j: (i, 0)),
        st_spec,
    ]
    return pl.pallas_call(
        functools.partial(_inproj_kernel, seq_tiles=seq_tiles, q_dtype=q_dtype),
        grid=(nt, IN_STEPS),
        in_specs=in_specs,
        out_specs=out_specs,
        out_shape=out_shape,
        scratch_shapes=scratch,
        compiler_params=pltpu.CompilerParams(
            dimension_semantics=("arbitrary", "arbitrary"), vmem_limit_bytes=VMEM_LIMIT),
        name="inproj_carry" if carry_mode else "inproj_seqs",
    )(*args)


def _softmax_sink(s, sink):
    m = jnp.maximum(jnp.max(s, axis=-1, keepdims=True), sink)
    p = jnp.exp(s - m)
    return p, jnp.sum(p, axis=-1, keepdims=True) + jnp.exp(sink - m)


def _attn_prompt_kernel(sinks_ref, q_ref, kvc_ref, kvp_ref, g_ref, o_ref, acc_ref):
    i = pl.program_id(1)
    qi = lax.broadcasted_iota(jnp.int32, (BLOCK, 2 * BLOCK), 0)
    kj = lax.broadcasted_iota(jnp.int32, (BLOCK, 2 * BLOCK), 1)
    dist_i = qi - kj + BLOCK
    valid = (dist_i >= 0) & (dist_i <= WINDOW) & ((kj >= BLOCK) | (i > 0))
    dist = dist_i.astype(_F32)
    for kvh in range(N_KV_HEADS):
        ks = slice(kvh * HEAD_DIM, (kvh + 1) * HEAD_DIM)
        vs = slice(KV_WIDTH + kvh * HEAD_DIM, KV_WIDTH + (kvh + 1) * HEAD_DIM)
        k = jnp.concatenate([kvp_ref[:, ks], kvc_ref[:, ks]], axis=0).astype(_BF)
        v = jnp.concatenate([kvp_ref[:, vs], kvc_ref[:, vs]], axis=0).astype(_BF)
        for gq in range(GROUP):
            h = kvh * GROUP + gq
            hs = slice(h * HEAD_DIM, (h + 1) * HEAD_DIM)
            s = _dot_nt(q_ref[:, hs], k)
            s = jnp.where(valid, s - _SLOPES[h] * dist, NEG)
            p, denom = _softmax_sink(s, sinks_ref[0, h])
            acc_ref[:, hs] = _dot(p.astype(_BF), v) / denom
    o_ref[...] = _rms(acc_ref[...], g_ref[...]).astype(_BF)


def _attn_prompt(q, kv, sinks, g, *, batch, seq):
    nb = seq // BLOCK
    return pl.pallas_call(
        _attn_prompt_kernel,
        grid=(batch, nb),
        in_specs=[
            pl.BlockSpec(memory_space=pltpu.SMEM),
            pl.BlockSpec((BLOCK, ATTN_WIDTH), lambda b, i: (b * nb + i, 0)),
            pl.BlockSpec((BLOCK, 2 * KV_WIDTH), lambda b, i: (b * nb + i, 0)),
            pl.BlockSpec((BLOCK, 2 * KV_WIDTH), lambda b, i: (b * nb + jnp.maximum(i - 1, 0), 0)),
            pl.BlockSpec((1, ATTN_WIDTH), lambda b, i: (0, 0)),
        ],
        out_specs=pl.BlockSpec((BLOCK, ATTN_WIDTH), lambda b, i: (b * nb + i, 0)),
        out_shape=jax.ShapeDtypeStruct((batch * seq, ATTN_WIDTH), _BF),
        scratch_shapes=[pltpu.VMEM((BLOCK, ATTN_WIDTH), _F32)],
        compiler_params=pltpu.CompilerParams(dimension_semantics=("arbitrary", "arbitrary")),
        name="attn_prompt",
    )(sinks, q, kv, kv, g)


def _round_bf16(x):
    return x.astype(_BF).astype(_F32)


def _attn_sample_kernel(sinks_ref, q_ref, kvn_ref, ck_ref, cv_ref, g_ref,
                        o_ref, kwin_ref, vwin_ref, acc_ref, *, nseq, t_new):
    rows = GROUP * t_new
    r = lax.broadcasted_iota(jnp.int32, (rows, 1), 0)
    t = lax.broadcasted_iota(jnp.int32, (rows, WINDOW + t_new), 0) % t_new
    kj = lax.broadcasted_iota(jnp.int32, (rows, WINDOW + t_new), 1)
    dist_i = t + WINDOW - kj
    valid = (dist_i >= 0) & (dist_i <= WINDOW)
    dist = dist_i.astype(_F32)
    for kvh in range(N_KV_HEADS):
        slope = jnp.full((rows, 1), _SLOPES[kvh * GROUP], _F32)
        sink = jnp.full((rows, 1), sinks_ref[0, kvh * GROUP], _F32)
        for gq in range(1, GROUP):
            slope = jnp.where(r >= gq * t_new, _SLOPES[kvh * GROUP + gq], slope)
            sink = jnp.where(r >= gq * t_new, sinks_ref[0, kvh * GROUP + gq], sink)
        bias = slope * dist
        ks = slice(kvh * HEAD_DIM, (kvh + 1) * HEAD_DIM)
        vs = slice(KV_WIDTH + kvh * HEAD_DIM, KV_WIDTH + (kvh + 1) * HEAD_DIM)
        for b in range(nseq):
            rs = slice(b * t_new, (b + 1) * t_new)
            k = _round_bf16(jnp.concatenate([ck_ref[b, :, ks], kvn_ref[rs, ks]], axis=0))
            v = _round_bf16(jnp.concatenate([cv_ref[b, :, ks], kvn_ref[rs, vs]], axis=0))
            qs = jnp.concatenate(
                [q_ref[rs, (kvh * GROUP + gq) * HEAD_DIM:(kvh * GROUP + gq + 1) * HEAD_DIM]
                 for gq in range(GROUP)], axis=0)
            s = _dot_nt(_round_bf16(qs), k)
            s = jnp.where(valid, s - bias, NEG)
            p, denom = _softmax_sink(s, sink)
            o = _dot(_round_bf16(p), v) / denom
            for gq in range(GROUP):
                h = kvh * GROUP + gq
                acc_ref[rs, h * HEAD_DIM:(h + 1) * HEAD_DIM] = o[gq * t_new:(gq + 1) * t_new, :]
    o_ref[...] = _rms(acc_ref[...], g_ref[...]).astype(_BF)
    kwin_ref[:, 0:WINDOW - t_new, :] = ck_ref[:, t_new:, :]
    vwin_ref[:, 0:WINDOW - t_new, :] = cv_ref[:, t_new:, :]
    kwin_ref[:, WINDOW - t_new:, :] = kvn_ref[:, 0:KV_WIDTH].reshape(nseq, t_new, KV_WIDTH)
    vwin_ref[:, WINDOW - t_new:, :] = kvn_ref[:, KV_WIDTH:].reshape(nseq, t_new, KV_WIDTH)


def _attn_sample(q, kv, cache_k, cache_v, sinks, g, *, nseq_total, t_new, nseq=8):
    rows = nseq * t_new
    cache_spec = pl.BlockSpec((nseq, WINDOW, KV_WIDTH), lambda i: (i, 0, 0))
    return pl.pallas_call(
        functools.partial(_attn_sample_kernel, nseq=nseq, t_new=t_new),
        grid=(nseq_total // nseq,),
        in_specs=[
            pl.BlockSpec(memory_space=pltpu.SMEM),
            pl.BlockSpec((rows, ATTN_WIDTH), lambda i: (i, 0)),
            pl.BlockSpec((rows, 2 * KV_WIDTH), lambda i: (i, 0)),
            cache_spec,
            cache_spec,
            pl.BlockSpec((1, ATTN_WIDTH), lambda i: (0, 0)),
        ],
        out_specs=[pl.BlockSpec((rows, ATTN_WIDTH), lambda i: (i, 0)), cache_spec, cache_spec],
        out_shape=[
            jax.ShapeDtypeStruct((nseq_total * t_new, ATTN_WIDTH), _BF),
            jax.ShapeDtypeStruct(cache_k.shape, _F32),
            jax.ShapeDtypeStruct(cache_v.shape, _F32),
        ],
        scratch_shapes=[pltpu.VMEM((rows, ATTN_WIDTH), _F32)],
        compiler_params=pltpu.CompilerParams(dimension_semantics=("arbitrary",)),
        name="attn_sample",
    )(sinks, q, kv, cache_k, cache_v, g)


def _outproj_kernel(x_ref, na_ref, nb_ref, wa_ref, wb_ref, o_ref):
    o_ref[...] = x_ref[...] + _dot(na_ref[...], wa_ref[...]) + _dot(nb_ref[...], wb_ref[...])


def _outproj(x, na, nb, w_out):
    n = x.shape[0]
    tm = ROW_TILE
    return pl.pallas_call(
        _outproj_kernel,
        grid=(n // tm,),
        in_specs=[
            pl.BlockSpec((tm, D_MODEL), lambda i: (i, 0)),
            pl.BlockSpec((tm, ATTN_WIDTH), lambda i: (i, 0)),
            pl.BlockSpec((tm, CONV_CH), lambda i: (i, 0)),
            pl.BlockSpec((ATTN_WIDTH, D_MODEL), lambda i: (0, 0)),
            pl.BlockSpec((CONV_CH, D_MODEL), lambda i: (1, 0)),
        ],
        out_specs=pl.BlockSpec((tm, D_MODEL), lambda i: (i, 0)),
        out_shape=jax.ShapeDtypeStruct((n, D_MODEL), _F32),
        compiler_params=pltpu.CompilerParams(
            dimension_semantics=("arbitrary",), vmem_limit_bytes=VMEM_LIMIT),
        name="outproj",
    )(x, na, nb, w_out, w_out)


def _ffn_kernel(*refs, seq_tiles):
    carry_mode = seq_tiles is not None
    if carry_mode:
        (x_ref, xc_ref, g_ref, wg_ref, wu_ref, cw_ref, cb_ref, wd_ref, gf_ref,
         o_ref, st_ref, h_scr, act_scr, y_scr, ss_scr, ext_scr, carry_scr) = refs
    else:
        (x_ref, xc_ref, g_ref, wg_ref, wu_ref, cw_ref, cb_ref, wd_ref, gf_ref, state_ref,
         o_ref, st_ref, h_scr, act_scr, y_scr, ss_scr) = refs
    i = pl.program_id(0)
    j = pl.program_id(1)

    @pl.when(j == 0)
    def _():
        h_scr[...] = _rms(x_ref[...], g_ref[...]).astype(_BF)

    @pl.when(j < FF_STEPS)
    def _():
        h = h_scr[...]
        gate = _dot(h, wg_ref[...])
        if carry_mode:
            conv = _conv3_carry(gate, cw_ref, ext_scr, carry_scr.at[j], i % seq_tiles == 0)
            st_ref[0] = gate[gate.shape[0] - 2:, :]
        else:
            conv, last2 = _conv3_seqs(gate, cw_ref, state_ref)
            st_ref[...] = last2
        a = conv + cb_ref[...]
        act_scr[j] = ((a / (1.0 + jnp.exp(-a))) * _dot(h, wu_ref[...])).astype(_BF)

    @pl.when(j >= FF_STEPS)
    def _():
        c = j - FF_STEPS
        act = jnp.concatenate([act_scr[k] for k in range(FF_STEPS)], axis=1)
        y = xc_ref[...] + _dot(act, wd_ref[...])
        y_scr[c] = y
        ss = jnp.sum(y * y, axis=-1, keepdims=True)

        @pl.when(c == 0)
        def _():
            ss_scr[...] = ss

        @pl.when(c > 0)
        def _():
            ss_scr[...] += ss

    @pl.when(j == FF_STEPS + DOWN_STEPS - 1)
    def _():
        inv = lax.rsqrt(ss_scr[...] * (1.0 / D_MODEL) + EPS)
        for cc in range(DOWN_STEPS):
            sl = slice(cc * COL_TILE, (cc + 1) * COL_TILE)
            o_ref[:, sl] = y_scr[cc] * inv * gf_ref[:, sl]


def _ffn(x, g, w_gate, w_up, conv_w, conv_b, w_down, g_final, state, *, seq_len):
    n = x.shape[0]
    tm = ROW_TILE
    nt = n // tm
    carry_mode = seq_len >= tm
    seq_tiles = seq_len // tm if carry_mode else None

    def up_map(i, j):
        return (0, jnp.minimum(j, FF_STEPS - 1))

    def wup_map(i, j):
        return (jnp.minimum(j, FF_STEPS - 1), 0, 0)

    def wdown_map(i, j):
        return (jnp.maximum(j - FF_STEPS, 0), 0, 0)

    def st_map(i, j):
        return (i, 0, jnp.minimum(j, FF_STEPS - 1))

    in_specs = [
        pl.BlockSpec((tm, D_MODEL), lambda i, j: (i, 0)),
        pl.BlockSpec((tm, COL_TILE), lambda i, j: (i, jnp.maximum(j - FF_STEPS, 0))),
        pl.BlockSpec((1, D_MODEL), lambda i, j: (0, 0)),
        pl.BlockSpec((None, D_MODEL, COL_TILE), wup_map),
        pl.BlockSpec((None, D_MODEL, COL_TILE), wup_map),
        pl.BlockSpec((3, COL_TILE), up_map),
        pl.BlockSpec((1, COL_TILE), up_map),
        pl.BlockSpec((None, D_FF, COL_TILE), wdown_map),
        pl.BlockSpec((1, D_MODEL), lambda i, j: (0, 0)),
    ]
    args = [x, x, g, w_gate, w_up, conv_w, conv_b, w_down, g_final]
    scratch = [
        pltpu.VMEM((tm, D_MODEL), _BF),
        pltpu.VMEM((FF_STEPS, tm, COL_TILE), _BF),
        pltpu.VMEM((DOWN_STEPS, tm, COL_TILE), _F32),
        pltpu.VMEM((tm, 1), _F32),
    ]
    if carry_mode:
        st_shape = jax.ShapeDtypeStruct((nt, 2, D_FF), _F32)
        st_spec = pl.BlockSpec((1, 2, COL_TILE), st_map)
        scratch += [pltpu.VMEM((tm + SUBLANES, COL_TILE), _F32),
                    pltpu.VMEM((FF_STEPS, SUBLANES, COL_TILE), _F32)]
    else:
        nseq = tm // seq_len
        st_shape = jax.ShapeDtypeStruct((n // seq_len, 2, D_FF), _F32)
        st_spec = pl.BlockSpec((nseq, 2, COL_TILE), st_map)
        in_specs.append(st_spec)
        args.append(state)
    return pl.pallas_call(
        functools.partial(_ffn_kernel, seq_tiles=seq_tiles),
        grid=(nt, FF_STEPS + DOWN_STEPS),
        in_specs=in_specs,
        out_specs=[pl.BlockSpec((tm, D_MODEL), lambda i, j: (i, 0)), st_spec],
        out_shape=[jax.ShapeDtypeStruct((n, D_MODEL), _F32), st_shape],
        scratch_shapes=scratch,
        compiler_params=pltpu.CompilerParams(
            dimension_semantics=("arbitrary", "arbitrary"), vmem_limit_bytes=VMEM_LIMIT),
        name="ffn_carry" if carry_mode else "ffn_seqs",
    )(*args)


def _col_tiles(w):
    k, n = w.shape
    return w.astype(_BF).reshape(k, n // COL_TILE, COL_TILE).transpose(1, 0, 2)


def kernel(x_prompt, x_sample, cache_k_window, cache_v_window, state_conv, state_ffn_conv, g_attn_norm, w_in, attn_sinks, conv_w, g_out_attn, g_out_conv, w_out, g_ffn_norm, w_gate, w_up, ffn_conv_w, ffn_conv_b, w_down, g_final):
    assert w_in.shape[0] == 1, "single layer"
    bp, sp, _ = x_prompt.shape
    bs, ts, _ = x_sample.shape
    assert sp % ROW_TILE == 0 and (bs * ts) % ROW_TILE == 0 and ts == SUBLANES

    w_in_b = _col_tiles(w_in[0])
    w_out_b = w_out[0].astype(_BF)
    w_gate_b = _col_tiles(w_gate[0])
    w_up_b = _col_tiles(w_up[0])
    w_down_b = _col_tiles(w_down[0])
    g_attn = g_attn_norm[0].reshape(1, D_MODEL)
    g_ffn = g_ffn_norm[0].reshape(1, D_MODEL)
    g_fin = g_final.reshape(1, D_MODEL)
    g_oa = g_out_attn[0].reshape(1, ATTN_WIDTH)
    g_oc = g_out_conv[0].reshape(1, CONV_CH)
    sinks = attn_sinks[0].reshape(1, N_HEADS)
    cw = conv_w[0]
    fcw = ffn_conv_w[0]
    fcb = ffn_conv_b[0].reshape(1, D_FF)

    xp = x_prompt.reshape(bp * sp, D_MODEL)
    q_p, kv_p, nb_p, cst_p = _inproj(xp, g_attn, w_in_b, cw, g_oc, None, seq_len=sp, q_dtype=_BF)
    na_p = _attn_prompt(q_p, kv_p, sinks, g_oa, batch=bp, seq=sp)
    x1_p = _outproj(xp, na_p, nb_p, w_out_b)
    y_p, fst_p = _ffn(x1_p, g_ffn, w_gate_b, w_up_b, fcw, fcb, w_down_b, g_fin, None, seq_len=sp)

    seq_tiles = sp // ROW_TILE
    kv_p3 = kv_p.reshape(bp, sp, 2 * KV_WIDTH)[:, sp - WINDOW:, :]
    k_win_p = kv_p3[:, :, :KV_WIDTH].reshape(1, bp, WINDOW, N_KV_HEADS, HEAD_DIM)
    v_win_p = kv_p3[:, :, KV_WIDTH:].reshape(1, bp, WINDOW, N_KV_HEADS, HEAD_DIM)
    conv_p = cst_p[seq_tiles - 1::seq_tiles][None]
    ffn_conv_p = fst_p[seq_tiles - 1::seq_tiles][None]

    xs = x_sample.reshape(bs * ts, D_MODEL)
    q_s, kv_s, nb_s, cst_s = _inproj(xs, g_attn, w_in_b, cw, g_oc, state_conv[0], seq_len=ts, q_dtype=_F32)
    ck = cache_k_window[0].reshape(bs, WINDOW, KV_WIDTH)
    cv = cache_v_window[0].reshape(bs, WINDOW, KV_WIDTH)
    na_s, k_win_s, v_win_s = _attn_sample(q_s, kv_s, ck, cv, sinks, g_oa, nseq_total=bs, t_new=ts)
    x1_s = _outproj(xs, na_s, nb_s, w_out_b)
    y_s, fst_s = _ffn(x1_s, g_ffn, w_gate_b, w_up_b, fcw, fcb, w_down_b, g_fin, state_ffn_conv[0], seq_len=ts)

    win_shape = (1, bs, WINDOW, N_KV_HEADS, HEAD_DIM)
    return (y_p.reshape(bp, sp, D_MODEL), y_s.reshape(bs, ts, D_MODEL),
            k_win_p, v_win_p, conv_p, ffn_conv_p,
            k_win_s.reshape(win_shape), v_win_s.reshape(win_shape), cst_s[None], fst_s[None])
```

```python
import functools

import jax
import jax.numpy as jnp
from jax import lax
from jax.experimental import pallas as pl
from jax.experimental.pallas import tpu as pltpu

D_MODEL = 2048
ATTN_WIDTH = 1024
CONV_CH = 1024
HEAD_DIM = 64
N_HEADS = 16
N_KV_HEADS = 4
GROUP = 4
KV_WIDTH = 256
WINDOW = 128
BLOCK = 128
D_FF = 5632
EPS = 1e-6
NEG = -1e30

SUBLANES = 8
ROW_TILE = 512
COL_TILE = 512
Q_STEPS = ATTN_WIDTH // COL_TILE
KV_STEP = Q_STEPS
CONV_STEP0 = KV_STEP + 1
CONV_STEPS = CONV_CH // COL_TILE
IN_STEPS = CONV_STEP0 + CONV_STEPS
FF_TILE = 256
FF_STEPS = D_FF // FF_TILE
FF_SUB = 512
FF_ROWS_PROMPT = 2048
FF_ROWS_SAMPLE = 1024
B_BLK0 = (ATTN_WIDTH + 2 * KV_WIDTH) // COL_TILE
C_BLK0 = B_BLK0 + CONV_STEPS
U_BLK0 = C_BLK0 + CONV_STEPS
VMEM_LIMIT = 56 * 1024 * 1024

_SLOPES = [2.0 ** (-8.0 * (h + 1) / N_HEADS) for h in range(N_HEADS)]
_BF = jnp.bfloat16
_F32 = jnp.float32


def _rms(x, g):
    return x * lax.rsqrt(jnp.mean(x * x, axis=-1, keepdims=True) + EPS) * g


def _dot(a, b):
    return jnp.dot(a, b, preferred_element_type=_F32)


def _dot_nt(a, b):
    return lax.dot_general(a, b, (((1,), (1,)), ((), ())), preferred_element_type=_F32)


def _conv3_carry(cur, w_ref, ext_ref, carry_ref, is_seq_start):
    rows = cur.shape[0]
    prev = jnp.where(is_seq_start, 0.0, carry_ref[...])
    ext_ref[0:SUBLANES, :] = prev
    ext_ref[SUBLANES:SUBLANES + rows, :] = cur
    carry_ref[...] = cur[rows - SUBLANES:, :]
    p1 = ext_ref[SUBLANES - 1:SUBLANES - 1 + rows, :]
    p2 = ext_ref[SUBLANES - 2:SUBLANES - 2 + rows, :]
    return w_ref[2:3, :] * cur + w_ref[1:2, :] * p1 + w_ref[0:1, :] * p2


def _conv3_seqs(cur, w_ref, state_ref):
    rows, ch = cur.shape
    nseq = rows // SUBLANES
    cur3 = cur.reshape(nseq, SUBLANES, ch)
    st = state_ref[...]
    t = lax.broadcasted_iota(jnp.int32, cur3.shape, 1)
    s0 = jnp.broadcast_to(st[:, 0:1, :], cur3.shape)
    s1 = jnp.broadcast_to(st[:, 1:2, :], cur3.shape)
    p1 = jnp.where(t == 0, s1, pltpu.roll(cur3, 1, 1))
    p2 = jnp.where(t == 0, s0, jnp.where(t == 1, s1, pltpu.roll(cur3, 2, 1)))
    w = w_ref[...]
    out = w[2:3, :][None] * cur3 + w[1:2, :][None] * p1 + w[0:1, :][None] * p2
    return out.reshape(rows, ch), cur3[:, SUBLANES - 2:, :]


def _inproj_kernel(*refs, seq_tiles, q_dtype):
    carry_mode = seq_tiles is not None
    if carry_mode:
        (x_ref, g_ref, wa_ref, wb_ref, wc_ref, cw_ref, gc_ref,
         q_ref, kv_ref, nb_ref, st_ref, h_scr, y_scr, ss_scr, ext_scr, carry_scr) = refs
    else:
        (x_ref, g_ref, wa_ref, wb_ref, wc_ref, cw_ref, gc_ref, state_ref,
         q_ref, kv_ref, nb_ref, st_ref, h_scr, y_scr, ss_scr) = refs
    i = pl.program_id(0)
    j = pl.program_id(1)

    @pl.when(j == 0)
    def _():
        h_scr[...] = _rms(x_ref[...], g_ref[...]).astype(_BF)

    @pl.when(j < Q_STEPS)
    def _():
        q_ref[...] = (_dot(h_scr[...], wa_ref[...]) * (HEAD_DIM ** -0.5)).astype(q_dtype)

    @pl.when(j == KV_STEP)
    def _():
        kv_ref[...] = _dot(h_scr[...], wa_ref[...])

    for c in range(CONV_STEPS):
        @pl.when(j == CONV_STEP0 + c)
        def _(c=c):
            h = h_scr[...]
            cu = _dot(h, wb_ref[...]) * _dot(h, wc_ref[...])
            if carry_mode:
                conv = _conv3_carry(cu, cw_ref, ext_scr, carry_scr.at[c], i % seq_tiles == 0)
                st_ref[0] = cu[cu.shape[0] - 2:, :]
            else:
                conv, last2 = _conv3_seqs(cu, cw_ref, state_ref)
                st_ref[...] = last2
            y = _dot(h, wa_ref[...]) * conv
            y_scr[c] = y
            ss = jnp.sum(y * y, axis=-1, keepdims=True)
            if c == 0:
                ss_scr[...] = ss
            else:
                ss_scr[...] += ss
            if c == CONV_STEPS - 1:
                inv = lax.rsqrt(ss_scr[...] * (1.0 / CONV_CH) + EPS)
                for cc in range(CONV_STEPS):
                    sl = slice(cc * COL_TILE, (cc + 1) * COL_TILE)
                    nb_ref[:, sl] = (y_scr[cc] * inv * gc_ref[:, sl]).astype(_BF)


def _inproj(x, g, w_in, conv_w, g_conv, state, *, seq_len, q_dtype):
    n = x.shape[0]
    tm = ROW_TILE
    nt = n // tm
    carry_mode = seq_len >= tm
    seq_tiles = seq_len // tm if carry_mode else None

    def wa_map(i, j):
        return (0, j)

    def wb_map(i, j):
        return (0, C_BLK0 + jnp.maximum(j - CONV_STEP0, 0))

    def wc_map(i, j):
        return (0, U_BLK0 + jnp.maximum(j - CONV_STEP0, 0))

    def cmap(i, j):
        return (0, jnp.maximum(j - CONV_STEP0, 0))

    in_specs = [
        pl.BlockSpec((tm, D_MODEL), lambda i, j: (i, 0)),
        pl.BlockSpec((1, D_MODEL), lambda i, j: (0, 0)),
        pl.BlockSpec((D_MODEL, COL_TILE), wa_map),
        pl.BlockSpec((D_MODEL, COL_TILE), wb_map),
        pl.BlockSpec((D_MODEL, COL_TILE), wc_map),
        pl.BlockSpec((3, COL_TILE), cmap),
        pl.BlockSpec((1, CONV_CH), lambda i, j: (0, 0)),
    ]
    args = [x, g, w_in, w_in, w_in, conv_w, g_conv]
    scratch = [
        pltpu.VMEM((tm, D_MODEL), _BF),
        pltpu.VMEM((CONV_STEPS, tm, COL_TILE), _F32),
        pltpu.VMEM((tm, 1), _F32),
    ]
    if carry_mode:
        st_shape = jax.ShapeDtypeStruct((nt, 2, CONV_CH), _F32)
        st_spec = pl.BlockSpec((1, 2, COL_TILE), lambda i, j: (i, 0, jnp.maximum(j - CONV_STEP0, 0)))
        scratch += [pltpu.VMEM((tm + SUBLANES, COL_TILE), _F32),
                    pltpu.VMEM((CONV_STEPS, SUBLANES, COL_TILE), _F32)]
    else:
        nseq = tm // seq_len
        st_shape = jax.ShapeDtypeStruct((n // seq_len, 2, CONV_CH), _F32)
        st_spec = pl.BlockSpec((nseq, 2, COL_TILE), lambda i, j: (i, 0, jnp.maximum(j - CONV_STEP0, 0)))
        in_specs.append(st_spec)
        args.append(state)
    out_shape = [
        jax.ShapeDtypeStruct((n, ATTN_WIDTH), q_dtype),
        jax.ShapeDtypeStruct((n, 2 * KV_WIDTH), _F32),
        jax.ShapeDtypeStruct((n, CONV_CH), _BF),
        st_shape,
    ]
    out_specs = [
        pl.BlockSpec((tm, COL_TILE), lambda i, j: (i, jnp.minimum(j, Q_STEPS - 1))),
        pl.BlockSpec((tm, 2 * KV_WIDTH), lambda i, j: (i, 0)),
        pl.BlockSpec((tm, CONV_CH), lambda i, j: (i, 0)),
        st_spec,
    ]
    return pl.pallas_call(
        functools.partial(_inproj_kernel, seq_tiles=seq_tiles, q_dtype=q_dtype),
        grid=(nt, IN_STEPS),
        in_specs=in_specs,
        out_specs=out_specs,
        out_shape=out_shape,
        scratch_shapes=scratch,
        compiler_params=pltpu.CompilerParams(
            dimension_semantics=("arbitrary", "arbitrary"), vmem_limit_bytes=VMEM_LIMIT),
        name="inproj_carry" if carry_mode else "inproj_seqs",
    )(*args)


def _softmax_sink(s, sink):
    m = jnp.maximum(jnp.max(s, axis=-1, keepdims=True), sink)
    p = jnp.exp(s - m)
    return p, jnp.sum(p, axis=-1, keepdims=True) + jnp.exp(sink - m)


def _attn_prompt_kernel(sinks_ref, q_ref, kvc_ref, kvp_ref, g_ref, o_ref, acc_ref):
    i = pl.program_id(1)
    qi = lax.broadcasted_iota(jnp.int32, (BLOCK, 2 * BLOCK), 0)
    kj = lax.broadcasted_iota(jnp.int32, (BLOCK, 2 * BLOCK), 1)
    dist_i = qi - kj + BLOCK
    valid = (dist_i >= 0) & (dist_i <= WINDOW) & ((kj >= BLOCK) | (i > 0))
    dist = dist_i.astype(_F32)
    for kvh in range(N_KV_HEADS):
        ks = slice(kvh * HEAD_DIM, (kvh + 1) * HEAD_DIM)
        vs = slice(KV_WIDTH + kvh * HEAD_DIM, KV_WIDTH + (kvh + 1) * HEAD_DIM)
        k = jnp.concatenate([kvp_ref[:, ks], kvc_ref[:, ks]], axis=0).astype(_BF)
        v = jnp.concatenate([kvp_ref[:, vs], kvc_ref[:, vs]], axis=0).astype(_BF)
        for gq in range(GROUP):
            h = kvh * GROUP + gq
            hs = slice(h * HEAD_DIM, (h + 1) * HEAD_DIM)
            s = _dot_nt(q_ref[:, hs], k)
            s = jnp.where(valid, s - _SLOPES[h] * dist, NEG)
            p, denom = _softmax_sink(s, sinks_ref[0, h])
            acc_ref[:, hs] = _dot(p.astype(_BF), v) / denom
    o_ref[...] = _rms(acc_ref[...], g_ref[...]).astype(_BF)


def _attn_prompt(q, kv, sinks, g, *, batch, seq):
    nb = seq // BLOCK
    return pl.pallas_call(
        _attn_prompt_kernel,
        grid=(batch, nb),
        in_specs=[
            pl.BlockSpec(memory_space=pltpu.SMEM),
            pl.BlockSpec((BLOCK, ATTN_WIDTH), lambda b, i: (b * nb + i, 0)),
            pl.BlockSpec((BLOCK, 2 * KV_WIDTH), lambda b, i: (b * nb + i, 0)),
            pl.BlockSpec((BLOCK, 2 * KV_WIDTH), lambda b, i: (b * nb + jnp.maximum(i - 1, 0), 0)),
            pl.BlockSpec((1, ATTN_WIDTH), lambda b, i: (0, 0)),
        ],
        out_specs=pl.BlockSpec((BLOCK, ATTN_WIDTH), lambda b, i: (b * nb + i, 0)),
        out_shape=jax.ShapeDtypeStruct((batch * seq, ATTN_WIDTH), _BF),
        scratch_shapes=[pltpu.VMEM((BLOCK, ATTN_WIDTH), _F32)],
        compiler_params=pltpu.CompilerParams(dimension_semantics=("arbitrary", "arbitrary")),
        name="attn_prompt",
    )(sinks, q, kv, kv, g)


def _round_bf16(x):
    return x.astype(_BF).astype(_F32)


def _attn_sample_kernel(sinks_ref, q_ref, kvn_ref, ck_ref, cv_ref, g_ref,
                        o_ref, kwin_ref, vwin_ref, acc_ref, *, nseq, t_new):
    rows = GROUP * t_new
    r = lax.broadcasted_iota(jnp.int32, (rows, 1), 0)
    t = lax.broadcasted_iota(jnp.int32, (rows, WINDOW + t_new), 0) % t_new
    kj = lax.broadcasted_iota(jnp.int32, (rows, WINDOW + t_new), 1)
    dist_i = t + WINDOW - kj
    valid = (dist_i >= 0) & (dist_i <= WINDOW)
    dist = dist_i.astype(_F32)
    for kvh in range(N_KV_HEADS):
        slope = jnp.full((rows, 1), _SLOPES[kvh * GROUP], _F32)
        sink = jnp.full((rows, 1), sinks_ref[0, kvh * GROUP], _F32)
        for gq in range(1, GROUP):
            slope = jnp.where(r >= gq * t_new, _SLOPES[kvh * GROUP + gq], slope)
            sink = jnp.where(r >= gq * t_new, sinks_ref[0, kvh * GROUP + gq], sink)
        bias = slope * dist
        ks = slice(kvh * HEAD_DIM, (kvh + 1) * HEAD_DIM)
        vs = slice(KV_WIDTH + kvh * HEAD_DIM, KV_WIDTH + (kvh + 1) * HEAD_DIM)
        for b in range(nseq):
            rs = slice(b * t_new, (b + 1) * t_new)
            k = _round_bf16(jnp.concatenate([ck_ref[b, :, ks], kvn_ref[rs, ks]], axis=0))
            v = _round_bf16(jnp.concatenate([cv_ref[b, :, ks], kvn_ref[rs, vs]], axis=0))
            qs = jnp.concatenate(
                [q_ref[rs, (kvh * GROUP + gq) * HEAD_DIM:(kvh * GROUP + gq + 1) * HEAD_DIM]
                 for gq in range(GROUP)], axis=0)
            s = _dot_nt(_round_bf16(qs), k)
            s = jnp.where(valid, s - bias, NEG)
            p, denom = _softmax_sink(s, sink)
            o = _dot(_round_bf16(p), v) / denom
            for gq in range(GROUP):
                h = kvh * GROUP + gq
                acc_ref[rs, h * HEAD_DIM:(h + 1) * HEAD_DIM] = o[gq * t_new:(gq + 1) * t_new, :]
    o_ref[...] = _rms(acc_ref[...], g_ref[...]).astype(_BF)
    kwin_ref[:, 0:WINDOW - t_new, :] = ck_ref[:, t_new:, :]
    vwin_ref[:, 0:WINDOW - t_new, :] = cv_ref[:, t_new:, :]
    kwin_ref[:, WINDOW - t_new:, :] = kvn_ref[:, 0:KV_WIDTH].reshape(nseq, t_new, KV_WIDTH)
    vwin_ref[:, WINDOW - t_new:, :] = kvn_ref[:, KV_WIDTH:].reshape(nseq, t_new, KV_WIDTH)


def _attn_sample(q, kv, cache_k, cache_v, sinks, g, *, nseq_total, t_new, nseq=8):
    rows = nseq * t_new
    cache_spec = pl.BlockSpec((nseq, WINDOW, KV_WIDTH), lambda i: (i, 0, 0))
    return pl.pallas_call(
        functools.partial(_attn_sample_kernel, nseq=nseq, t_new=t_new),
        grid=(nseq_total // nseq,),
        in_specs=[
            pl.BlockSpec(memory_space=pltpu.SMEM),
            pl.BlockSpec((rows, ATTN_WIDTH), lambda i: (i, 0)),
            pl.BlockSpec((rows, 2 * KV_WIDTH), lambda i: (i, 0)),
            cache_spec,
            cache_spec,
            pl.BlockSpec((1, ATTN_WIDTH), lambda i: (0, 0)),
        ],
        out_specs=[pl.BlockSpec((rows, ATTN_WIDTH), lambda i: (i, 0)), cache_spec, cache_spec],
        out_shape=[
            jax.ShapeDtypeStruct((nseq_total * t_new, ATTN_WIDTH), _BF),
            jax.ShapeDtypeStruct(cache_k.shape, _F32),
            jax.ShapeDtypeStruct(cache_v.shape, _F32),
        ],
        scratch_shapes=[pltpu.VMEM((rows, ATTN_WIDTH), _F32)],
        compiler_params=pltpu.CompilerParams(dimension_semantics=("arbitrary",)),
        name="attn_sample",
    )(sinks, q, kv, cache_k, cache_v, g)


def _outproj_kernel(x_ref, na_ref, nb_ref, wa_ref, wb_ref, o_ref):
    o_ref[...] = x_ref[...] + _dot(na_ref[...], wa_ref[...]) + _dot(nb_ref[...], wb_ref[...])


def _outproj(x, na, nb, w_out):
    n = x.shape[0]
    tm = ROW_TILE
    return pl.pallas_call(
        _outproj_kernel,
        grid=(n // tm,),
        in_specs=[
            pl.BlockSpec((tm, D_MODEL), lambda i: (i, 0)),
            pl.BlockSpec((tm, ATTN_WIDTH), lambda i: (i, 0)),
            pl.BlockSpec((tm, CONV_CH), lambda i: (i, 0)),
            pl.BlockSpec((ATTN_WIDTH, D_MODEL), lambda i: (0, 0)),
            pl.BlockSpec((CONV_CH, D_MODEL), lambda i: (1, 0)),
        ],
        out_specs=pl.BlockSpec((tm, D_MODEL), lambda i: (i, 0)),
        out_shape=jax.ShapeDtypeStruct((n, D_MODEL), _F32),
        compiler_params=pltpu.CompilerParams(
            dimension_semantics=("arbitrary",), vmem_limit_bytes=VMEM_LIMIT),
        name="outproj",
    )(x, na, nb, w_out, w_out)


def _ffn_kernel(*refs, seq_tiles, sub_steps):
    carry_mode = seq_tiles is not None
    if carry_mode:
        (x_ref, g_ref, wg_ref, wu_ref, cw_ref, cb_ref, wd_ref, gf_ref,
         o_ref, st_ref, h_scr, acc_scr, ext_scr, carry_scr) = refs
    else:
        (x_ref, g_ref, wg_ref, wu_ref, cw_ref, cb_ref, wd_ref, gf_ref, state_ref,
         o_ref, st_ref, h_scr, acc_scr) = refs
    i = pl.program_id(0)
    j = pl.program_id(1)
    sub = x_ref.shape[0]

    @pl.when(j < sub_steps)
    def _():
        rows = pl.ds(pl.multiple_of(j * sub, sub), sub)
        x = x_ref[...]
        acc_scr[rows, :] = x
        h_scr[rows, :] = _rms(x, g_ref[...]).astype(_BF)

    @pl.when((j >= sub_steps) & (j < sub_steps + FF_STEPS))
    def _():
        h = h_scr[...]
        gate = _dot(h, wg_ref[...])
        if carry_mode:
            conv = _conv3_carry(gate, cw_ref, ext_scr, carry_scr.at[j - sub_steps], i % seq_tiles == 0)
            st_ref[0] = gate[gate.shape[0] - 2:, :]
        else:
            conv, last2 = _conv3_seqs(gate, cw_ref, state_ref)
            st_ref[...] = last2
        a = conv + cb_ref[...]
        act = ((a / (1.0 + jnp.exp(-a))) * _dot(h, wu_ref[...])).astype(_BF)
        acc_scr[...] += _dot(act, wd_ref[...])

    @pl.when(j >= sub_steps + FF_STEPS)
    def _():
        rows = pl.ds(pl.multiple_of((j - sub_steps - FF_STEPS) * sub, sub), sub)
        o_ref[...] = _rms(acc_scr[rows, :], gf_ref[...])


def _ffn(x, g, w_gate, w_up, conv_w, conv_b, w_down, g_final, state, *, seq_len, tm):
    n = x.shape[0]
    nt = n // tm
    sub_steps = tm // FF_SUB
    carry_mode = seq_len >= tm
    seq_tiles = seq_len // tm if carry_mode else None

    def ff_idx(j):
        return jnp.clip(j - sub_steps, 0, FF_STEPS - 1)

    def st_map(i, j):
        return (i, 0, ff_idx(j))

    in_specs = [
        pl.BlockSpec((FF_SUB, D_MODEL), lambda i, j: (i * sub_steps + jnp.minimum(j, sub_steps - 1), 0)),
        pl.BlockSpec((1, D_MODEL), lambda i, j: (0, 0)),
        pl.BlockSpec((D_MODEL, FF_TILE), lambda i, j: (0, ff_idx(j))),
        pl.BlockSpec((D_MODEL, FF_TILE), lambda i, j: (0, ff_idx(j))),
        pl.BlockSpec((3, FF_TILE), lambda i, j: (0, ff_idx(j))),
        pl.BlockSpec((1, FF_TILE), lambda i, j: (0, ff_idx(j))),
        pl.BlockSpec((FF_TILE, D_MODEL), lambda i, j: (ff_idx(j), 0)),
        pl.BlockSpec((1, D_MODEL), lambda i, j: (0, 0)),
    ]
    args = [x, g, w_gate, w_up, conv_w, conv_b, w_down, g_final]
    scratch = [pltpu.VMEM((tm, D_MODEL), _BF), pltpu.VMEM((tm, D_MODEL), _F32)]
    if carry_mode:
        st_shape = jax.ShapeDtypeStruct((nt, 2, D_FF), _F32)
        st_spec = pl.BlockSpec((1, 2, FF_TILE), st_map)
        scratch += [pltpu.VMEM((tm + SUBLANES, FF_TILE), _F32),
                    pltpu.VMEM((FF_STEPS, SUBLANES, FF_TILE), _F32)]
    else:
        nseq = tm // seq_len
        st_shape = jax.ShapeDtypeStruct((n // seq_len, 2, D_FF), _F32)
        st_spec = pl.BlockSpec((nseq, 2, FF_TILE), st_map)
        in_specs.append(st_spec)
        args.append(state)
    out_spec = pl.BlockSpec(
        (FF_SUB, D_MODEL), lambda i, j: (i * sub_steps + jnp.maximum(j - sub_steps - FF_STEPS, 0), 0))
    return pl.pallas_call(
        functools.partial(_ffn_kernel, seq_tiles=seq_tiles, sub_steps=sub_steps),
        grid=(nt, FF_STEPS + 2 * sub_steps),
        in_specs=in_specs,
        out_specs=[out_spec, st_spec],
        out_shape=[jax.ShapeDtypeStruct((n, D_MODEL), _F32), st_shape],
        scratch_shapes=scratch,
        compiler_params=pltpu.CompilerParams(
            dimension_semantics=("arbitrary", "arbitrary"), vmem_limit_bytes=VMEM_LIMIT),
        name="ffn_carry" if carry_mode else "ffn_seqs",
    )(*args)


def kernel(x_prompt, x_sample, cache_k_window, cache_v_window, state_conv, state_ffn_conv, g_attn_norm, w_in, attn_sinks, conv_w, g_out_attn, g_out_conv, w_out, g_ffn_norm, w_gate, w_up, ffn_conv_w, ffn_conv_b, w_down, g_final):
    assert w_in.shape[0] == 1, "single layer"
    bp, sp, _ = x_prompt.shape
    bs, ts, _ = x_sample.shape
    assert sp % ROW_TILE == 0 and (bs * ts) % ROW_TILE == 0 and ts == SUBLANES

    w_in_b = w_in[0].astype(_BF)
    w_out_b = w_out[0].astype(_BF)
    w_gate_b = w_gate[0].astype(_BF)
    w_up_b = w_up[0].astype(_BF)
    w_down_b = w_down[0].astype(_BF)
    g_attn = g_attn_norm[0].reshape(1, D_MODEL)
    g_ffn = g_ffn_norm[0].reshape(1, D_MODEL)
    g_fin = g_final.reshape(1, D_MODEL)
    g_oa = g_out_attn[0].reshape(1, ATTN_WIDTH)
    g_oc = g_out_conv[0].reshape(1, CONV_CH)
    sinks = attn_sinks[0].reshape(1, N_HEADS)
    cw = conv_w[0]
    fcw = ffn_conv_w[0]
    fcb = ffn_conv_b[0].reshape(1, D_FF)

    xp = x_prompt.reshape(bp * sp, D_MODEL)
    q_p, kv_p, nb_p, cst_p = _inproj(xp, g_attn, w_in_b, cw, g_oc, None, seq_len=sp, q_dtype=_BF)
    na_p = _attn_prompt(q_p, kv_p, sinks, g_oa, batch=bp, seq=sp)
    x1_p = _outproj(xp, na_p, nb_p, w_out_b)
    ff_rows = min(FF_ROWS_PROMPT, sp)
    y_p, fst_p = _ffn(x1_p, g_ffn, w_gate_b, w_up_b, fcw, fcb, w_down_b, g_fin, None, seq_len=sp, tm=ff_rows)

    seq_tiles = sp // ROW_TILE
    ff_seq_tiles = sp // ff_rows
    kv_p3 = kv_p.reshape(bp, sp, 2 * KV_WIDTH)[:, sp - WINDOW:, :]
    k_win_p = kv_p3[:, :, :KV_WIDTH].reshape(1, bp, WINDOW, N_KV_HEADS, HEAD_DIM)
    v_win_p = kv_p3[:, :, KV_WIDTH:].reshape(1, bp, WINDOW, N_KV_HEADS, HEAD_DIM)
    conv_p = cst_p[seq_tiles - 1::seq_tiles][None]
    ffn_conv_p = fst_p[ff_seq_tiles - 1::ff_seq_tiles][None]

    xs = x_sample.reshape(bs * ts, D_MODEL)
    q_s, kv_s, nb_s, cst_s = _inproj(xs, g_attn, w_in_b, cw, g_oc, state_conv[0], seq_len=ts, q_dtype=_F32)
    ck = cache_k_window[0].reshape(bs, WINDOW, KV_WIDTH)
    cv = cache_v_window[0].reshape(bs, WINDOW, KV_WIDTH)
    na_s, k_win_s, v_win_s = _attn_sample(q_s, kv_s, ck, cv, sinks, g_oa, nseq_total=bs, t_new=ts)
    x1_s = _outproj(xs, na_s, nb_s, w_out_b)
    y_s, fst_s = _ffn(x1_s, g_ffn, w_gate_b, w_up_b, fcw, fcb, w_down_b, g_fin, state_ffn_conv[0],
                      seq_len=ts, tm=min(FF_ROWS_SAMPLE, bs * ts))

    win_shape = (1, bs, WINDOW, N_KV_HEADS, HEAD_DIM)
    return (y_p.reshape(bp, sp, D_MODEL), y_s.reshape(bs, ts, D_MODEL),
            k_win_p, v_win_p, conv_p, ffn_conv_p,
            k_win_s.reshape(win_shape), v_win_s.reshape(win_shape), cst_s[None], fst_s[None])
```

```python
import functools

import jax
import jax.numpy as jnp
from jax import lax
from jax.experimental import pallas as pl
from jax.experimental.pallas import tpu as pltpu

D_MODEL = 2048
ATTN_WIDTH = 1024
CONV_CH = 1024
HEAD_DIM = 64
N_HEADS = 16
N_KV_HEADS = 4
GROUP = 4
KV_WIDTH = 256
WINDOW = 128
BLOCK = 128
D_FF = 5632
EPS = 1e-6
NEG = -1e30

SUBLANES = 8
ROW_TILE = 512
COL_TILE = 512
Q_STEPS = ATTN_WIDTH // COL_TILE
KV_STEP = Q_STEPS
CONV_STEP0 = KV_STEP + 1
CONV_STEPS = CONV_CH // COL_TILE
IN_STEPS = CONV_STEP0 + CONV_STEPS
FF_TILE = 256
FF_STEPS = D_FF // FF_TILE
FF_SUB = 512
FF_ROWS_PROMPT = 2048
FF_ROWS_SAMPLE = 1024
B_BLK0 = (ATTN_WIDTH + 2 * KV_WIDTH) // COL_TILE
C_BLK0 = B_BLK0 + CONV_STEPS
U_BLK0 = C_BLK0 + CONV_STEPS
VMEM_LIMIT = 56 * 1024 * 1024

_SLOPES = [2.0 ** (-8.0 * (h + 1) / N_HEADS) for h in range(N_HEADS)]
_BF = jnp.bfloat16
_F32 = jnp.float32


def _rms(x, g):
    return x * lax.rsqrt(jnp.mean(x * x, axis=-1, keepdims=True) + EPS) * g


def _dot(a, b):
    return jnp.dot(a, b, preferred_element_type=_F32)


def _dot_nt(a, b):
    return lax.dot_general(a, b, (((1,), (1,)), ((), ())), preferred_element_type=_F32)


def _conv3_carry(cur, w_ref, ext_ref, carry_ref, is_seq_start):
    rows = cur.shape[0]
    prev = jnp.where(is_seq_start, 0.0, carry_ref[...])
    ext_ref[0:SUBLANES, :] = prev
    ext_ref[SUBLANES:SUBLANES + rows, :] = cur
    carry_ref[...] = cur[rows - SUBLANES:, :]
    p1 = ext_ref[SUBLANES - 1:SUBLANES - 1 + rows, :]
    p2 = ext_ref[SUBLANES - 2:SUBLANES - 2 + rows, :]
    return w_ref[2:3, :] * cur + w_ref[1:2, :] * p1 + w_ref[0:1, :] * p2


def _conv3_seqs(cur, w_ref, state_ref):
    rows, ch = cur.shape
    nseq = rows // SUBLANES
    cur3 = cur.reshape(nseq, SUBLANES, ch)
    st = state_ref[...]
    t = lax.broadcasted_iota(jnp.int32, cur3.shape, 1)
    s0 = jnp.broadcast_to(st[:, 0:1, :], cur3.shape)
    s1 = jnp.broadcast_to(st[:, 1:2, :], cur3.shape)
    p1 = jnp.where(t == 0, s1, pltpu.roll(cur3, 1, 1))
    p2 = jnp.where(t == 0, s0, jnp.where(t == 1, s1, pltpu.roll(cur3, 2, 1)))
    w = w_ref[...]
    out = w[2:3, :][None] * cur3 + w[1:2, :][None] * p1 + w[0:1, :][None] * p2
    return out.reshape(rows, ch), cur3[:, SUBLANES - 2:, :]


def _inproj_kernel(*refs, seq_tiles, q_dtype):
    carry_mode = seq_tiles is not None
    if carry_mode:
        (x_ref, g_ref, wa_ref, wb_ref, wc_ref, cw_ref, gc_ref,
         q_ref, kv_ref, nb_ref, st_ref, h_scr, y_scr, ss_scr, ext_scr, carry_scr) = refs
    else:
        (x_ref, g_ref, wa_ref, wb_ref, wc_ref, cw_ref, gc_ref, state_ref,
         q_ref, kv_ref, nb_ref, st_ref, h_scr, y_scr, ss_scr) = refs
    i = pl.program_id(0)
    j = pl.program_id(1)

    @pl.when(j == 0)
    def _():
        h_scr[...] = _rms(x_ref[...], g_ref[...]).astype(_BF)

    @pl.when(j < Q_STEPS)
    def _():
        q_ref[...] = (_dot(h_scr[...], wa_ref[...]) * (HEAD_DIM ** -0.5)).astype(q_dtype)

    @pl.when(j == KV_STEP)
    def _():
        kv_ref[...] = _dot(h_scr[...], wa_ref[...])

    for c in range(CONV_STEPS):
        @pl.when(j == CONV_STEP0 + c)
        def _(c=c):
            h = h_scr[...]
            cu = _dot(h, wb_ref[...]) * _dot(h, wc_ref[...])
            if carry_mode:
                conv = _conv3_carry(cu, cw_ref, ext_scr, carry_scr.at[c], i % seq_tiles == 0)
                st_ref[0] = cu[cu.shape[0] - 2:, :]
            else:
                conv, last2 = _conv3_seqs(cu, cw_ref, state_ref)
                st_ref[...] = last2
            y = _dot(h, wa_ref[...]) * conv
            y_scr[c] = y
            ss = jnp.sum(y * y, axis=-1, keepdims=True)
            if c == 0:
                ss_scr[...] = ss
            else:
                ss_scr[...] += ss
            if c == CONV_STEPS - 1:
                inv = lax.rsqrt(ss_scr[...] * (1.0 / CONV_CH) + EPS)
                for cc in range(CONV_STEPS):
                    sl = slice(cc * COL_TILE, (cc + 1) * COL_TILE)
                    nb_ref[:, sl] = (y_scr[cc] * inv * gc_ref[:, sl]).astype(_BF)


def _inproj(x, g, w_in, conv_w, g_conv, state, *, seq_len, q_dtype):
    n = x.shape[0]
    tm = ROW_TILE
    nt = n // tm
    carry_mode = seq_len >= tm
    seq_tiles = seq_len // tm if carry_mode else None

    def wa_map(i, j):
        return (0, j)

    def wb_map(i, j):
        return (0, C_BLK0 + jnp.maximum(j - CONV_STEP0, 0))

    def wc_map(i, j):
        return (0, U_BLK0 + jnp.maximum(j - CONV_STEP0, 0))

    def cmap(i, j):
        return (0, jnp.maximum(j - CONV_STEP0, 0))

    in_specs = [
        pl.BlockSpec((tm, D_MODEL), lambda i, j: (i, 0)),
        pl.BlockSpec((1, D_MODEL), lambda i, j: (0, 0)),
        pl.BlockSpec((D_MODEL, COL_TILE), wa_map),
        pl.BlockSpec((D_MODEL, COL_TILE), wb_map),
        pl.BlockSpec((D_MODEL, COL_TILE), wc_map),
        pl.BlockSpec((3, COL_TILE), cmap),
        pl.BlockSpec((1, CONV_CH), lambda i, j: (0, 0)),
    ]
    args = [x, g, w_in, w_in, w_in, conv_w, g_conv]
    scratch = [
        pltpu.VMEM((tm, D_MODEL), _BF),
        pltpu.VMEM((CONV_STEPS, tm, COL_TILE), _F32),
        pltpu.VMEM((tm, 1), _F32),
    ]
    if carry_mode:
        st_shape = jax.ShapeDtypeStruct((nt, 2, CONV_CH), _F32)
        st_spec = pl.BlockSpec((1, 2, COL_TILE), lambda i, j: (i, 0, jnp.maximum(j - CONV_STEP0, 0)))
        scratch += [pltpu.VMEM((tm + SUBLANES, COL_TILE), _F32),
                    pltpu.VMEM((CONV_STEPS, SUBLANES, COL_TILE), _F32)]
    else:
        nseq = tm // seq_len
        st_shape = jax.ShapeDtypeStruct((n // seq_len, 2, CONV_CH), _F32)
        st_spec = pl.BlockSpec((nseq, 2, COL_TILE), lambda i, j: (i, 0, jnp.maximum(j - CONV_STEP0, 0)))
        in_specs.append(st_spec)
        args.append(state)
    out_shape = [
        jax.ShapeDtypeStruct((n, ATTN_WIDTH), q_dtype),
        jax.ShapeDtypeStruct((n, 2 * KV_WIDTH), _F32),
        jax.ShapeDtypeStruct((n, CONV_CH), _BF),
        st_shape,
    ]
    out_specs = [
        pl.BlockSpec((tm, COL_TILE), lambda i, j: (i, jnp.minimum(j, Q_STEPS - 1))),
        pl.BlockSpec((tm, 2 * KV_WIDTH), lambda i, j: (i, 0)),
        pl.BlockSpec((tm, CONV_CH), lambda i, j: (i, 0)),
        st_spec,
    ]
    return pl.pallas_call(
        functools.partial(_inproj_kernel, seq_tiles=seq_tiles, q_dtype=q_dtype),
        grid=(nt, IN_STEPS),
        in_specs=in_specs,
        out_specs=out_specs,
        out_shape=out_shape,
        scratch_shapes=scratch,
        compiler_params=pltpu.CompilerParams(
            dimension_semantics=("arbitrary", "arbitrary"), vmem_limit_bytes=VMEM_LIMIT),
        name="inproj_carry" if carry_mode else "inproj_seqs",
    )(*args)


def _softmax_sink(s, sink):
    m = jnp.maximum(jnp.max(s, axis=-1, keepdims=True), sink)
    p = jnp.exp(s - m)
    return p, jnp.sum(p, axis=-1, keepdims=True) + jnp.exp(sink - m)


def _half_lane_operands(slab, low_half):
    lo = lax.broadcasted_iota(jnp.int32, slab.shape, 1) < HEAD_DIM
    moved = pltpu.roll(slab, HEAD_DIM, 1)
    in_low = jnp.where(lo, slab if low_half else moved, 0.0).astype(_BF)
    in_high = jnp.where(lo, 0.0, moved if low_half else slab).astype(_BF)
    return in_low, in_high


def _attn_prompt_kernel(sinks_ref, q_ref, kvc_ref, kvp_ref, g_ref, o_ref, acc_ref):
    i = pl.program_id(1)
    pair_w = 2 * HEAD_DIM
    qi = lax.broadcasted_iota(jnp.int32, (BLOCK, 2 * BLOCK), 0)
    kj = lax.broadcasted_iota(jnp.int32, (BLOCK, 2 * BLOCK), 1)
    dist_i = qi - kj + BLOCK
    valid = (dist_i >= 0) & (dist_i <= WINDOW) & ((kj >= BLOCK) | (i > 0))
    dist = jnp.where(valid, dist_i.astype(_F32), -NEG / min(_SLOPES))
    low = lax.broadcasted_iota(jnp.int32, (BLOCK, pair_w), 1) < HEAD_DIM

    k_ops, v_ops = [], []
    for m in range(N_KV_HEADS // 2):
        ksl = slice(m * pair_w, (m + 1) * pair_w)
        vsl = slice(KV_WIDTH + m * pair_w, KV_WIDTH + (m + 1) * pair_w)
        kslab = jnp.concatenate([kvp_ref[:, ksl], kvc_ref[:, ksl]], axis=0)
        vslab = jnp.concatenate([kvp_ref[:, vsl], kvc_ref[:, vsl]], axis=0)
        for low_half in (True, False):
            k_ops.append(_half_lane_operands(kslab, low_half))
            v_ops.append(_half_lane_operands(vslab, low_half))

    def scores(c):
        qc = q_ref[:, c * pair_w:(c + 1) * pair_w]
        k_low, k_high = k_ops[c // 2]
        return _dot_nt(qc, k_low), _dot_nt(qc, k_high)

    n_pairs = N_HEADS // 2
    s_next = scores(0)
    for c in range(n_pairs):
        s_pair = s_next
        if c + 1 < n_pairs:
            s_next = scores(c + 1)
        v_low, v_high = v_ops[c // 2]
        p0, d0 = _softmax_sink(s_pair[0] - _SLOPES[2 * c] * dist, sinks_ref[0, 2 * c])
        p1, d1 = _softmax_sink(s_pair[1] - _SLOPES[2 * c + 1] * dist, sinks_ref[0, 2 * c + 1])
        o = _dot(p0.astype(_BF), v_low) + _dot(p1.astype(_BF), v_high)
        acc_ref[:, c * pair_w:(c + 1) * pair_w] = o / jnp.where(low, d0, d1)
    o_ref[...] = _rms(acc_ref[...], g_ref[...]).astype(_BF)


def _attn_prompt(q, kv, sinks, g, *, batch, seq):
    nb = seq // BLOCK
    return pl.pallas_call(
        _attn_prompt_kernel,
        grid=(batch, nb),
        in_specs=[
            pl.BlockSpec(memory_space=pltpu.SMEM),
            pl.BlockSpec((BLOCK, ATTN_WIDTH), lambda b, i: (b * nb + i, 0)),
            pl.BlockSpec((BLOCK, 2 * KV_WIDTH), lambda b, i: (b * nb + i, 0)),
            pl.BlockSpec((BLOCK, 2 * KV_WIDTH), lambda b, i: (b * nb + jnp.maximum(i - 1, 0), 0)),
            pl.BlockSpec((1, ATTN_WIDTH), lambda b, i: (0, 0)),
        ],
        out_specs=pl.BlockSpec((BLOCK, ATTN_WIDTH), lambda b, i: (b * nb + i, 0)),
        out_shape=jax.ShapeDtypeStruct((batch * seq, ATTN_WIDTH), _BF),
        scratch_shapes=[pltpu.VMEM((BLOCK, ATTN_WIDTH), _F32)],
        compiler_params=pltpu.CompilerParams(dimension_semantics=("arbitrary", "arbitrary")),
        name="attn_prompt",
    )(sinks, q, kv, kv, g)


def _round_bf16(x):
    return x.astype(_BF).astype(_F32)


def _attn_sample_kernel(sinks_ref, q_ref, kvn_ref, ck_ref, cv_ref, g_ref,
                        o_ref, kwin_ref, vwin_ref, acc_ref, s_scr, *, nseq, t_new):
    rows = GROUP * t_new
    nkeys = WINDOW + t_new
    ksl = [slice(kvh * HEAD_DIM, (kvh + 1) * HEAD_DIM) for kvh in range(N_KV_HEADS)]
    vsl = [slice(KV_WIDTH + kvh * HEAD_DIM, KV_WIDTH + (kvh + 1) * HEAD_DIM) for kvh in range(N_KV_HEADS)]

    for kvh in range(N_KV_HEADS):
        for b in range(nseq):
            rs = slice(b * t_new, (b + 1) * t_new)
            k = _round_bf16(jnp.concatenate([ck_ref[b, :, ksl[kvh]], kvn_ref[rs, ksl[kvh]]], axis=0))
            qs = jnp.concatenate(
                [q_ref[rs, (kvh * GROUP + gq) * HEAD_DIM:(kvh * GROUP + gq + 1) * HEAD_DIM]
                 for gq in range(GROUP)], axis=0)
            s_scr[kvh, b * rows:(b + 1) * rows, :] = _dot_nt(_round_bf16(qs), k)

    r = lax.broadcasted_iota(jnp.int32, (nseq * rows, 1), 0)
    gq_of_row = (r // t_new) % GROUP
    t = lax.broadcasted_iota(jnp.int32, (nseq * rows, nkeys), 0) % t_new
    kj = lax.broadcasted_iota(jnp.int32, (nseq * rows, nkeys), 1)
    dist_i = t + WINDOW - kj
    valid = (dist_i >= 0) & (dist_i <= WINDOW)
    dist = jnp.where(valid, dist_i.astype(_F32), -NEG / min(_SLOPES))
    for kvh in range(N_KV_HEADS):
        slope = jnp.full((nseq * rows, 1), _SLOPES[kvh * GROUP], _F32)
        sink = jnp.full((nseq * rows, 1), sinks_ref[0, kvh * GROUP], _F32)
        for gq in range(1, GROUP):
            slope = jnp.where(gq_of_row == gq, _SLOPES[kvh * GROUP + gq], slope)
            sink = jnp.where(gq_of_row == gq, sinks_ref[0, kvh * GROUP + gq], sink)
        p, denom = _softmax_sink(s_scr[kvh] - slope * dist, sink)
        s_scr[kvh] = _round_bf16(p / denom)

    for kvh in range(N_KV_HEADS):
        for b in range(nseq):
            rs = slice(b * t_new, (b + 1) * t_new)
            v = _round_bf16(jnp.concatenate([cv_ref[b, :, ksl[kvh]], kvn_ref[rs, vsl[kvh]]], axis=0))
            o = _dot(s_scr[kvh, b * rows:(b + 1) * rows, :], v)
            for gq in range(GROUP):
                h = kvh * GROUP + gq
                acc_ref[rs, h * HEAD_DIM:(h + 1) * HEAD_DIM] = o[gq * t_new:(gq + 1) * t_new, :]
    o_ref[...] = _rms(acc_ref[...], g_ref[...]).astype(_BF)
    kwin_ref[:, 0:WINDOW - t_new, :] = ck_ref[:, t_new:, :]
    vwin_ref[:, 0:WINDOW - t_new, :] = cv_ref[:, t_new:, :]
    kwin_ref[:, WINDOW - t_new:, :] = kvn_ref[:, 0:KV_WIDTH].reshape(nseq, t_new, KV_WIDTH)
    vwin_ref[:, WINDOW - t_new:, :] = kvn_ref[:, KV_WIDTH:].reshape(nseq, t_new, KV_WIDTH)


def _attn_sample(q, kv, cache_k, cache_v, sinks, g, *, nseq_total, t_new, nseq=8):
    rows = nseq * t_new
    cache_spec = pl.BlockSpec((nseq, WINDOW, KV_WIDTH), lambda i: (i, 0, 0))
    return pl.pallas_call(
        functools.partial(_attn_sample_kernel, nseq=nseq, t_new=t_new),
        grid=(nseq_total // nseq,),
        in_specs=[
            pl.BlockSpec(memory_space=pltpu.SMEM),
            pl.BlockSpec((rows, ATTN_WIDTH), lambda i: (i, 0)),
            pl.BlockSpec((rows, 2 * KV_WIDTH), lambda i: (i, 0)),
            cache_spec,
            cache_spec,
            pl.BlockSpec((1, ATTN_WIDTH), lambda i: (0, 0)),
        ],
        out_specs=[pl.BlockSpec((rows, ATTN_WIDTH), lambda i: (i, 0)), cache_spec, cache_spec],
        out_shape=[
            jax.ShapeDtypeStruct((nseq_total * t_new, ATTN_WIDTH), _BF),
            jax.ShapeDtypeStruct(cache_k.shape, _F32),
            jax.ShapeDtypeStruct(cache_v.shape, _F32),
        ],
        scratch_shapes=[pltpu.VMEM((rows, ATTN_WIDTH), _F32),
                        pltpu.VMEM((N_KV_HEADS, nseq * GROUP * t_new, WINDOW + t_new), _F32)],
        compiler_params=pltpu.CompilerParams(dimension_semantics=("arbitrary",)),
        name="attn_sample",
    )(sinks, q, kv, cache_k, cache_v, g)


def _outproj_kernel(x_ref, na_ref, nb_ref, wa_ref, wb_ref, o_ref):
    o_ref[...] = x_ref[...] + _dot(na_ref[...], wa_ref[...]) + _dot(nb_ref[...], wb_ref[...])


def _outproj(x, na, nb, w_out):
    n = x.shape[0]
    tm = ROW_TILE
    return pl.pallas_call(
        _outproj_kernel,
        grid=(n // tm,),
        in_specs=[
            pl.BlockSpec((tm, D_MODEL), lambda i: (i, 0)),
            pl.BlockSpec((tm, ATTN_WIDTH), lambda i: (i, 0)),
            pl.BlockSpec((tm, CONV_CH), lambda i: (i, 0)),
            pl.BlockSpec((ATTN_WIDTH, D_MODEL), lambda i: (0, 0)),
            pl.BlockSpec((CONV_CH, D_MODEL), lambda i: (1, 0)),
        ],
        out_specs=pl.BlockSpec((tm, D_MODEL), lambda i: (i, 0)),
        out_shape=jax.ShapeDtypeStruct((n, D_MODEL), _F32),
        compiler_params=pltpu.CompilerParams(
            dimension_semantics=("arbitrary",), vmem_limit_bytes=VMEM_LIMIT),
        name="outproj",
    )(x, na, nb, w_out, w_out)


def _ffn_kernel(*refs, seq_tiles, sub_steps):
    carry_mode = seq_tiles is not None
    if carry_mode:
        (x_ref, g_ref, wg_ref, wu_ref, cw_ref, cb_ref, wd_ref, gf_ref,
         o_ref, st_ref, h_scr, acc_scr, ext_scr, carry_scr) = refs
    else:
        (x_ref, g_ref, wg_ref, wu_ref, cw_ref, cb_ref, wd_ref, gf_ref, state_ref,
         o_ref, st_ref, h_scr, acc_scr) = refs
    i = pl.program_id(0)
    j = pl.program_id(1)
    sub = x_ref.shape[0]

    @pl.when(j < sub_steps)
    def _():
        rows = pl.ds(pl.multiple_of(j * sub, sub), sub)
        x = x_ref[...]
        acc_scr[rows, :] = x
        h_scr[rows, :] = _rms(x, g_ref[...]).astype(_BF)

    @pl.when((j >= sub_steps) & (j < sub_steps + FF_STEPS))
    def _():
        h = h_scr[...]
        gate = _dot(h, wg_ref[...])
        if carry_mode:
            conv = _conv3_carry(gate, cw_ref, ext_scr, carry_scr.at[j - sub_steps], i % seq_tiles == 0)
            st_ref[0] = gate[gate.shape[0] - 2:, :]
        else:
            conv, last2 = _conv3_seqs(gate, cw_ref, state_ref)
            st_ref[...] = last2
        a = conv + cb_ref[...]
        act = ((a / (1.0 + jnp.exp(-a))) * _dot(h, wu_ref[...])).astype(_BF)
        acc_scr[...] += _dot(act, wd_ref[...])

    @pl.when(j >= sub_steps + FF_STEPS)
    def _():
        rows = pl.ds(pl.multiple_of((j - sub_steps - FF_STEPS) * sub, sub), sub)
        o_ref[...] = _rms(acc_scr[rows, :], gf_ref[...])


def _ffn(x, g, w_gate, w_up, conv_w, conv_b, w_down, g_final, state, *, seq_len, tm):
    n = x.shape[0]
    nt = n // tm
    sub_steps = tm // FF_SUB
    carry_mode = seq_len >= tm
    seq_tiles = seq_len // tm if carry_mode else None

    def ff_idx(j):
        return jnp.clip(j - sub_steps, 0, FF_STEPS - 1)

    def st_map(i, j):
        return (i, 0, ff_idx(j))

    in_specs = [
        pl.BlockSpec((FF_SUB, D_MODEL), lambda i, j: (i * sub_steps + jnp.minimum(j, sub_steps - 1), 0)),
        pl.BlockSpec((1, D_MODEL), lambda i, j: (0, 0)),
        pl.BlockSpec((D_MODEL, FF_TILE), lambda i, j: (0, ff_idx(j))),
        pl.BlockSpec((D_MODEL, FF_TILE), lambda i, j: (0, ff_idx(j))),
        pl.BlockSpec((3, FF_TILE), lambda i, j: (0, ff_idx(j))),
        pl.BlockSpec((1, FF_TILE), lambda i, j: (0, ff_idx(j))),
        pl.BlockSpec((FF_TILE, D_MODEL), lambda i, j: (ff_idx(j), 0)),
        pl.BlockSpec((1, D_MODEL), lambda i, j: (0, 0)),
    ]
    args = [x, g, w_gate, w_up, conv_w, conv_b, w_down, g_final]
    scratch = [pltpu.VMEM((tm, D_MODEL), _BF), pltpu.VMEM((tm, D_MODEL), _F32)]
    if carry_mode:
        st_shape = jax.ShapeDtypeStruct((nt, 2, D_FF), _F32)
        st_spec = pl.BlockSpec((1, 2, FF_TILE), st_map)
        scratch += [pltpu.VMEM((tm + SUBLANES, FF_TILE), _F32),
                    pltpu.VMEM((FF_STEPS, SUBLANES, FF_TILE), _F32)]
    else:
        nseq = tm // seq_len
        st_shape = jax.ShapeDtypeStruct((n // seq_len, 2, D_FF), _F32)
        st_spec = pl.BlockSpec((nseq, 2, FF_TILE), st_map)
        in_specs.append(st_spec)
        args.append(state)
    out_spec = pl.BlockSpec(
        (FF_SUB, D_MODEL), lambda i, j: (i * sub_steps + jnp.maximum(j - sub_steps - FF_STEPS, 0), 0))
    return pl.pallas_call(
        functools.partial(_ffn_kernel, seq_tiles=seq_tiles, sub_steps=sub_steps),
        grid=(nt, FF_STEPS + 2 * sub_steps),
        in_specs=in_specs,
        out_specs=[out_spec, st_spec],
        out_shape=[jax.ShapeDtypeStruct((n, D_MODEL), _F32), st_shape],
        scratch_shapes=scratch,
        compiler_params=pltpu.CompilerParams(
            dimension_semantics=("arbitrary", "arbitrary"), vmem_limit_bytes=VMEM_LIMIT),
        name="ffn_carry" if carry_mode else "ffn_seqs",
    )(*args)


def kernel(x_prompt, x_sample, cache_k_window, cache_v_window, state_conv, state_ffn_conv, g_attn_norm, w_in, attn_sinks, conv_w, g_out_attn, g_out_conv, w_out, g_ffn_norm, w_gate, w_up, ffn_conv_w, ffn_conv_b, w_down, g_final):
    assert w_in.shape[0] == 1, "single layer"
    bp, sp, _ = x_prompt.shape
    bs, ts, _ = x_sample.shape
    assert sp % ROW_TILE == 0 and (bs * ts) % ROW_TILE == 0 and ts == SUBLANES

    w_in_b = w_in[0].astype(_BF)
    w_out_b = w_out[0].astype(_BF)
    w_gate_b = w_gate[0].astype(_BF)
    w_up_b = w_up[0].astype(_BF)
    w_down_b = w_down[0].astype(_BF)
    g_attn = g_attn_norm[0].reshape(1, D_MODEL)
    g_ffn = g_ffn_norm[0].reshape(1, D_MODEL)
    g_fin = g_final.reshape(1, D_MODEL)
    g_oa = g_out_attn[0].reshape(1, ATTN_WIDTH)
    g_oc = g_out_conv[0].reshape(1, CONV_CH)
    sinks = attn_sinks[0].reshape(1, N_HEADS)
    cw = conv_w[0]
    fcw = ffn_conv_w[0]
    fcb = ffn_conv_b[0].reshape(1, D_FF)

    xp = x_prompt.reshape(bp * sp, D_MODEL)
    q_p, kv_p, nb_p, cst_p = _inproj(xp, g_attn, w_in_b, cw, g_oc, None, seq_len=sp, q_dtype=_BF)
    na_p = _attn_prompt(q_p, kv_p, sinks, g_oa, batch=bp, seq=sp)
    x1_p = _outproj(xp, na_p, nb_p, w_out_b)
    ff_rows = min(FF_ROWS_PROMPT, sp)
    y_p, fst_p = _ffn(x1_p, g_ffn, w_gate_b, w_up_b, fcw, fcb, w_down_b, g_fin, None, seq_len=sp, tm=ff_rows)

    seq_tiles = sp // ROW_TILE
    ff_seq_tiles = sp // ff_rows
    kv_p3 = kv_p.reshape(bp, sp, 2 * KV_WIDTH)[:, sp - WINDOW:, :]
    k_win_p = kv_p3[:, :, :KV_WIDTH].reshape(1, bp, WINDOW, N_KV_HEADS, HEAD_DIM)
    v_win_p = kv_p3[:, :, KV_WIDTH:].reshape(1, bp, WINDOW, N_KV_HEADS, HEAD_DIM)
    conv_p = cst_p[seq_tiles - 1::seq_tiles][None]
    ffn_conv_p = fst_p[ff_seq_tiles - 1::ff_seq_tiles][None]

    xs = x_sample.reshape(bs * ts, D_MODEL)
    q_s, kv_s, nb_s, cst_s = _inproj(xs, g_attn, w_in_b, cw, g_oc, state_conv[0], seq_len=ts, q_dtype=_F32)
    ck = cache_k_window[0].reshape(bs, WINDOW, KV_WIDTH)
    cv = cache_v_window[0].reshape(bs, WINDOW, KV_WIDTH)
    na_s, k_win_s, v_win_s = _attn_sample(q_s, kv_s, ck, cv, sinks, g_oa, nseq_total=bs, t_new=ts)
    x1_s = _outproj(xs, na_s, nb_s, w_out_b)
    y_s, fst_s = _ffn(x1_s, g_ffn, w_gate_b, w_up_b, fcw, fcb, w_down_b, g_fin, state_ffn_conv[0],
                      seq_len=ts, tm=min(FF_ROWS_SAMPLE, bs * ts))

    win_shape = (1, bs, WINDOW, N_KV_HEADS, HEAD_DIM)
    return (y_p.reshape(bp, sp, D_MODEL), y_s.reshape(bs, ts, D_MODEL),
            k_win_p, v_win_p, conv_p, ffn_conv_p,
            k_win_s.reshape(win_shape), v_win_s.reshape(win_shape), cst_s[None], fst_s[None])
```

```python
import functools

import jax
import jax.numpy as jnp
from jax import lax
from jax.experimental import pallas as pl
from jax.experimental.pallas import tpu as pltpu

D_MODEL = 2048
ATTN_WIDTH = 1024
CONV_CH = 1024
HEAD_DIM = 64
N_HEADS = 16
N_KV_HEADS = 4
GROUP = 4
KV_WIDTH = 256
WINDOW = 128
BLOCK = 128
D_FF = 5632
EPS = 1e-6
NEG = -1e30

SUBLANES = 8
ROW_TILE = 512
SUB_ROWS = 512
IN_ROWS = 1024
COL_TILE = 512
Q_STEPS = ATTN_WIDTH // COL_TILE
KV_STEP = Q_STEPS
CONV_STEP0 = KV_STEP + 1
CONV_STEPS = CONV_CH // COL_TILE
IN_STEPS = CONV_STEP0 + CONV_STEPS
FF_TILE = 256
FF_STEPS = D_FF // FF_TILE
FF_ROWS_PROMPT = 2048
FF_ROWS_SAMPLE = 1024
B_BLK0 = (ATTN_WIDTH + 2 * KV_WIDTH) // COL_TILE
C_BLK0 = B_BLK0 + CONV_STEPS
U_BLK0 = C_BLK0 + CONV_STEPS
VMEM_LIMIT = 56 * 1024 * 1024

_SLOPES = [2.0 ** (-8.0 * (h + 1) / N_HEADS) for h in range(N_HEADS)]
_BF = jnp.bfloat16
_F32 = jnp.float32


def _rms(x, g):
    return x * lax.rsqrt(jnp.mean(x * x, axis=-1, keepdims=True) + EPS) * g


def _dot(a, b):
    return jnp.dot(a, b, preferred_element_type=_F32)


def _dot_nt(a, b):
    return lax.dot_general(a, b, (((1,), (1,)), ((), ())), preferred_element_type=_F32)


def _conv3_carry(cur, w_ref, ext_ref, carry_ref, is_seq_start):
    rows = cur.shape[0]
    prev = jnp.where(is_seq_start, 0.0, carry_ref[...])
    ext_ref[0:SUBLANES, :] = prev
    ext_ref[SUBLANES:SUBLANES + rows, :] = cur
    carry_ref[...] = cur[rows - SUBLANES:, :]
    p1 = ext_ref[SUBLANES - 1:SUBLANES - 1 + rows, :]
    p2 = ext_ref[SUBLANES - 2:SUBLANES - 2 + rows, :]
    return w_ref[2:3, :] * cur + w_ref[1:2, :] * p1 + w_ref[0:1, :] * p2


def _conv3_seqs(cur, w_ref, state_ref):
    rows, ch = cur.shape
    nseq = rows // SUBLANES
    cur3 = cur.reshape(nseq, SUBLANES, ch)
    st = state_ref[...]
    t = lax.broadcasted_iota(jnp.int32, cur3.shape, 1)
    s0 = jnp.broadcast_to(st[:, 0:1, :], cur3.shape)
    s1 = jnp.broadcast_to(st[:, 1:2, :], cur3.shape)
    p1 = jnp.where(t == 0, s1, pltpu.roll(cur3, 1, 1))
    p2 = jnp.where(t == 0, s0, jnp.where(t == 1, s1, pltpu.roll(cur3, 2, 1)))
    w = w_ref[...]
    out = w[2:3, :][None] * cur3 + w[1:2, :][None] * p1 + w[0:1, :][None] * p2
    return out.reshape(rows, ch), cur3[:, SUBLANES - 2:, :]


def _inproj_kernel(*refs, seq_tiles, sub_steps, q_dtype):
    carry_mode = seq_tiles is not None
    if carry_mode:
        (x_ref, g_ref, wa_ref, wb_ref, wc_ref, cw_ref, gc_ref,
         q_ref, kv_ref, nb_ref, st_ref, h_scr, y_scr, ss_scr, ext_scr, carry_scr) = refs
    else:
        (x_ref, g_ref, wa_ref, wb_ref, wc_ref, cw_ref, gc_ref, state_ref,
         q_ref, kv_ref, nb_ref, st_ref, h_scr, y_scr, ss_scr) = refs
    i = pl.program_id(0)
    j = pl.program_id(1) - sub_steps
    sub = x_ref.shape[0]

    @pl.when(j < 0)
    def _():
        rows = pl.ds(pl.multiple_of((j + sub_steps) * sub, sub), sub)
        h_scr[rows, :] = _rms(x_ref[...], g_ref[...]).astype(_BF)

    @pl.when((j >= 0) & (j < Q_STEPS))
    def _():
        q_ref[...] = (_dot(h_scr[...], wa_ref[...]) * (HEAD_DIM ** -0.5)).astype(q_dtype)

    @pl.when(j == KV_STEP)
    def _():
        kv_ref[...] = _dot(h_scr[...], wa_ref[...])

    for c in range(CONV_STEPS):
        @pl.when(j == CONV_STEP0 + c)
        def _(c=c):
            h = h_scr[...]
            cu = _dot(h, wb_ref[...]) * _dot(h, wc_ref[...])
            if carry_mode:
                conv = _conv3_carry(cu, cw_ref, ext_scr, carry_scr.at[c], i % seq_tiles == 0)
                st_ref[0] = cu[cu.shape[0] - 2:, :]
            else:
                conv, last2 = _conv3_seqs(cu, cw_ref, state_ref)
                st_ref[...] = last2
            y = _dot(h, wa_ref[...]) * conv
            y_scr[c] = y
            ss = jnp.sum(y * y, axis=-1, keepdims=True)
            if c == 0:
                ss_scr[...] = ss
            else:
                ss_scr[...] += ss
            if c == CONV_STEPS - 1:
                inv = lax.rsqrt(ss_scr[...] * (1.0 / CONV_CH) + EPS)
                for cc in range(CONV_STEPS):
                    sl = slice(cc * COL_TILE, (cc + 1) * COL_TILE)
                    nb_ref[:, sl] = (y_scr[cc] * inv * gc_ref[:, sl]).astype(_BF)


def _inproj(x, g, w_in, conv_w, g_conv, state, *, seq_len, q_dtype):
    n = x.shape[0]
    tm = min(IN_ROWS, n)
    nt = n // tm
    sub_steps = tm // SUB_ROWS
    carry_mode = seq_len >= tm
    seq_tiles = seq_len // tm if carry_mode else None

    def col(s):
        return jnp.maximum(s - sub_steps, 0)

    def conv_col(s):
        return jnp.maximum(s - sub_steps - CONV_STEP0, 0)

    def st_map(i, s):
        return (i, 0, conv_col(s))

    in_specs = [
        pl.BlockSpec((SUB_ROWS, D_MODEL), lambda i, s: (i * sub_steps + jnp.minimum(s, sub_steps - 1), 0)),
        pl.BlockSpec((1, D_MODEL), lambda i, s: (0, 0)),
        pl.BlockSpec((D_MODEL, COL_TILE), lambda i, s: (0, col(s))),
        pl.BlockSpec((D_MODEL, COL_TILE), lambda i, s: (0, C_BLK0 + conv_col(s))),
        pl.BlockSpec((D_MODEL, COL_TILE), lambda i, s: (0, U_BLK0 + conv_col(s))),
        pl.BlockSpec((3, COL_TILE), lambda i, s: (0, conv_col(s))),
        pl.BlockSpec((1, CONV_CH), lambda i, s: (0, 0)),
    ]
    args = [x, g, w_in, w_in, w_in, conv_w, g_conv]
    scratch = [
        pltpu.VMEM((tm, D_MODEL), _BF),
        pltpu.VMEM((CONV_STEPS, tm, COL_TILE), _F32),
        pltpu.VMEM((tm, 1), _F32),
    ]
    if carry_mode:
        st_shape = jax.ShapeDtypeStruct((nt, 2, CONV_CH), _F32)
        st_spec = pl.BlockSpec((1, 2, COL_TILE), st_map)
        scratch += [pltpu.VMEM((tm + SUBLANES, COL_TILE), _F32),
                    pltpu.VMEM((CONV_STEPS, SUBLANES, COL_TILE), _F32)]
    else:
        nseq = tm // seq_len
        st_shape = jax.ShapeDtypeStruct((n // seq_len, 2, CONV_CH), _F32)
        st_spec = pl.BlockSpec((nseq, 2, COL_TILE), st_map)
        in_specs.append(st_spec)
        args.append(state)
    out_shape = [
        jax.ShapeDtypeStruct((n, ATTN_WIDTH), q_dtype),
        jax.ShapeDtypeStruct((n, 2 * KV_WIDTH), _F32),
        jax.ShapeDtypeStruct((n, CONV_CH), _BF),
        st_shape,
    ]
    out_specs = [
        pl.BlockSpec((tm, COL_TILE), lambda i, s: (i, jnp.minimum(col(s), Q_STEPS - 1))),
        pl.BlockSpec((tm, 2 * KV_WIDTH), lambda i, s: (i, 0)),
        pl.BlockSpec((tm, CONV_CH), lambda i, s: (i, 0)),
        st_spec,
    ]
    return pl.pallas_call(
        functools.partial(_inproj_kernel, seq_tiles=seq_tiles, sub_steps=sub_steps, q_dtype=q_dtype),
        grid=(nt, sub_steps + IN_STEPS),
        in_specs=in_specs,
        out_specs=out_specs,
        out_shape=out_shape,
        scratch_shapes=scratch,
        compiler_params=pltpu.CompilerParams(
            dimension_semantics=("arbitrary", "arbitrary"), vmem_limit_bytes=VMEM_LIMIT),
        name="inproj_carry" if carry_mode else "inproj_seqs",
    )(*args)


def _softmax_sink(s, sink):
    m = jnp.maximum(jnp.max(s, axis=-1, keepdims=True), sink)
    p = jnp.exp(s - m)
    return p, jnp.sum(p, axis=-1, keepdims=True) + jnp.exp(sink - m)


def _half_lane_operands(slab, low_half):
    lo = lax.broadcasted_iota(jnp.int32, slab.shape, 1) < HEAD_DIM
    moved = pltpu.roll(slab, HEAD_DIM, 1)
    in_low = jnp.where(lo, slab if low_half else moved, 0.0).astype(_BF)
    in_high = jnp.where(lo, 0.0, moved if low_half else slab).astype(_BF)
    return in_low, in_high


def _attn_prompt_kernel(sinks_ref, q_ref, kvc_ref, kvp_ref, g_ref, o_ref, acc_ref):
    i = pl.program_id(1)
    pair_w = 2 * HEAD_DIM
    qi = lax.broadcasted_iota(jnp.int32, (BLOCK, 2 * BLOCK), 0)
    kj = lax.broadcasted_iota(jnp.int32, (BLOCK, 2 * BLOCK), 1)
    dist_i = qi - kj + BLOCK
    valid = (dist_i >= 0) & (dist_i <= WINDOW) & ((kj >= BLOCK) | (i > 0))
    dist = jnp.where(valid, dist_i.astype(_F32), -NEG / min(_SLOPES))
    low = lax.broadcasted_iota(jnp.int32, (BLOCK, pair_w), 1) < HEAD_DIM

    k_ops, v_ops = [], []
    for m in range(N_KV_HEADS // 2):
        ksl = slice(m * pair_w, (m + 1) * pair_w)
        vsl = slice(KV_WIDTH + m * pair_w, KV_WIDTH + (m + 1) * pair_w)
        kslab = jnp.concatenate([kvp_ref[:, ksl], kvc_ref[:, ksl]], axis=0)
        vslab = jnp.concatenate([kvp_ref[:, vsl], kvc_ref[:, vsl]], axis=0)
        for low_half in (True, False):
            k_ops.append(_half_lane_operands(kslab, low_half))
            v_ops.append(_half_lane_operands(vslab, low_half))

    def scores(c):
        qc = q_ref[:, c * pair_w:(c + 1) * pair_w]
        k_low, k_high = k_ops[c // 2]
        return _dot_nt(qc, k_low), _dot_nt(qc, k_high)

    n_pairs = N_HEADS // 2
    s_next = scores(0)
    for c in range(n_pairs):
        s_pair = s_next
        if c + 1 < n_pairs:
            s_next = scores(c + 1)
        v_low, v_high = v_ops[c // 2]
        p0, d0 = _softmax_sink(s_pair[0] - _SLOPES[2 * c] * dist, sinks_ref[0, 2 * c])
        p1, d1 = _softmax_sink(s_pair[1] - _SLOPES[2 * c + 1] * dist, sinks_ref[0, 2 * c + 1])
        o = _dot(p0.astype(_BF), v_low) + _dot(p1.astype(_BF), v_high)
        acc_ref[:, c * pair_w:(c + 1) * pair_w] = o / jnp.where(low, d0, d1)
    o_ref[...] = _rms(acc_ref[...], g_ref[...]).astype(_BF)


def _attn_prompt(q, kv, sinks, g, *, batch, seq):
    nb = seq // BLOCK
    return pl.pallas_call(
        _attn_prompt_kernel,
        grid=(batch, nb),
        in_specs=[
            pl.BlockSpec(memory_space=pltpu.SMEM),
            pl.BlockSpec((BLOCK, ATTN_WIDTH), lambda b, i: (b * nb + i, 0)),
            pl.BlockSpec((BLOCK, 2 * KV_WIDTH), lambda b, i: (b * nb + i, 0)),
            pl.BlockSpec((BLOCK, 2 * KV_WIDTH), lambda b, i: (b * nb + jnp.maximum(i - 1, 0), 0)),
            pl.BlockSpec((1, ATTN_WIDTH), lambda b, i: (0, 0)),
        ],
        out_specs=pl.BlockSpec((BLOCK, ATTN_WIDTH), lambda b, i: (b * nb + i, 0)),
        out_shape=jax.ShapeDtypeStruct((batch * seq, ATTN_WIDTH), _BF),
        scratch_shapes=[pltpu.VMEM((BLOCK, ATTN_WIDTH), _F32)],
        compiler_params=pltpu.CompilerParams(dimension_semantics=("arbitrary", "arbitrary")),
        name="attn_prompt",
    )(sinks, q, kv, kv, g)


def _round_bf16(x):
    return x.astype(_BF).astype(_F32)


def _attn_sample_kernel(sinks_ref, q_ref, kvn_ref, ck_ref, cv_ref, g_ref,
                        o_ref, kwin_ref, vwin_ref, acc_ref, s_scr, *, nseq, t_new):
    rows = GROUP * t_new
    nkeys = WINDOW + t_new
    ksl = [slice(kvh * HEAD_DIM, (kvh + 1) * HEAD_DIM) for kvh in range(N_KV_HEADS)]
    vsl = [slice(KV_WIDTH + kvh * HEAD_DIM, KV_WIDTH + (kvh + 1) * HEAD_DIM) for kvh in range(N_KV_HEADS)]

    for kvh in range(N_KV_HEADS):
        for b in range(nseq):
            rs = slice(b * t_new, (b + 1) * t_new)
            k = _round_bf16(jnp.concatenate([ck_ref[b, :, ksl[kvh]], kvn_ref[rs, ksl[kvh]]], axis=0))
            qs = jnp.concatenate(
                [q_ref[rs, (kvh * GROUP + gq) * HEAD_DIM:(kvh * GROUP + gq + 1) * HEAD_DIM]
                 for gq in range(GROUP)], axis=0)
            s_scr[kvh, b * rows:(b + 1) * rows, :] = _dot_nt(_round_bf16(qs), k)

    r = lax.broadcasted_iota(jnp.int32, (nseq * rows, 1), 0)
    gq_of_row = (r // t_new) % GROUP
    t = lax.broadcasted_iota(jnp.int32, (nseq * rows, nkeys), 0) % t_new
    kj = lax.broadcasted_iota(jnp.int32, (nseq * rows, nkeys), 1)
    dist_i = t + WINDOW - kj
    valid = (dist_i >= 0) & (dist_i <= WINDOW)
    dist = jnp.where(valid, dist_i.astype(_F32), -NEG / min(_SLOPES))
    for kvh in range(N_KV_HEADS):
        slope = jnp.full((nseq * rows, 1), _SLOPES[kvh * GROUP], _F32)
        sink = jnp.full((nseq * rows, 1), sinks_ref[0, kvh * GROUP], _F32)
        for gq in range(1, GROUP):
            slope = jnp.where(gq_of_row == gq, _SLOPES[kvh * GROUP + gq], slope)
            sink = jnp.where(gq_of_row == gq, sinks_ref[0, kvh * GROUP + gq], sink)
        p, denom = _softmax_sink(s_scr[kvh] - slope * dist, sink)
        s_scr[kvh] = _round_bf16(p / denom)

    for kvh in range(N_KV_HEADS):
        for b in range(nseq):
            rs = slice(b * t_new, (b + 1) * t_new)
            v = _round_bf16(jnp.concatenate([cv_ref[b, :, ksl[kvh]], kvn_ref[rs, vsl[kvh]]], axis=0))
            o = _dot(s_scr[kvh, b * rows:(b + 1) * rows, :], v)
            for gq in range(GROUP):
                h = kvh * GROUP + gq
                acc_ref[rs, h * HEAD_DIM:(h + 1) * HEAD_DIM] = o[gq * t_new:(gq + 1) * t_new, :]
    o_ref[...] = _rms(acc_ref[...], g_ref[...]).astype(_BF)
    kwin_ref[:, 0:WINDOW - t_new, :] = ck_ref[:, t_new:, :]
    vwin_ref[:, 0:WINDOW - t_new, :] = cv_ref[:, t_new:, :]
    kwin_ref[:, WINDOW - t_new:, :] = kvn_ref[:, 0:KV_WIDTH].reshape(nseq, t_new, KV_WIDTH)
    vwin_ref[:, WINDOW - t_new:, :] = kvn_ref[:, KV_WIDTH:].reshape(nseq, t_new, KV_WIDTH)


def _attn_sample(q, kv, cache_k, cache_v, sinks, g, *, nseq_total, t_new, nseq=8):
    rows = nseq * t_new
    cache_spec = pl.BlockSpec((nseq, WINDOW, KV_WIDTH), lambda i: (i, 0, 0))
    return pl.pallas_call(
        functools.partial(_attn_sample_kernel, nseq=nseq, t_new=t_new),
        grid=(nseq_total // nseq,),
        in_specs=[
            pl.BlockSpec(memory_space=pltpu.SMEM),
            pl.BlockSpec((rows, ATTN_WIDTH), lambda i: (i, 0)),
            pl.BlockSpec((rows, 2 * KV_WIDTH), lambda i: (i, 0)),
            cache_spec,
            cache_spec,
            pl.BlockSpec((1, ATTN_WIDTH), lambda i: (0, 0)),
        ],
        out_specs=[pl.BlockSpec((rows, ATTN_WIDTH), lambda i: (i, 0)), cache_spec, cache_spec],
        out_shape=[
            jax.ShapeDtypeStruct((nseq_total * t_new, ATTN_WIDTH), _BF),
            jax.ShapeDtypeStruct(cache_k.shape, _F32),
            jax.ShapeDtypeStruct(cache_v.shape, _F32),
        ],
        scratch_shapes=[pltpu.VMEM((rows, ATTN_WIDTH), _F32),
                        pltpu.VMEM((N_KV_HEADS, nseq * GROUP * t_new, WINDOW + t_new), _F32)],
        compiler_params=pltpu.CompilerParams(dimension_semantics=("arbitrary",)),
        name="attn_sample",
    )(sinks, q, kv, cache_k, cache_v, g)


def _outproj_kernel(x_ref, na_ref, nb_ref, wa_ref, wb_ref, o_ref):
    o_ref[...] = x_ref[...] + _dot(na_ref[...], wa_ref[...]) + _dot(nb_ref[...], wb_ref[...])


def _outproj(x, na, nb, w_out):
    n = x.shape[0]
    tm = ROW_TILE
    return pl.pallas_call(
        _outproj_kernel,
        grid=(n // tm,),
        in_specs=[
            pl.BlockSpec((tm, D_MODEL), lambda i: (i, 0)),
            pl.BlockSpec((tm, ATTN_WIDTH), lambda i: (i, 0)),
            pl.BlockSpec((tm, CONV_CH), lambda i: (i, 0)),
            pl.BlockSpec((ATTN_WIDTH, D_MODEL), lambda i: (0, 0)),
            pl.BlockSpec((CONV_CH, D_MODEL), lambda i: (1, 0)),
        ],
        out_specs=pl.BlockSpec((tm, D_MODEL), lambda i: (i, 0)),
        out_shape=jax.ShapeDtypeStruct((n, D_MODEL), _F32),
        compiler_params=pltpu.CompilerParams(
            dimension_semantics=("arbitrary",), vmem_limit_bytes=VMEM_LIMIT),
        name="outproj",
    )(x, na, nb, w_out, w_out)


def _ffn_kernel(*refs, seq_tiles, sub_steps):
    carry_mode = seq_tiles is not None
    if carry_mode:
        (x_ref, g_ref, wg_ref, wu_ref, cw_ref, cb_ref, wd_ref, gf_ref,
         o_ref, st_ref, h_scr, acc_scr, act_scr, ext_scr, carry_scr) = refs
    else:
        (x_ref, g_ref, wg_ref, wu_ref, cw_ref, cb_ref, wd_ref, gf_ref, state_ref,
         o_ref, st_ref, h_scr, acc_scr, act_scr) = refs
    i = pl.program_id(0)
    j = pl.program_id(1)
    t = j - sub_steps
    sub = x_ref.shape[0]

    @pl.when(j < sub_steps)
    def _():
        rows = pl.ds(pl.multiple_of(j * sub, sub), sub)
        x = x_ref[...]
        acc_scr[rows, :] = x
        h_scr[rows, :] = _rms(x, g_ref[...]).astype(_BF)

    @pl.when(j == 0)
    def _():
        act_scr[1] = jnp.zeros(act_scr.shape[1:], _BF)

    @pl.when((t >= 0) & (t < FF_STEPS))
    def _():
        h = h_scr[...]
        gate = _dot(h, wg_ref[...])
        up = _dot(h, wu_ref[...])
        acc_scr[...] += _dot(act_scr[(t + 1) % 2], wd_ref[...])
        if carry_mode:
            conv = _conv3_carry(gate, cw_ref, ext_scr, carry_scr.at[t], i % seq_tiles == 0)
            st_ref[0] = gate[gate.shape[0] - 2:, :]
        else:
            conv, last2 = _conv3_seqs(gate, cw_ref, state_ref)
            st_ref[...] = last2
        a = conv + cb_ref[...]
        act_scr[t % 2] = ((a / (1.0 + jnp.exp(-a))) * up).astype(_BF)

    @pl.when(t == FF_STEPS)
    def _():
        acc_scr[...] += _dot(act_scr[(FF_STEPS - 1) % 2], wd_ref[...])

    @pl.when(t > FF_STEPS)
    def _():
        rows = pl.ds(pl.multiple_of((t - FF_STEPS - 1) * sub, sub), sub)
        o_ref[...] = _rms(acc_scr[rows, :], gf_ref[...])


def _ffn(x, g, w_gate, w_up, conv_w, conv_b, w_down, g_final, state, *, seq_len, tm):
    n = x.shape[0]
    nt = n // tm
    sub_steps = tm // SUB_ROWS
    carry_mode = seq_len >= tm
    seq_tiles = seq_len // tm if carry_mode else None

    def ff_idx(j):
        return jnp.clip(j - sub_steps, 0, FF_STEPS - 1)

    def st_map(i, j):
        return (i, 0, ff_idx(j))

    in_specs = [
        pl.BlockSpec((SUB_ROWS, D_MODEL), lambda i, j: (i * sub_steps + jnp.minimum(j, sub_steps - 1), 0)),
        pl.BlockSpec((1, D_MODEL), lambda i, j: (0, 0)),
        pl.BlockSpec((D_MODEL, FF_TILE), lambda i, j: (0, ff_idx(j))),
        pl.BlockSpec((D_MODEL, FF_TILE), lambda i, j: (0, ff_idx(j))),
        pl.BlockSpec((3, FF_TILE), lambda i, j: (0, ff_idx(j))),
        pl.BlockSpec((1, FF_TILE), lambda i, j: (0, ff_idx(j))),
        pl.BlockSpec((FF_TILE, D_MODEL), lambda i, j: (ff_idx(j - 1), 0)),
        pl.BlockSpec((1, D_MODEL), lambda i, j: (0, 0)),
    ]
    args = [x, g, w_gate, w_up, conv_w, conv_b, w_down, g_final]
    scratch = [pltpu.VMEM((tm, D_MODEL), _BF), pltpu.VMEM((tm, D_MODEL), _F32),
               pltpu.VMEM((2, tm, FF_TILE), _BF)]
    if carry_mode:
        st_shape = jax.ShapeDtypeStruct((nt, 2, D_FF), _F32)
        st_spec = pl.BlockSpec((1, 2, FF_TILE), st_map)
        scratch += [pltpu.VMEM((tm + SUBLANES, FF_TILE), _F32),
                    pltpu.VMEM((FF_STEPS, SUBLANES, FF_TILE), _F32)]
    else:
        nseq = tm // seq_len
        st_shape = jax.ShapeDtypeStruct((n // seq_len, 2, D_FF), _F32)
        st_spec = pl.BlockSpec((nseq, 2, FF_TILE), st_map)
        in_specs.append(st_spec)
        args.append(state)
    out_spec = pl.BlockSpec(
        (SUB_ROWS, D_MODEL), lambda i, j: (i * sub_steps + jnp.maximum(j - sub_steps - FF_STEPS - 1, 0), 0))
    return pl.pallas_call(
        functools.partial(_ffn_kernel, seq_tiles=seq_tiles, sub_steps=sub_steps),
        grid=(nt, FF_STEPS + 1 + 2 * sub_steps),
        in_specs=in_specs,
        out_specs=[out_spec, st_spec],
        out_shape=[jax.ShapeDtypeStruct((n, D_MODEL), _F32), st_shape],
        scratch_shapes=scratch,
        compiler_params=pltpu.CompilerParams(
            dimension_semantics=("arbitrary", "arbitrary"), vmem_limit_bytes=VMEM_LIMIT),
        name="ffn_carry" if carry_mode else "ffn_seqs",
    )(*args)


def kernel(x_prompt, x_sample, cache_k_window, cache_v_window, state_conv, state_ffn_conv, g_attn_norm, w_in, attn_sinks, conv_w, g_out_attn, g_out_conv, w_out, g_ffn_norm, w_gate, w_up, ffn_conv_w, ffn_conv_b, w_down, g_final):
    assert w_in.shape[0] == 1, "single layer"
    bp, sp, _ = x_prompt.shape
    bs, ts, _ = x_sample.shape
    assert sp % IN_ROWS == 0 and (bs * ts) % IN_ROWS == 0 and ts == SUBLANES

    w_in_b = w_in[0].astype(_BF)
    w_out_b = w_out[0].astype(_BF)
    w_gate_b = w_gate[0].astype(_BF)
    w_up_b = w_up[0].astype(_BF)
    w_down_b = w_down[0].astype(_BF)
    g_attn = g_attn_norm[0].reshape(1, D_MODEL)
    g_ffn = g_ffn_norm[0].reshape(1, D_MODEL)
    g_fin = g_final.reshape(1, D_MODEL)
    g_oa = g_out_attn[0].reshape(1, ATTN_WIDTH)
    g_oc = g_out_conv[0].reshape(1, CONV_CH)
    sinks = attn_sinks[0].reshape(1, N_HEADS)
    cw = conv_w[0]
    fcw = ffn_conv_w[0]
    fcb = ffn_conv_b[0].reshape(1, D_FF)

    xp = x_prompt.reshape(bp * sp, D_MODEL)
    q_p, kv_p, nb_p, cst_p = _inproj(xp, g_attn, w_in_b, cw, g_oc, None, seq_len=sp, q_dtype=_BF)
    na_p = _attn_prompt(q_p, kv_p, sinks, g_oa, batch=bp, seq=sp)
    x1_p = _outproj(xp, na_p, nb_p, w_out_b)
    ff_rows = min(FF_ROWS_PROMPT, sp)
    y_p, fst_p = _ffn(x1_p, g_ffn, w_gate_b, w_up_b, fcw, fcb, w_down_b, g_fin, None, seq_len=sp, tm=ff_rows)

    seq_tiles = sp // min(IN_ROWS, sp)
    ff_seq_tiles = sp // ff_rows
    kv_p3 = kv_p.reshape(bp, sp, 2 * KV_WIDTH)[:, sp - WINDOW:, :]
    k_win_p = kv_p3[:, :, :KV_WIDTH].reshape(1, bp, WINDOW, N_KV_HEADS, HEAD_DIM)
    v_win_p = kv_p3[:, :, KV_WIDTH:].reshape(1, bp, WINDOW, N_KV_HEADS, HEAD_DIM)
    conv_p = cst_p[seq_tiles - 1::seq_tiles][None]
    ffn_conv_p = fst_p[ff_seq_tiles - 1::ff_seq_tiles][None]

    xs = x_sample.reshape(bs * ts, D_MODEL)
    q_s, kv_s, nb_s, cst_s = _inproj(xs, g_attn, w_in_b, cw, g_oc, state_conv[0], seq_len=ts, q_dtype=_F32)
    ck = cache_k_window[0].reshape(bs, WINDOW, KV_WIDTH)
    cv = cache_v_window[0].reshape(bs, WINDOW, KV_WIDTH)
    na_s, k_win_s, v_win_s = _attn_sample(q_s, kv_s, ck, cv, sinks, g_oa, nseq_total=bs, t_new=ts)
    x1_s = _outproj(xs, na_s, nb_s, w_out_b)
    y_s, fst_s = _ffn(x1_s, g_ffn, w_gate_b, w_up_b, fcw, fcb, w_down_b, g_fin, state_ffn_conv[0],
                      seq_len=ts, tm=min(FF_ROWS_SAMPLE, bs * ts))

    win_shape = (1, bs, WINDOW, N_KV_HEADS, HEAD_DIM)
    return (y_p.reshape(bp, sp, D_MODEL), y_s.reshape(bs, ts, D_MODEL),
            k_win_p, v_win_p, conv_p, ffn_conv_p,
            k_win_s.reshape(win_shape), v_win_s.reshape(win_shape), cst_s[None], fst_s[None])
```

```python
import functools

import jax
import jax.numpy as jnp
from jax import lax
from jax.experimental import pallas as pl
from jax.experimental.pallas import tpu as pltpu

D_MODEL = 2048
ATTN_WIDTH = 1024
CONV_CH = 1024
HEAD_DIM = 64
N_HEADS = 16
N_KV_HEADS = 4
GROUP = 4
KV_WIDTH = 256
WINDOW = 128
BLOCK = 128
D_FF = 5632
EPS = 1e-6
NEG = -1e30

SUBLANES = 8
ROW_TILE = 512
SUB_ROWS = 512
IN_ROWS = 1024
COL_TILE = 512
Q_STEPS = ATTN_WIDTH // COL_TILE
KV_STEP = Q_STEPS
CONV_STEP0 = KV_STEP + 1
CONV_STEPS = CONV_CH // COL_TILE
IN_STEPS = CONV_STEP0 + CONV_STEPS
FF_ROWS = 512
FF_TILE = 512
FF_STEPS = D_FF // FF_TILE
B_BLK0 = (ATTN_WIDTH + 2 * KV_WIDTH) // COL_TILE
C_BLK0 = B_BLK0 + CONV_STEPS
U_BLK0 = C_BLK0 + CONV_STEPS
VMEM_LIMIT = 56 * 1024 * 1024

_SLOPES = [2.0 ** (-8.0 * (h + 1) / N_HEADS) for h in range(N_HEADS)]
_BF = jnp.bfloat16
_F32 = jnp.float32


def _rms(x, g):
    return x * lax.rsqrt(jnp.mean(x * x, axis=-1, keepdims=True) + EPS) * g


def _dot(a, b):
    return jnp.dot(a, b, preferred_element_type=_F32)


def _dot_nt(a, b):
    return lax.dot_general(a, b, (((1,), (1,)), ((), ())), preferred_element_type=_F32)


def _conv3_carry(cur, w_ref, ext_ref, carry_ref, is_seq_start):
    rows = cur.shape[0]
    prev = jnp.where(is_seq_start, 0.0, carry_ref[...])
    ext_ref[0:SUBLANES, :] = prev
    ext_ref[SUBLANES:SUBLANES + rows, :] = cur
    carry_ref[...] = cur[rows - SUBLANES:, :]
    p1 = ext_ref[SUBLANES - 1:SUBLANES - 1 + rows, :]
    p2 = ext_ref[SUBLANES - 2:SUBLANES - 2 + rows, :]
    return w_ref[2:3, :] * cur + w_ref[1:2, :] * p1 + w_ref[0:1, :] * p2


def _conv3_seqs(cur, w_ref, state_ref):
    rows, ch = cur.shape
    nseq = rows // SUBLANES
    cur3 = cur.reshape(nseq, SUBLANES, ch)
    st = state_ref[...]
    t = lax.broadcasted_iota(jnp.int32, cur3.shape, 1)
    s0 = jnp.broadcast_to(st[:, 0:1, :], cur3.shape)
    s1 = jnp.broadcast_to(st[:, 1:2, :], cur3.shape)
    p1 = jnp.where(t == 0, s1, pltpu.roll(cur3, 1, 1))
    p2 = jnp.where(t == 0, s0, jnp.where(t == 1, s1, pltpu.roll(cur3, 2, 1)))
    w = w_ref[...]
    out = w[2:3, :][None] * cur3 + w[1:2, :][None] * p1 + w[0:1, :][None] * p2
    return out.reshape(rows, ch), cur3[:, SUBLANES - 2:, :]


def _inproj_kernel(*refs, seq_tiles, sub_steps, q_dtype):
    carry_mode = seq_tiles is not None
    if carry_mode:
        (x_ref, g_ref, wa_ref, wb_ref, wc_ref, cw_ref, gc_ref,
         q_ref, kv_ref, nb_ref, st_ref, h_scr, y_scr, ss_scr, ext_scr, carry_scr) = refs
    else:
        (x_ref, g_ref, wa_ref, wb_ref, wc_ref, cw_ref, gc_ref, state_ref,
         q_ref, kv_ref, nb_ref, st_ref, h_scr, y_scr, ss_scr) = refs
    i = pl.program_id(0)
    j = pl.program_id(1) - sub_steps
    sub = x_ref.shape[0]

    @pl.when(j < 0)
    def _():
        rows = pl.ds(pl.multiple_of((j + sub_steps) * sub, sub), sub)
        h_scr[rows, :] = _rms(x_ref[...], g_ref[...]).astype(_BF)

    @pl.when((j >= 0) & (j < Q_STEPS))
    def _():
        q_ref[...] = (_dot(h_scr[...], wa_ref[...]) * (HEAD_DIM ** -0.5)).astype(q_dtype)

    @pl.when(j == KV_STEP)
    def _():
        kv_ref[...] = _dot(h_scr[...], wa_ref[...])

    for c in range(CONV_STEPS):
        @pl.when(j == CONV_STEP0 + c)
        def _(c=c):
            h = h_scr[...]
            cu = _dot(h, wb_ref[...]) * _dot(h, wc_ref[...])
            if carry_mode:
                conv = _conv3_carry(cu, cw_ref, ext_scr, carry_scr.at[c], i % seq_tiles == 0)
                st_ref[0] = cu[cu.shape[0] - 2:, :]
            else:
                conv, last2 = _conv3_seqs(cu, cw_ref, state_ref)
                st_ref[...] = last2
            y = _dot(h, wa_ref[...]) * conv
            y_scr[c] = y
            ss = jnp.sum(y * y, axis=-1, keepdims=True)
            if c == 0:
                ss_scr[...] = ss
            else:
                ss_scr[...] += ss
            if c == CONV_STEPS - 1:
                inv = lax.rsqrt(ss_scr[...] * (1.0 / CONV_CH) + EPS)
                for cc in range(CONV_STEPS):
                    sl = slice(cc * COL_TILE, (cc + 1) * COL_TILE)
                    nb_ref[:, sl] = (y_scr[cc] * inv * gc_ref[:, sl]).astype(_BF)


def _inproj(x, g, w_in, conv_w, g_conv, state, *, seq_len, q_dtype):
    n = x.shape[0]
    tm = min(IN_ROWS, n)
    nt = n // tm
    sub_steps = tm // SUB_ROWS
    carry_mode = seq_len >= tm
    seq_tiles = seq_len // tm if carry_mode else None

    def col(s):
        return jnp.maximum(s - sub_steps, 0)

    def conv_col(s):
        return jnp.maximum(s - sub_steps - CONV_STEP0, 0)

    def st_map(i, s):
        return (i, 0, conv_col(s))

    in_specs = [
        pl.BlockSpec((SUB_ROWS, D_MODEL), lambda i, s: (i * sub_steps + jnp.minimum(s, sub_steps - 1), 0)),
        pl.BlockSpec((1, D_MODEL), lambda i, s: (0, 0)),
        pl.BlockSpec((D_MODEL, COL_TILE), lambda i, s: (0, col(s))),
        pl.BlockSpec((D_MODEL, COL_TILE), lambda i, s: (0, C_BLK0 + conv_col(s))),
        pl.BlockSpec((D_MODEL, COL_TILE), lambda i, s: (0, U_BLK0 + conv_col(s))),
        pl.BlockSpec((3, COL_TILE), lambda i, s: (0, conv_col(s))),
        pl.BlockSpec((1, CONV_CH), lambda i, s: (0, 0)),
    ]
    args = [x, g, w_in, w_in, w_in, conv_w, g_conv]
    scratch = [
        pltpu.VMEM((tm, D_MODEL), _BF),
        pltpu.VMEM((CONV_STEPS, tm, COL_TILE), _F32),
        pltpu.VMEM((tm, 1), _F32),
    ]
    if carry_mode:
        st_shape = jax.ShapeDtypeStruct((nt, 2, CONV_CH), _F32)
        st_spec = pl.BlockSpec((1, 2, COL_TILE), st_map)
        scratch += [pltpu.VMEM((tm + SUBLANES, COL_TILE), _F32),
                    pltpu.VMEM((CONV_STEPS, SUBLANES, COL_TILE), _F32)]
    else:
        nseq = tm // seq_len
        st_shape = jax.ShapeDtypeStruct((n // seq_len, 2, CONV_CH), _F32)
        st_spec = pl.BlockSpec((nseq, 2, COL_TILE), st_map)
        in_specs.append(st_spec)
        args.append(state)
    out_shape = [
        jax.ShapeDtypeStruct((n, ATTN_WIDTH), q_dtype),
        jax.ShapeDtypeStruct((n, 2 * KV_WIDTH), _F32),
        jax.ShapeDtypeStruct((n, CONV_CH), _BF),
        st_shape,
    ]
    out_specs = [
        pl.BlockSpec((tm, COL_TILE), lambda i, s: (i, jnp.minimum(col(s), Q_STEPS - 1))),
        pl.BlockSpec((tm, 2 * KV_WIDTH), lambda i, s: (i, 0)),
        pl.BlockSpec((tm, CONV_CH), lambda i, s: (i, 0)),
        st_spec,
    ]
    return pl.pallas_call(
        functools.partial(_inproj_kernel, seq_tiles=seq_tiles, sub_steps=sub_steps, q_dtype=q_dtype),
        grid=(nt, sub_steps + IN_STEPS),
        in_specs=in_specs,
        out_specs=out_specs,
        out_shape=out_shape,
        scratch_shapes=scratch,
        compiler_params=pltpu.CompilerParams(
            dimension_semantics=("arbitrary", "arbitrary"), vmem_limit_bytes=VMEM_LIMIT),
        name="inproj_carry" if carry_mode else "inproj_seqs",
    )(*args)


def _softmax_sink(s, sink):
    m = jnp.maximum(jnp.max(s, axis=-1, keepdims=True), sink)
    p = jnp.exp(s - m)
    return p, jnp.sum(p, axis=-1, keepdims=True) + jnp.exp(sink - m)


def _half_lane_operands(slab, low_half):
    lo = lax.broadcasted_iota(jnp.int32, slab.shape, 1) < HEAD_DIM
    moved = pltpu.roll(slab, HEAD_DIM, 1)
    in_low = jnp.where(lo, slab if low_half else moved, 0.0).astype(_BF)
    in_high = jnp.where(lo, 0.0, moved if low_half else slab).astype(_BF)
    return in_low, in_high


def _attn_prompt_kernel(sinks_ref, q_ref, kvc_ref, kvp_ref, g_ref, o_ref, acc_ref):
    i = pl.program_id(1)
    pair_w = 2 * HEAD_DIM
    qi = lax.broadcasted_iota(jnp.int32, (BLOCK, 2 * BLOCK), 0)
    kj = lax.broadcasted_iota(jnp.int32, (BLOCK, 2 * BLOCK), 1)
    dist_i = qi - kj + BLOCK
    valid = (dist_i >= 0) & (dist_i <= WINDOW) & ((kj >= BLOCK) | (i > 0))
    dist = jnp.where(valid, dist_i.astype(_F32), -NEG / min(_SLOPES))
    low = lax.broadcasted_iota(jnp.int32, (BLOCK, pair_w), 1) < HEAD_DIM

    k_ops, v_ops = [], []
    for m in range(N_KV_HEADS // 2):
        ksl = slice(m * pair_w, (m + 1) * pair_w)
        vsl = slice(KV_WIDTH + m * pair_w, KV_WIDTH + (m + 1) * pair_w)
        kslab = jnp.concatenate([kvp_ref[:, ksl], kvc_ref[:, ksl]], axis=0)
        vslab = jnp.concatenate([kvp_ref[:, vsl], kvc_ref[:, vsl]], axis=0)
        for low_half in (True, False):
            k_ops.append(_half_lane_operands(kslab, low_half))
            v_ops.append(_half_lane_operands(vslab, low_half))

    def scores(c):
        qc = q_ref[:, c * pair_w:(c + 1) * pair_w]
        k_low, k_high = k_ops[c // 2]
        return _dot_nt(qc, k_low), _dot_nt(qc, k_high)

    n_pairs = N_HEADS // 2
    s_next = scores(0)
    for c in range(n_pairs):
        s_pair = s_next
        if c + 1 < n_pairs:
            s_next = scores(c + 1)
        v_low, v_high = v_ops[c // 2]
        p0, d0 = _softmax_sink(s_pair[0] - _SLOPES[2 * c] * dist, sinks_ref[0, 2 * c])
        p1, d1 = _softmax_sink(s_pair[1] - _SLOPES[2 * c + 1] * dist, sinks_ref[0, 2 * c + 1])
        o = _dot(p0.astype(_BF), v_low) + _dot(p1.astype(_BF), v_high)
        acc_ref[:, c * pair_w:(c + 1) * pair_w] = o / jnp.where(low, d0, d1)
    o_ref[...] = _rms(acc_ref[...], g_ref[...]).astype(_BF)


def _attn_prompt(q, kv, sinks, g, *, batch, seq):
    nb = seq // BLOCK
    return pl.pallas_call(
        _attn_prompt_kernel,
        grid=(batch, nb),
        in_specs=[
            pl.BlockSpec(memory_space=pltpu.SMEM),
            pl.BlockSpec((BLOCK, ATTN_WIDTH), lambda b, i: (b * nb + i, 0)),
            pl.BlockSpec((BLOCK, 2 * KV_WIDTH), lambda b, i: (b * nb + i, 0)),
            pl.BlockSpec((BLOCK, 2 * KV_WIDTH), lambda b, i: (b * nb + jnp.maximum(i - 1, 0), 0)),
            pl.BlockSpec((1, ATTN_WIDTH), lambda b, i: (0, 0)),
        ],
        out_specs=pl.BlockSpec((BLOCK, ATTN_WIDTH), lambda b, i: (b * nb + i, 0)),
        out_shape=jax.ShapeDtypeStruct((batch * seq, ATTN_WIDTH), _BF),
        scratch_shapes=[pltpu.VMEM((BLOCK, ATTN_WIDTH), _F32)],
        compiler_params=pltpu.CompilerParams(dimension_semantics=("arbitrary", "arbitrary")),
        name="attn_prompt",
    )(sinks, q, kv, kv, g)


def _round_bf16(x):
    return x.astype(_BF).astype(_F32)


def _attn_sample_kernel(sinks_ref, q_ref, kvn_ref, ck_ref, cv_ref, g_ref,
                        o_ref, kwin_ref, vwin_ref, acc_ref, s_scr, *, nseq, t_new):
    rows = GROUP * t_new
    nkeys = WINDOW + t_new
    ksl = [slice(kvh * HEAD_DIM, (kvh + 1) * HEAD_DIM) for kvh in range(N_KV_HEADS)]
    vsl = [slice(KV_WIDTH + kvh * HEAD_DIM, KV_WIDTH + (kvh + 1) * HEAD_DIM) for kvh in range(N_KV_HEADS)]

    for kvh in range(N_KV_HEADS):
        for b in range(nseq):
            rs = slice(b * t_new, (b + 1) * t_new)
            k = _round_bf16(jnp.concatenate([ck_ref[b, :, ksl[kvh]], kvn_ref[rs, ksl[kvh]]], axis=0))
            qs = jnp.concatenate(
                [q_ref[rs, (kvh * GROUP + gq) * HEAD_DIM:(kvh * GROUP + gq + 1) * HEAD_DIM]
                 for gq in range(GROUP)], axis=0)
            s_scr[kvh, b * rows:(b + 1) * rows, :] = _dot_nt(_round_bf16(qs), k)

    r = lax.broadcasted_iota(jnp.int32, (nseq * rows, 1), 0)
    gq_of_row = (r // t_new) % GROUP
    t = lax.broadcasted_iota(jnp.int32, (nseq * rows, nkeys), 0) % t_new
    kj = lax.broadcasted_iota(jnp.int32, (nseq * rows, nkeys), 1)
    dist_i = t + WINDOW - kj
    valid = (dist_i >= 0) & (dist_i <= WINDOW)
    dist = jnp.where(valid, dist_i.astype(_F32), -NEG / min(_SLOPES))
    for kvh in range(N_KV_HEADS):
        slope = jnp.full((nseq * rows, 1), _SLOPES[kvh * GROUP], _F32)
        sink = jnp.full((nseq * rows, 1), sinks_ref[0, kvh * GROUP], _F32)
        for gq in range(1, GROUP):
            slope = jnp.where(gq_of_row == gq, _SLOPES[kvh * GROUP + gq], slope)
            sink = jnp.where(gq_of_row == gq, sinks_ref[0, kvh * GROUP + gq], sink)
        p, denom = _softmax_sink(s_scr[kvh] - slope * dist, sink)
        s_scr[kvh] = _round_bf16(p / denom)

    for kvh in range(N_KV_HEADS):
        for b in range(nseq):
            rs = slice(b * t_new, (b + 1) * t_new)
            v = _round_bf16(jnp.concatenate([cv_ref[b, :, ksl[kvh]], kvn_ref[rs, vsl[kvh]]], axis=0))
            o = _dot(s_scr[kvh, b * rows:(b + 1) * rows, :], v)
            for gq in range(GROUP):
                h = kvh * GROUP + gq
                acc_ref[rs, h * HEAD_DIM:(h + 1) * HEAD_DIM] = o[gq * t_new:(gq + 1) * t_new, :]
    o_ref[...] = _rms(acc_ref[...], g_ref[...]).astype(_BF)
    kwin_ref[:, 0:WINDOW - t_new, :] = ck_ref[:, t_new:, :]
    vwin_ref[:, 0:WINDOW - t_new, :] = cv_ref[:, t_new:, :]
    kwin_ref[:, WINDOW - t_new:, :] = kvn_ref[:, 0:KV_WIDTH].reshape(nseq, t_new, KV_WIDTH)
    vwin_ref[:, WINDOW - t_new:, :] = kvn_ref[:, KV_WIDTH:].reshape(nseq, t_new, KV_WIDTH)


def _attn_sample(q, kv, cache_k, cache_v, sinks, g, *, nseq_total, t_new, nseq=8):
    rows = nseq * t_new
    cache_spec = pl.BlockSpec((nseq, WINDOW, KV_WIDTH), lambda i: (i, 0, 0))
    return pl.pallas_call(
        functools.partial(_attn_sample_kernel, nseq=nseq, t_new=t_new),
        grid=(nseq_total // nseq,),
        in_specs=[
            pl.BlockSpec(memory_space=pltpu.SMEM),
            pl.BlockSpec((rows, ATTN_WIDTH), lambda i: (i, 0)),
            pl.BlockSpec((rows, 2 * KV_WIDTH), lambda i: (i, 0)),
            cache_spec,
            cache_spec,
            pl.BlockSpec((1, ATTN_WIDTH), lambda i: (0, 0)),
        ],
        out_specs=[pl.BlockSpec((rows, ATTN_WIDTH), lambda i: (i, 0)), cache_spec, cache_spec],
        out_shape=[
            jax.ShapeDtypeStruct((nseq_total * t_new, ATTN_WIDTH), _BF),
            jax.ShapeDtypeStruct(cache_k.shape, _F32),
            jax.ShapeDtypeStruct(cache_v.shape, _F32),
        ],
        scratch_shapes=[pltpu.VMEM((rows, ATTN_WIDTH), _F32),
                        pltpu.VMEM((N_KV_HEADS, nseq * GROUP * t_new, WINDOW + t_new), _F32)],
        compiler_params=pltpu.CompilerParams(dimension_semantics=("arbitrary",)),
        name="attn_sample",
    )(sinks, q, kv, cache_k, cache_v, g)


def _outproj_kernel(x_ref, na_ref, nb_ref, wa_ref, wb_ref, o_ref):
    o_ref[...] = x_ref[...] + _dot(na_ref[...], wa_ref[...]) + _dot(nb_ref[...], wb_ref[...])


def _outproj(x, na, nb, w_out):
    n = x.shape[0]
    tm = ROW_TILE
    return pl.pallas_call(
        _outproj_kernel,
        grid=(n // tm,),
        in_specs=[
            pl.BlockSpec((tm, D_MODEL), lambda i: (i, 0)),
            pl.BlockSpec((tm, ATTN_WIDTH), lambda i: (i, 0)),
            pl.BlockSpec((tm, CONV_CH), lambda i: (i, 0)),
            pl.BlockSpec((ATTN_WIDTH, D_MODEL), lambda i: (0, 0)),
            pl.BlockSpec((CONV_CH, D_MODEL), lambda i: (1, 0)),
        ],
        out_specs=pl.BlockSpec((tm, D_MODEL), lambda i: (i, 0)),
        out_shape=jax.ShapeDtypeStruct((n, D_MODEL), _F32),
        compiler_params=pltpu.CompilerParams(
            dimension_semantics=("arbitrary",), vmem_limit_bytes=VMEM_LIMIT),
        name="outproj",
    )(x, na, nb, w_out, w_out)


def _ffn_kernel(*refs, seq_tiles):
    carry_mode = seq_tiles is not None
    if carry_mode:
        (x_ref, g_ref, wg_ref, wu_ref, cw_ref, cb_ref, wd_ref, gf_ref,
         o_ref, st_ref, h_scr, acc_scr, ext_scr, carry_scr) = refs
    else:
        (x_ref, g_ref, wg_ref, wu_ref, cw_ref, cb_ref, wd_ref, gf_ref, state_ref,
         o_ref, st_ref, h_scr, acc_scr) = refs
    i = pl.program_id(0)
    j = pl.program_id(1)

    @pl.when(j == 0)
    def _():
        x = x_ref[...]
        acc_scr[...] = x
        h_scr[...] = _rms(x, g_ref[...]).astype(_BF)

    h = h_scr[...]
    gate = _dot(h, wg_ref[...])
    if carry_mode:
        conv = _conv3_carry(gate, cw_ref, ext_scr, carry_scr.at[j], i % seq_tiles == 0)
        st_ref[0] = gate[gate.shape[0] - 2:, :]
    else:
        conv, last2 = _conv3_seqs(gate, cw_ref, state_ref)
        st_ref[...] = last2
    a = conv + cb_ref[...]
    act = ((a / (1.0 + jnp.exp(-a))) * _dot(h, wu_ref[...])).astype(_BF)
    acc_scr[...] += _dot(act, wd_ref[...])

    @pl.when(j == FF_STEPS - 1)
    def _():
        o_ref[...] = _rms(acc_scr[...], gf_ref[...])


def _ffn(x, g, w_gate, w_up, conv_w, conv_b, w_down, g_final, state, *, seq_len):
    n = x.shape[0]
    tm = FF_ROWS
    nt = n // tm
    carry_mode = seq_len >= tm
    seq_tiles = seq_len // tm if carry_mode else None

    def st_map(i, j):
        return (i, 0, j)

    in_specs = [
        pl.BlockSpec((tm, D_MODEL), lambda i, j: (i, 0)),
        pl.BlockSpec((1, D_MODEL), lambda i, j: (0, 0)),
        pl.BlockSpec((D_MODEL, FF_TILE), lambda i, j: (0, j)),
        pl.BlockSpec((D_MODEL, FF_TILE), lambda i, j: (0, j)),
        pl.BlockSpec((3, FF_TILE), lambda i, j: (0, j)),
        pl.BlockSpec((1, FF_TILE), lambda i, j: (0, j)),
        pl.BlockSpec((FF_TILE, D_MODEL), lambda i, j: (j, 0)),
        pl.BlockSpec((1, D_MODEL), lambda i, j: (0, 0)),
    ]
    args = [x, g, w_gate, w_up, conv_w, conv_b, w_down, g_final]
    scratch = [pltpu.VMEM((tm, D_MODEL), _BF), pltpu.VMEM((tm, D_MODEL), _F32)]
    if carry_mode:
        st_shape = jax.ShapeDtypeStruct((nt, 2, D_FF), _F32)
        st_spec = pl.BlockSpec((1, 2, FF_TILE), st_map)
        scratch += [pltpu.VMEM((tm + SUBLANES, FF_TILE), _F32),
                    pltpu.VMEM((FF_STEPS, SUBLANES, FF_TILE), _F32)]
    else:
        nseq = tm // seq_len
        st_shape = jax.ShapeDtypeStruct((n // seq_len, 2, D_FF), _F32)
        st_spec = pl.BlockSpec((nseq, 2, FF_TILE), st_map)
        in_specs.append(st_spec)
        args.append(state)
    return pl.pallas_call(
        functools.partial(_ffn_kernel, seq_tiles=seq_tiles),
        grid=(nt, FF_STEPS),
        in_specs=in_specs,
        out_specs=[pl.BlockSpec((tm, D_MODEL), lambda i, j: (i, 0)), st_spec],
        out_shape=[jax.ShapeDtypeStruct((n, D_MODEL), _F32), st_shape],
        scratch_shapes=scratch,
        compiler_params=pltpu.CompilerParams(
            dimension_semantics=("arbitrary", "arbitrary"), vmem_limit_bytes=VMEM_LIMIT),
        name="ffn_carry" if carry_mode else "ffn_seqs",
    )(*args)


def kernel(x_prompt, x_sample, cache_k_window, cache_v_window, state_conv, state_ffn_conv, g_attn_norm, w_in, attn_sinks, conv_w, g_out_attn, g_out_conv, w_out, g_ffn_norm, w_gate, w_up, ffn_conv_w, ffn_conv_b, w_down, g_final):
    assert w_in.shape[0] == 1, "single layer"
    bp, sp, _ = x_prompt.shape
    bs, ts, _ = x_sample.shape
    assert sp % IN_ROWS == 0 and (bs * ts) % IN_ROWS == 0 and ts == SUBLANES

    w_in_b = w_in[0].astype(_BF)
    w_out_b = w_out[0].astype(_BF)
    w_gate_b = w_gate[0].astype(_BF)
    w_up_b = w_up[0].astype(_BF)
    w_down_b = w_down[0].astype(_BF)
    g_attn = g_attn_norm[0].reshape(1, D_MODEL)
    g_ffn = g_ffn_norm[0].reshape(1, D_MODEL)
    g_fin = g_final.reshape(1, D_MODEL)
    g_oa = g_out_attn[0].reshape(1, ATTN_WIDTH)
    g_oc = g_out_conv[0].reshape(1, CONV_CH)
    sinks = attn_sinks[0].reshape(1, N_HEADS)
    cw = conv_w[0]
    fcw = ffn_conv_w[0]
    fcb = ffn_conv_b[0].reshape(1, D_FF)

    xp = x_prompt.reshape(bp * sp, D_MODEL)
    q_p, kv_p, nb_p, cst_p = _inproj(xp, g_attn, w_in_b, cw, g_oc, None, seq_len=sp, q_dtype=_BF)
    na_p = _attn_prompt(q_p, kv_p, sinks, g_oa, batch=bp, seq=sp)
    x1_p = _outproj(xp, na_p, nb_p, w_out_b)
    y_p, fst_p = _ffn(x1_p, g_ffn, w_gate_b, w_up_b, fcw, fcb, w_down_b, g_fin, None, seq_len=sp)

    seq_tiles = sp // min(IN_ROWS, sp)
    ff_seq_tiles = sp // FF_ROWS
    kv_p3 = kv_p.reshape(bp, sp, 2 * KV_WIDTH)[:, sp - WINDOW:, :]
    k_win_p = kv_p3[:, :, :KV_WIDTH].reshape(1, bp, WINDOW, N_KV_HEADS, HEAD_DIM)
    v_win_p = kv_p3[:, :, KV_WIDTH:].reshape(1, bp, WINDOW, N_KV_HEADS, HEAD_DIM)
    conv_p = cst_p[seq_tiles - 1::seq_tiles][None]
    ffn_conv_p = fst_p[ff_seq_tiles - 1::ff_seq_tiles][None]

    xs = x_sample.reshape(bs * ts, D_MODEL)
    q_s, kv_s, nb_s, cst_s = _inproj(xs, g_attn, w_in_b, cw, g_oc, state_conv[0], seq_len=ts, q_dtype=_F32)
    ck = cache_k_window[0].reshape(bs, WINDOW, KV_WIDTH)
    cv = cache_v_window[0].reshape(bs, WINDOW, KV_WIDTH)
    na_s, k_win_s, v_win_s = _attn_sample(q_s, kv_s, ck, cv, sinks, g_oa, nseq_total=bs, t_new=ts)
    x1_s = _outproj(xs, na_s, nb_s, w_out_b)
    y_s, fst_s = _ffn(x1_s, g_ffn, w_gate_b, w_up_b, fcw, fcb, w_down_b, g_fin, state_ffn_conv[0], seq_len=ts)

    win_shape = (1, bs, WINDOW, N_KV_HEADS, HEAD_DIM)
    return (y_p.reshape(bp, sp, D_MODEL), y_s.reshape(bs, ts, D_MODEL),
            k_win_p, v_win_p, conv_p, ffn_conv_p,
            k_win_s.reshape(win_shape), v_win_s.reshape(win_shape), cst_s[None], fst_s[None])
```

```python
import functools

import jax
import jax.numpy as jnp
from jax import lax
from jax.experimental import pallas as pl
from jax.experimental.pallas import tpu as pltpu

D_MODEL = 2048
ATTN_WIDTH = 1024
CONV_CH = 1024
HEAD_DIM = 64
N_HEADS = 16
N_KV_HEADS = 4
GROUP = 4
KV_WIDTH = 256
WINDOW = 128
BLOCK = 128
D_FF = 5632
EPS = 1e-6
NEG = -1e30

SUBLANES = 8
ROW_TILE = 512
IN_ROWS = 512
COL_TILE = 512
assert ATTN_WIDTH == 2 * COL_TILE and 2 * KV_WIDTH == COL_TILE
CONV_STEP0 = 1
CONV_STEPS = CONV_CH // COL_TILE
IN_STEPS = CONV_STEP0 + CONV_STEPS
FF_ROWS = 512
FF_TILE = 512
FF_STEPS = D_FF // FF_TILE
B_BLK0 = (ATTN_WIDTH + 2 * KV_WIDTH) // COL_TILE
C_BLK0 = B_BLK0 + CONV_STEPS
U_BLK0 = C_BLK0 + CONV_STEPS
VMEM_LIMIT = 56 * 1024 * 1024

_SLOPES = [2.0 ** (-8.0 * (h + 1) / N_HEADS) for h in range(N_HEADS)]
_BF = jnp.bfloat16
_F32 = jnp.float32


def _rms(x, g):
    return x * lax.rsqrt(jnp.mean(x * x, axis=-1, keepdims=True) + EPS) * g


def _dot(a, b):
    return jnp.dot(a, b, preferred_element_type=_F32)


def _dot_nt(a, b):
    return lax.dot_general(a, b, (((1,), (1,)), ((), ())), preferred_element_type=_F32)


def _conv3_carry(cur, w_ref, ext_ref, carry_ref, is_seq_start):
    rows = cur.shape[0]
    prev = jnp.where(is_seq_start, 0.0, carry_ref[...])
    ext_ref[0:SUBLANES, :] = prev
    ext_ref[SUBLANES:SUBLANES + rows, :] = cur
    carry_ref[...] = cur[rows - SUBLANES:, :]
    p1 = ext_ref[SUBLANES - 1:SUBLANES - 1 + rows, :]
    p2 = ext_ref[SUBLANES - 2:SUBLANES - 2 + rows, :]
    return w_ref[2:3, :] * cur + w_ref[1:2, :] * p1 + w_ref[0:1, :] * p2


def _conv3_seqs(cur, w_ref, state_ref):
    rows, ch = cur.shape
    nseq = rows // SUBLANES
    cur3 = cur.reshape(nseq, SUBLANES, ch)
    st = state_ref[...]
    t = lax.broadcasted_iota(jnp.int32, cur3.shape, 1)
    s0 = jnp.broadcast_to(st[:, 0:1, :], cur3.shape)
    s1 = jnp.broadcast_to(st[:, 1:2, :], cur3.shape)
    p1 = jnp.where(t == 0, s1, pltpu.roll(cur3, 1, 1))
    p2 = jnp.where(t == 0, s0, jnp.where(t == 1, s1, pltpu.roll(cur3, 2, 1)))
    w = w_ref[...]
    out = w[2:3, :][None] * cur3 + w[1:2, :][None] * p1 + w[0:1, :][None] * p2
    return out.reshape(rows, ch), cur3[:, SUBLANES - 2:, :]


def _inproj_kernel(*refs, seq_tiles, q_dtype, n_cast):
    carry_mode = seq_tiles is not None
    n_in = 7 if carry_mode else 8
    ins, rest = refs[:n_in], refs[n_in:]
    cast_in, outs, cast_out, scr = (rest[:n_cast], rest[n_cast:n_cast + 4],
                                    rest[n_cast + 4:2 * n_cast + 4], rest[2 * n_cast + 4:])
    q_ref, kv_ref, nb_ref, st_ref = outs
    if carry_mode:
        x_ref, g_ref, wa_ref, wb_ref, wc_ref, cw_ref, gc_ref = ins
        h_scr, y_scr, ss_scr, ext_scr, carry_scr = scr
    else:
        x_ref, g_ref, wa_ref, wb_ref, wc_ref, cw_ref, gc_ref, state_ref = ins
        h_scr, y_scr, ss_scr = scr
    i = pl.program_id(0)
    j = pl.program_id(1)

    @pl.when(j == 0)
    def _():
        _cast_rows(cast_in + cast_out)
        h = _rms(x_ref[...], g_ref[...]).astype(_BF)
        h_scr[...] = h
        q_ref[:, :COL_TILE] = (_dot(h, wa_ref[...]) * (HEAD_DIM ** -0.5)).astype(q_dtype)
        q_ref[:, COL_TILE:] = (_dot(h, wb_ref[...]) * (HEAD_DIM ** -0.5)).astype(q_dtype)
        kv_ref[...] = _dot(h, wc_ref[...])

    for c in range(CONV_STEPS):
        @pl.when(j == CONV_STEP0 + c)
        def _(c=c):
            h = h_scr[...]
            cu = _dot(h, wb_ref[...]) * _dot(h, wc_ref[...])
            if carry_mode:
                conv = _conv3_carry(cu, cw_ref, ext_scr, carry_scr.at[c], i % seq_tiles == 0)
                st_ref[0] = cu[cu.shape[0] - 2:, :]
            else:
                conv, last2 = _conv3_seqs(cu, cw_ref, state_ref)
                st_ref[...] = last2
            y = _dot(h, wa_ref[...]) * conv
            y_scr[c] = y
            ss = jnp.sum(y * y, axis=-1, keepdims=True)
            if c == 0:
                ss_scr[...] = ss
            else:
                ss_scr[...] += ss
            if c == CONV_STEPS - 1:
                inv = lax.rsqrt(ss_scr[...] * (1.0 / CONV_CH) + EPS)
                for cc in range(CONV_STEPS):
                    sl = slice(cc * COL_TILE, (cc + 1) * COL_TILE)
                    nb_ref[:, sl] = (y_scr[cc] * inv * gc_ref[:, sl]).astype(_BF)


def _inproj(x, g, w_in, conv_w, g_conv, state, cast_weights, *, seq_len, q_dtype):
    n = x.shape[0]
    tm = IN_ROWS
    nt = n // tm
    carry_mode = seq_len >= tm
    seq_tiles = seq_len // tm if carry_mode else None

    def conv_col(j):
        return jnp.maximum(j - CONV_STEP0, 0)

    def w_map(first, slab0):
        return lambda i, j: (0, jnp.where(j == 0, first, slab0 + conv_col(j)))

    def st_map(i, j):
        return (i, 0, conv_col(j))

    in_specs = [
        pl.BlockSpec((tm, D_MODEL), lambda i, j: (jnp.minimum(i + jnp.minimum(j, 1), nt - 1), 0)),
        pl.BlockSpec((1, D_MODEL), lambda i, j: (0, 0)),
        pl.BlockSpec((D_MODEL, COL_TILE), w_map(0, B_BLK0)),
        pl.BlockSpec((D_MODEL, COL_TILE), w_map(1, C_BLK0)),
        pl.BlockSpec((D_MODEL, COL_TILE), w_map(2, U_BLK0)),
        pl.BlockSpec((3, COL_TILE), lambda i, j: (0, conv_col(j))),
        pl.BlockSpec((1, CONV_CH), lambda i, j: (0, 0)),
    ]
    args = [x, g, w_in, w_in, w_in, conv_w, g_conv]
    scratch = [
        pltpu.VMEM((tm, D_MODEL), _BF),
        pltpu.VMEM((CONV_STEPS, tm, COL_TILE), _F32),
        pltpu.VMEM((tm, 1), _F32),
    ]
    if carry_mode:
        st_shape = jax.ShapeDtypeStruct((nt, 2, CONV_CH), _F32)
        st_spec = pl.BlockSpec((1, 2, COL_TILE), st_map)
        scratch += [pltpu.VMEM((tm + SUBLANES, COL_TILE), _F32),
                    pltpu.VMEM((CONV_STEPS, SUBLANES, COL_TILE), _F32)]
    else:
        nseq = tm // seq_len
        st_shape = jax.ShapeDtypeStruct((n // seq_len, 2, CONV_CH), _F32)
        st_spec = pl.BlockSpec((nseq, 2, COL_TILE), st_map)
        in_specs.append(st_spec)
        args.append(state)
    out_shape = [
        jax.ShapeDtypeStruct((n, ATTN_WIDTH), q_dtype),
        jax.ShapeDtypeStruct((n, 2 * KV_WIDTH), _F32),
        jax.ShapeDtypeStruct((n, CONV_CH), _BF),
        st_shape,
    ]
    out_specs = [
        pl.BlockSpec((tm, ATTN_WIDTH), lambda i, j: (i, 0)),
        pl.BlockSpec((tm, 2 * KV_WIDTH), lambda i, j: (i, 0)),
        pl.BlockSpec((tm, CONV_CH), lambda i, j: (i, 0)),
        st_spec,
    ]
    cast_in, cast_out, cast_shapes = _cast_specs(
        cast_weights, nt, lambda i, j: jnp.minimum(i + jnp.clip(j - 1, 0, 1), nt - 1), lambda i, j: i)
    return pl.pallas_call(
        functools.partial(_inproj_kernel, seq_tiles=seq_tiles, q_dtype=q_dtype, n_cast=len(cast_weights)),
        grid=(nt, IN_STEPS),
        in_specs=in_specs + cast_in,
        out_specs=out_specs + cast_out,
        out_shape=out_shape + cast_shapes,
        scratch_shapes=scratch,
        compiler_params=pltpu.CompilerParams(
            dimension_semantics=("arbitrary", "arbitrary"), vmem_limit_bytes=VMEM_LIMIT),
        name="inproj_carry" if carry_mode else "inproj_seqs",
    )(*args, *cast_weights)


def _softmax_sink(s, sink):
    m = jnp.maximum(jnp.max(s, axis=-1, keepdims=True), sink)
    p = jnp.exp(s - m)
    return p, jnp.sum(p, axis=-1, keepdims=True) + jnp.exp(sink - m)


def _half_lane_operands(slab, low_half):
    lo = lax.broadcasted_iota(jnp.int32, slab.shape, 1) < HEAD_DIM
    moved = pltpu.roll(slab, HEAD_DIM, 1)
    in_low = jnp.where(lo, slab if low_half else moved, 0.0).astype(_BF)
    in_high = jnp.where(lo, 0.0, moved if low_half else slab).astype(_BF)
    return in_low, in_high


def _attn_prompt_kernel(sinks_ref, q_ref, kvc_ref, kvp_ref, g_ref, o_ref, acc_ref, cast_refs):
    i = pl.program_id(1)
    pair_w = 2 * HEAD_DIM
    qi = lax.broadcasted_iota(jnp.int32, (BLOCK, 2 * BLOCK), 0)
    kj = lax.broadcasted_iota(jnp.int32, (BLOCK, 2 * BLOCK), 1)
    dist_i = qi - kj + BLOCK
    valid = (dist_i >= 0) & (dist_i <= WINDOW) & ((kj >= BLOCK) | (i > 0))
    dist = jnp.where(valid, dist_i.astype(_F32), -NEG / min(_SLOPES))
    low = lax.broadcasted_iota(jnp.int32, (BLOCK, pair_w), 1) < HEAD_DIM

    k_ops, v_ops = [], []
    for m in range(N_KV_HEADS // 2):
        ksl = slice(m * pair_w, (m + 1) * pair_w)
        vsl = slice(KV_WIDTH + m * pair_w, KV_WIDTH + (m + 1) * pair_w)
        kslab = jnp.concatenate([kvp_ref[:, ksl], kvc_ref[:, ksl]], axis=0)
        vslab = jnp.concatenate([kvp_ref[:, vsl], kvc_ref[:, vsl]], axis=0)
        for low_half in (True, False):
            k_ops.append(_half_lane_operands(kslab, low_half))
            v_ops.append(_half_lane_operands(vslab, low_half))

    def scores(c):
        qc = q_ref[:, c * pair_w:(c + 1) * pair_w]
        k_low, k_high = k_ops[c // 2]
        return _dot_nt(qc, k_low), _dot_nt(qc, k_high)

    n_pairs = N_HEADS // 2
    s_next = scores(0)
    for c in range(n_pairs):
        s_pair = s_next
        if c + 1 < n_pairs:
            s_next = scores(c + 1)
        v_low, v_high = v_ops[c // 2]
        p0, d0 = _softmax_sink(s_pair[0] - _SLOPES[2 * c] * dist, sinks_ref[0, 2 * c])
        p1, d1 = _softmax_sink(s_pair[1] - _SLOPES[2 * c + 1] * dist, sinks_ref[0, 2 * c + 1])
        o = _dot(p0.astype(_BF), v_low) + _dot(p1.astype(_BF), v_high)
        acc_ref[:, c * pair_w:(c + 1) * pair_w] = o / jnp.where(low, d0, d1)
    o_ref[...] = _rms(acc_ref[...], g_ref[...]).astype(_BF)
    _cast_rows(cast_refs)


def _cast_rows(cast_refs):
    n = len(cast_refs) // 2
    for src, dst in zip(cast_refs[:n], cast_refs[n:]):
        dst[...] = src[...].astype(_BF)


def _cast_specs(weights, steps, in_index, out_index=None):
    out_index = out_index or in_index
    in_specs, out_specs, shapes = [], [], []
    for w in weights:
        rows = w.shape[0] // steps
        assert rows * steps == w.shape[0] and rows % 16 == 0
        in_specs.append(pl.BlockSpec((rows, w.shape[1]), lambda *ids: (in_index(*ids), 0)))
        out_specs.append(pl.BlockSpec((rows, w.shape[1]), lambda *ids: (out_index(*ids), 0)))
        shapes.append(jax.ShapeDtypeStruct(w.shape, _BF))
    return in_specs, out_specs, shapes


def _attn_prompt_body(sinks_ref, q_ref, kvc_ref, kvp_ref, g_ref, *rest, n_cast):
    cast_in, (o_ref,), cast_out, (acc_ref,) = (rest[:n_cast], rest[n_cast:n_cast + 1],
                                               rest[n_cast + 1:2 * n_cast + 1], rest[2 * n_cast + 1:])
    _attn_prompt_kernel(sinks_ref, q_ref, kvc_ref, kvp_ref, g_ref, o_ref, acc_ref, cast_in + cast_out)


def _attn_prompt(q, kv, sinks, g, cast_weights, *, batch, seq):
    nb = seq // BLOCK
    cast_in, cast_out, cast_shapes = _cast_specs(cast_weights, batch * nb, lambda b, i: b * nb + i)
    return pl.pallas_call(
        functools.partial(_attn_prompt_body, n_cast=len(cast_weights)),
        grid=(batch, nb),
        in_specs=[
            pl.BlockSpec(memory_space=pltpu.SMEM),
            pl.BlockSpec((BLOCK, ATTN_WIDTH), lambda b, i: (b * nb + i, 0)),
            pl.BlockSpec((BLOCK, 2 * KV_WIDTH), lambda b, i: (b * nb + i, 0)),
            pl.BlockSpec((BLOCK, 2 * KV_WIDTH), lambda b, i: (b * nb + jnp.maximum(i - 1, 0), 0)),
            pl.BlockSpec((1, ATTN_WIDTH), lambda b, i: (0, 0)),
        ] + cast_in,
        out_specs=[pl.BlockSpec((BLOCK, ATTN_WIDTH), lambda b, i: (b * nb + i, 0))] + cast_out,
        out_shape=[jax.ShapeDtypeStruct((batch * seq, ATTN_WIDTH), _BF)] + cast_shapes,
        scratch_shapes=[pltpu.VMEM((BLOCK, ATTN_WIDTH), _F32)],
        compiler_params=pltpu.CompilerParams(dimension_semantics=("arbitrary", "arbitrary")),
        name="attn_prompt",
    )(sinks, q, kv, kv, g, *cast_weights)


def _round_bf16(x):
    return x.astype(_BF).astype(_F32)


def _attn_sample_kernel(sinks_ref, q_ref, kvn_ref, ck_ref, cv_ref, g_ref,
                        o_ref, kwin_ref, vwin_ref, acc_ref, s_scr, *, nseq, t_new):
    rows = GROUP * t_new
    nkeys = WINDOW + t_new
    ksl = [slice(kvh * HEAD_DIM, (kvh + 1) * HEAD_DIM) for kvh in range(N_KV_HEADS)]
    vsl = [slice(KV_WIDTH + kvh * HEAD_DIM, KV_WIDTH + (kvh + 1) * HEAD_DIM) for kvh in range(N_KV_HEADS)]

    for kvh in range(N_KV_HEADS):
        for b in range(nseq):
            rs = slice(b * t_new, (b + 1) * t_new)
            k = _round_bf16(jnp.concatenate([ck_ref[b, :, ksl[kvh]], kvn_ref[rs, ksl[kvh]]], axis=0))
            qs = jnp.concatenate(
                [q_ref[rs, (kvh * GROUP + gq) * HEAD_DIM:(kvh * GROUP + gq + 1) * HEAD_DIM]
                 for gq in range(GROUP)], axis=0)
            s_scr[kvh, b * rows:(b + 1) * rows, :] = _dot_nt(_round_bf16(qs), k)

    r = lax.broadcasted_iota(jnp.int32, (nseq * rows, 1), 0)
    gq_of_row = (r // t_new) % GROUP
    t = lax.broadcasted_iota(jnp.int32, (nseq * rows, nkeys), 0) % t_new
    kj = lax.broadcasted_iota(jnp.int32, (nseq * rows, nkeys), 1)
    dist_i = t + WINDOW - kj
    valid = (dist_i >= 0) & (dist_i <= WINDOW)
    dist = jnp.where(valid, dist_i.astype(_F32), -NEG / min(_SLOPES))
    for kvh in range(N_KV_HEADS):
        slope = jnp.full((nseq * rows, 1), _SLOPES[kvh * GROUP], _F32)
        sink = jnp.full((nseq * rows, 1), sinks_ref[0, kvh * GROUP], _F32)
        for gq in range(1, GROUP):
            slope = jnp.where(gq_of_row == gq, _SLOPES[kvh * GROUP + gq], slope)
            sink = jnp.where(gq_of_row == gq, sinks_ref[0, kvh * GROUP + gq], sink)
        p, denom = _softmax_sink(s_scr[kvh] - slope * dist, sink)
        s_scr[kvh] = _round_bf16(p / denom)

    for kvh in range(N_KV_HEADS):
        for b in range(nseq):
            rs = slice(b * t_new, (b + 1) * t_new)
            v = _round_bf16(jnp.concatenate([cv_ref[b, :, ksl[kvh]], kvn_ref[rs, vsl[kvh]]], axis=0))
            o = _dot(s_scr[kvh, b * rows:(b + 1) * rows, :], v)
            for gq in range(GROUP):
                h = kvh * GROUP + gq
                acc_ref[rs, h * HEAD_DIM:(h + 1) * HEAD_DIM] = o[gq * t_new:(gq + 1) * t_new, :]
    o_ref[...] = _rms(acc_ref[...], g_ref[...]).astype(_BF)
    kwin_ref[:, 0:WINDOW - t_new, :] = ck_ref[:, t_new:, :]
    vwin_ref[:, 0:WINDOW - t_new, :] = cv_ref[:, t_new:, :]
    kwin_ref[:, WINDOW - t_new:, :] = kvn_ref[:, 0:KV_WIDTH].reshape(nseq, t_new, KV_WIDTH)
    vwin_ref[:, WINDOW - t_new:, :] = kvn_ref[:, KV_WIDTH:].reshape(nseq, t_new, KV_WIDTH)


def _attn_sample(q, kv, cache_k, cache_v, sinks, g, *, nseq_total, t_new, nseq=8):
    rows = nseq * t_new
    cache_spec = pl.BlockSpec((nseq, WINDOW, KV_WIDTH), lambda i: (i, 0, 0))
    return pl.pallas_call(
        functools.partial(_attn_sample_kernel, nseq=nseq, t_new=t_new),
        grid=(nseq_total // nseq,),
        in_specs=[
            pl.BlockSpec(memory_space=pltpu.SMEM),
            pl.BlockSpec((rows, ATTN_WIDTH), lambda i: (i, 0)),
            pl.BlockSpec((rows, 2 * KV_WIDTH), lambda i: (i, 0)),
            cache_spec,
            cache_spec,
            pl.BlockSpec((1, ATTN_WIDTH), lambda i: (0, 0)),
        ],
        out_specs=[pl.BlockSpec((rows, ATTN_WIDTH), lambda i: (i, 0)), cache_spec, cache_spec],
        out_shape=[
            jax.ShapeDtypeStruct((nseq_total * t_new, ATTN_WIDTH), _BF),
            jax.ShapeDtypeStruct(cache_k.shape, _F32),
            jax.ShapeDtypeStruct(cache_v.shape, _F32),
        ],
        scratch_shapes=[pltpu.VMEM((rows, ATTN_WIDTH), _F32),
                        pltpu.VMEM((N_KV_HEADS, nseq * GROUP * t_new, WINDOW + t_new), _F32)],
        compiler_params=pltpu.CompilerParams(dimension_semantics=("arbitrary",)),
        name="attn_sample",
    )(sinks, q, kv, cache_k, cache_v, g)


def _outproj_kernel(x_ref, na_ref, nb_ref, wa_ref, wb_ref, o_ref):
    o_ref[...] = x_ref[...] + _dot(na_ref[...], wa_ref[...]) + _dot(nb_ref[...], wb_ref[...])


def _outproj(x, na, nb, w_out):
    n = x.shape[0]
    tm = ROW_TILE
    return pl.pallas_call(
        _outproj_kernel,
        grid=(n // tm,),
        in_specs=[
            pl.BlockSpec((tm, D_MODEL), lambda i: (i, 0)),
            pl.BlockSpec((tm, ATTN_WIDTH), lambda i: (i, 0)),
            pl.BlockSpec((tm, CONV_CH), lambda i: (i, 0)),
            pl.BlockSpec((ATTN_WIDTH, D_MODEL), lambda i: (0, 0)),
            pl.BlockSpec((CONV_CH, D_MODEL), lambda i: (1, 0)),
        ],
        out_specs=pl.BlockSpec((tm, D_MODEL), lambda i: (i, 0)),
        out_shape=jax.ShapeDtypeStruct((n, D_MODEL), _F32),
        compiler_params=pltpu.CompilerParams(
            dimension_semantics=("arbitrary",), vmem_limit_bytes=VMEM_LIMIT),
        name="outproj",
    )(x, na, nb, w_out, w_out)


def _ffn_kernel(*refs, seq_tiles):
    carry_mode = seq_tiles is not None
    if carry_mode:
        (x_ref, g_ref, wg_ref, wu_ref, cw_ref, cb_ref, wd_ref, gf_ref,
         o_ref, st_ref, h_scr, acc_scr, ext_scr, carry_scr) = refs
    else:
        (x_ref, g_ref, wg_ref, wu_ref, cw_ref, cb_ref, wd_ref, gf_ref, state_ref,
         o_ref, st_ref, h_scr, acc_scr) = refs
    i = pl.program_id(0)
    j = pl.program_id(1)

    @pl.when(j == 0)
    def _():
        x = x_ref[...]
        acc_scr[...] = x
        h_scr[...] = _rms(x, g_ref[...]).astype(_BF)

    h = h_scr[...]
    gate = _dot(h, wg_ref[...])
    if carry_mode:
        conv = _conv3_carry(gate, cw_ref, ext_scr, carry_scr.at[j], i % seq_tiles == 0)
        st_ref[0] = gate[gate.shape[0] - 2:, :]
    else:
        conv, last2 = _conv3_seqs(gate, cw_ref, state_ref)
        st_ref[...] = last2
    a = conv + cb_ref[...]
    act = ((a / (1.0 + jnp.exp(-a))) * _dot(h, wu_ref[...])).astype(_BF)
    acc_scr[...] += _dot(act, wd_ref[...])

    @pl.when(j == FF_STEPS - 1)
    def _():
        o_ref[...] = _rms(acc_scr[...], gf_ref[...])


def _ffn(x, g, w_gate, w_up, conv_w, conv_b, w_down, g_final, state, *, seq_len):
    n = x.shape[0]
    tm = FF_ROWS
    nt = n // tm
    carry_mode = seq_len >= tm
    seq_tiles = seq_len // tm if carry_mode else None

    def st_map(i, j):
        return (i, 0, j)

    in_specs = [
        pl.BlockSpec((tm, D_MODEL), lambda i, j: (jnp.minimum(i + jnp.minimum(j, 1), nt - 1), 0)),
        pl.BlockSpec((1, D_MODEL), lambda i, j: (0, 0)),
        pl.BlockSpec((D_MODEL, FF_TILE), lambda i, j: (0, j)),
        pl.BlockSpec((D_MODEL, FF_TILE), lambda i, j: (0, j)),
        pl.BlockSpec((3, FF_TILE), lambda i, j: (0, j)),
        pl.BlockSpec((1, FF_TILE), lambda i, j: (0, j)),
        pl.BlockSpec((FF_TILE, D_MODEL), lambda i, j: (j, 0)),
        pl.BlockSpec((1, D_MODEL), lambda i, j: (0, 0)),
    ]
    args = [x, g, w_gate, w_up, conv_w, conv_b, w_down, g_final]
    scratch = [pltpu.VMEM((tm, D_MODEL), _BF), pltpu.VMEM((tm, D_MODEL), _F32)]
    if carry_mode:
        st_shape = jax.ShapeDtypeStruct((nt, 2, D_FF), _F32)
        st_spec = pl.BlockSpec((1, 2, FF_TILE), st_map)
        scratch += [pltpu.VMEM((tm + SUBLANES, FF_TILE), _F32),
                    pltpu.VMEM((FF_STEPS, SUBLANES, FF_TILE), _F32)]
    else:
        nseq = tm // seq_len
        st_shape = jax.ShapeDtypeStruct((n // seq_len, 2, D_FF), _F32)
        st_spec = pl.BlockSpec((nseq, 2, FF_TILE), st_map)
        in_specs.append(st_spec)
        args.append(state)
    return pl.pallas_call(
        functools.partial(_ffn_kernel, seq_tiles=seq_tiles),
        grid=(nt, FF_STEPS),
        in_specs=in_specs,
        out_specs=[pl.BlockSpec((tm, D_MODEL), lambda i, j: (i, 0)), st_spec],
        out_shape=[jax.ShapeDtypeStruct((n, D_MODEL), _F32), st_shape],
        scratch_shapes=scratch,
        compiler_params=pltpu.CompilerParams(
            dimension_semantics=("arbitrary", "arbitrary"), vmem_limit_bytes=VMEM_LIMIT),
        name="ffn_carry" if carry_mode else "ffn_seqs",
    )(*args)


def kernel(x_prompt, x_sample, cache_k_window, cache_v_window, state_conv, state_ffn_conv, g_attn_norm, w_in, attn_sinks, conv_w, g_out_attn, g_out_conv, w_out, g_ffn_norm, w_gate, w_up, ffn_conv_w, ffn_conv_b, w_down, g_final):
    assert w_in.shape[0] == 1, "single layer"
    bp, sp, _ = x_prompt.shape
    bs, ts, _ = x_sample.shape
    assert sp % IN_ROWS == 0 and (bs * ts) % IN_ROWS == 0 and ts == SUBLANES

    w_in_b = w_in[0].astype(_BF)
    g_attn = g_attn_norm[0].reshape(1, D_MODEL)
    g_ffn = g_ffn_norm[0].reshape(1, D_MODEL)
    g_fin = g_final.reshape(1, D_MODEL)
    g_oa = g_out_attn[0].reshape(1, ATTN_WIDTH)
    g_oc = g_out_conv[0].reshape(1, CONV_CH)
    sinks = attn_sinks[0].reshape(1, N_HEADS)
    cw = conv_w[0]
    fcw = ffn_conv_w[0]
    fcb = ffn_conv_b[0].reshape(1, D_FF)

    xp = x_prompt.reshape(bp * sp, D_MODEL)
    q_p, kv_p, nb_p, cst_p, w_out_b, w_down_b = _inproj(
        xp, g_attn, w_in_b, cw, g_oc, None, [w_out[0], w_down[0]], seq_len=sp, q_dtype=_BF)
    na_p, w_gate_b, w_up_b = _attn_prompt(q_p, kv_p, sinks, g_oa, [w_gate[0], w_up[0]], batch=bp, seq=sp)
    x1_p = _outproj(xp, na_p, nb_p, w_out_b)
    y_p, fst_p = _ffn(x1_p, g_ffn, w_gate_b, w_up_b, fcw, fcb, w_down_b, g_fin, None, seq_len=sp)

    seq_tiles = sp // min(IN_ROWS, sp)
    ff_seq_tiles = sp // FF_ROWS
    kv_p3 = kv_p.reshape(bp, sp, 2 * KV_WIDTH)[:, sp - WINDOW:, :]
    k_win_p = kv_p3[:, :, :KV_WIDTH].reshape(1, bp, WINDOW, N_KV_HEADS, HEAD_DIM)
    v_win_p = kv_p3[:, :, KV_WIDTH:].reshape(1, bp, WINDOW, N_KV_HEADS, HEAD_DIM)
    conv_p = cst_p[seq_tiles - 1::seq_tiles][None]
    ffn_conv_p = fst_p[ff_seq_tiles - 1::ff_seq_tiles][None]

    xs = x_sample.reshape(bs * ts, D_MODEL)
    q_s, kv_s, nb_s, cst_s = _inproj(xs, g_attn, w_in_b, cw, g_oc, state_conv[0], [], seq_len=ts, q_dtype=_F32)
    ck = cache_k_window[0].reshape(bs, WINDOW, KV_WIDTH)
    cv = cache_v_window[0].reshape(bs, WINDOW, KV_WIDTH)
    na_s, k_win_s, v_win_s = _attn_sample(q_s, kv_s, ck, cv, sinks, g_oa, nseq_total=bs, t_new=ts)
    x1_s = _outproj(xs, na_s, nb_s, w_out_b)
    y_s, fst_s = _ffn(x1_s, g_ffn, w_gate_b, w_up_b, fcw, fcb, w_down_b, g_fin, state_ffn_conv[0], seq_len=ts)

    win_shape = (1, bs, WINDOW, N_KV_HEADS, HEAD_DIM)
    return (y_p.reshape(bp, sp, D_MODEL), y_s.reshape(bs, ts, D_MODEL),
            k_win_p, v_win_p, conv_p, ffn_conv_p,
            k_win_s.reshape(win_shape), v_win_s.reshape(win_shape), cst_s[None], fst_s[None])
```

```python
import functools

import jax
import jax.numpy as jnp
from jax import lax
from jax.experimental import pallas as pl
from jax.experimental.pallas import tpu as pltpu

D_MODEL = 2048
ATTN_WIDTH = 1024
CONV_CH = 1024
HEAD_DIM = 64
N_HEADS = 16
N_KV_HEADS = 4
GROUP = 4
KV_WIDTH = 256
WINDOW = 128
BLOCK = 128
D_FF = 5632
EPS = 1e-6
NEG = -1e30

SUBLANES = 8
IN_ROWS = 512
COL_TILE = 512
assert ATTN_WIDTH == 2 * COL_TILE and 2 * KV_WIDTH == COL_TILE
CONV_STEP0 = 1
CONV_STEPS = CONV_CH // COL_TILE
IN_STEPS = CONV_STEP0 + CONV_STEPS
FF_ROWS = 512
FF_TILE = 512
FF_STEPS = D_FF // FF_TILE
B_BLK0 = (ATTN_WIDTH + 2 * KV_WIDTH) // COL_TILE
C_BLK0 = B_BLK0 + CONV_STEPS
U_BLK0 = C_BLK0 + CONV_STEPS
VMEM_LIMIT = 56 * 1024 * 1024

_SLOPES = [2.0 ** (-8.0 * (h + 1) / N_HEADS) for h in range(N_HEADS)]
_BF = jnp.bfloat16
_F32 = jnp.float32


def _rms(x, g):
    return x * lax.rsqrt(jnp.mean(x * x, axis=-1, keepdims=True) + EPS) * g


def _dot(a, b):
    return jnp.dot(a, b, preferred_element_type=_F32)


def _dot_nt(a, b):
    return lax.dot_general(a, b, (((1,), (1,)), ((), ())), preferred_element_type=_F32)


def _conv3_carry(cur, w_ref, ext_ref, carry_ref, is_seq_start):
    rows = cur.shape[0]
    prev = jnp.where(is_seq_start, 0.0, carry_ref[...])
    ext_ref[0:SUBLANES, :] = prev
    ext_ref[SUBLANES:SUBLANES + rows, :] = cur
    carry_ref[...] = cur[rows - SUBLANES:, :]
    p1 = ext_ref[SUBLANES - 1:SUBLANES - 1 + rows, :]
    p2 = ext_ref[SUBLANES - 2:SUBLANES - 2 + rows, :]
    return w_ref[2:3, :] * cur + w_ref[1:2, :] * p1 + w_ref[0:1, :] * p2


def _conv3_seqs(cur, w_ref, state_ref):
    rows, ch = cur.shape
    nseq = rows // SUBLANES
    cur3 = cur.reshape(nseq, SUBLANES, ch)
    st = state_ref[...]
    t = lax.broadcasted_iota(jnp.int32, cur3.shape, 1)
    s0 = jnp.broadcast_to(st[:, 0:1, :], cur3.shape)
    s1 = jnp.broadcast_to(st[:, 1:2, :], cur3.shape)
    p1 = jnp.where(t == 0, s1, pltpu.roll(cur3, 1, 1))
    p2 = jnp.where(t == 0, s0, jnp.where(t == 1, s1, pltpu.roll(cur3, 2, 1)))
    w = w_ref[...]
    out = w[2:3, :][None] * cur3 + w[1:2, :][None] * p1 + w[0:1, :][None] * p2
    return out.reshape(rows, ch), cur3[:, SUBLANES - 2:, :]


def _inproj_kernel(*refs, seq_tiles, q_dtype, n_cast):
    carry_mode = seq_tiles is not None
    n_in = 7 if carry_mode else 8
    ins, rest = refs[:n_in], refs[n_in:]
    cast_in, outs, cast_out, scr = (rest[:n_cast], rest[n_cast:n_cast + 4],
                                    rest[n_cast + 4:2 * n_cast + 4], rest[2 * n_cast + 4:])
    q_ref, kv_ref, nb_ref, st_ref = outs
    if carry_mode:
        x_ref, g_ref, wa_ref, wb_ref, wc_ref, cw_ref, gc_ref = ins
        h_scr, y_scr, ss_scr, ext_scr, carry_scr = scr
    else:
        x_ref, g_ref, wa_ref, wb_ref, wc_ref, cw_ref, gc_ref, state_ref = ins
        h_scr, y_scr, ss_scr = scr
    i = pl.program_id(0)
    j = pl.program_id(1)

    @pl.when(j == 0)
    def _():
        _cast_rows(cast_in + cast_out)
        h = _rms(x_ref[...], g_ref[...]).astype(_BF)
        h_scr[...] = h
        q_ref[:, :COL_TILE] = (_dot(h, wa_ref[...]) * (HEAD_DIM ** -0.5)).astype(q_dtype)
        q_ref[:, COL_TILE:] = (_dot(h, wb_ref[...]) * (HEAD_DIM ** -0.5)).astype(q_dtype)
        kv_ref[...] = _dot(h, wc_ref[...])

    for c in range(CONV_STEPS):
        @pl.when(j == CONV_STEP0 + c)
        def _(c=c):
            h = h_scr[...]
            cu = _dot(h, wb_ref[...]) * _dot(h, wc_ref[...])
            if carry_mode:
                conv = _conv3_carry(cu, cw_ref, ext_scr, carry_scr.at[c], i % seq_tiles == 0)
                st_ref[0] = cu[cu.shape[0] - 2:, :]
            else:
                conv, last2 = _conv3_seqs(cu, cw_ref, state_ref)
                st_ref[...] = last2
            y = _dot(h, wa_ref[...]) * conv
            y_scr[c] = y
            ss = jnp.sum(y * y, axis=-1, keepdims=True)
            if c == 0:
                ss_scr[...] = ss
            else:
                ss_scr[...] += ss
            if c == CONV_STEPS - 1:
                inv = lax.rsqrt(ss_scr[...] * (1.0 / CONV_CH) + EPS)
                for cc in range(CONV_STEPS):
                    sl = slice(cc * COL_TILE, (cc + 1) * COL_TILE)
                    nb_ref[:, sl] = (y_scr[cc] * inv * gc_ref[:, sl]).astype(_BF)


def _inproj(x, g, w_in, conv_w, g_conv, state, cast_weights, *, seq_len, q_dtype):
    n = x.shape[0]
    tm = IN_ROWS
    nt = n // tm
    carry_mode = seq_len >= tm
    seq_tiles = seq_len // tm if carry_mode else None

    def conv_col(j):
        return jnp.maximum(j - CONV_STEP0, 0)

    def w_map(first, slab0):
        return lambda i, j: (jnp.where(j == 0, first, slab0 + conv_col(j)), 0, 0)

    def st_map(i, j):
        return (i, 0, conv_col(j))

    in_specs = [
        pl.BlockSpec((tm, D_MODEL), lambda i, j: (i, 0)),
        pl.BlockSpec((1, D_MODEL), lambda i, j: (0, 0)),
        pl.BlockSpec((None, D_MODEL, COL_TILE), w_map(0, B_BLK0)),
        pl.BlockSpec((None, D_MODEL, COL_TILE), w_map(1, C_BLK0)),
        pl.BlockSpec((None, D_MODEL, COL_TILE), w_map(2, U_BLK0)),
        pl.BlockSpec((3, COL_TILE), lambda i, j: (0, conv_col(j))),
        pl.BlockSpec((1, CONV_CH), lambda i, j: (0, 0)),
    ]
    args = [x, g, w_in, w_in, w_in, conv_w, g_conv]
    scratch = [
        pltpu.VMEM((tm, D_MODEL), _BF),
        pltpu.VMEM((CONV_STEPS, tm, COL_TILE), _F32),
        pltpu.VMEM((tm, 1), _F32),
    ]
    if carry_mode:
        st_shape = jax.ShapeDtypeStruct((nt, 2, CONV_CH), _F32)
        st_spec = pl.BlockSpec((1, 2, COL_TILE), st_map)
        scratch += [pltpu.VMEM((tm + SUBLANES, COL_TILE), _F32),
                    pltpu.VMEM((CONV_STEPS, SUBLANES, COL_TILE), _F32)]
    else:
        nseq = tm // seq_len
        st_shape = jax.ShapeDtypeStruct((n // seq_len, 2, CONV_CH), _F32)
        st_spec = pl.BlockSpec((nseq, 2, COL_TILE), st_map)
        in_specs.append(st_spec)
        args.append(state)
    out_shape = [
        jax.ShapeDtypeStruct((n, ATTN_WIDTH), q_dtype),
        jax.ShapeDtypeStruct((n, 2 * KV_WIDTH), _F32),
        jax.ShapeDtypeStruct((n, CONV_CH), _BF),
        st_shape,
    ]
    out_specs = [
        pl.BlockSpec((tm, ATTN_WIDTH), lambda i, j: (i, 0)),
        pl.BlockSpec((tm, 2 * KV_WIDTH), lambda i, j: (i, 0)),
        pl.BlockSpec((tm, CONV_CH), lambda i, j: (i, 0)),
        st_spec,
    ]
    cast_in, cast_out, cast_shapes = _cast_specs(cast_weights, nt, lambda i, j: i)
    return pl.pallas_call(
        functools.partial(_inproj_kernel, seq_tiles=seq_tiles, q_dtype=q_dtype, n_cast=len(cast_weights)),
        grid=(nt, IN_STEPS),
        in_specs=in_specs + cast_in,
        out_specs=out_specs + cast_out,
        out_shape=out_shape + cast_shapes,
        scratch_shapes=scratch,
        compiler_params=pltpu.CompilerParams(
            dimension_semantics=("arbitrary", "arbitrary"), vmem_limit_bytes=VMEM_LIMIT),
        name="inproj_carry" if carry_mode else "inproj_seqs",
    )(*args, *[w for w, _ in cast_weights])


def _softmax_sink(s, sink):
    m = jnp.maximum(jnp.max(s, axis=-1, keepdims=True), sink)
    p = jnp.exp(s - m)
    return p, jnp.sum(p, axis=-1, keepdims=True) + jnp.exp(sink - m)


def _half_lane_operands(slab, low_half):
    lo = lax.broadcasted_iota(jnp.int32, slab.shape, 1) < HEAD_DIM
    moved = pltpu.roll(slab, HEAD_DIM, 1)
    in_low = jnp.where(lo, slab if low_half else moved, 0.0).astype(_BF)
    in_high = jnp.where(lo, 0.0, moved if low_half else slab).astype(_BF)
    return in_low, in_high


def _attn_prompt_kernel(sinks_ref, q_ref, kvc_ref, kvp_ref, g_ref, o_ref, acc_ref, cast_refs):
    i = pl.program_id(1)
    pair_w = 2 * HEAD_DIM
    qi = lax.broadcasted_iota(jnp.int32, (BLOCK, 2 * BLOCK), 0)
    kj = lax.broadcasted_iota(jnp.int32, (BLOCK, 2 * BLOCK), 1)
    dist_i = qi - kj + BLOCK
    valid = (dist_i >= 0) & (dist_i <= WINDOW) & ((kj >= BLOCK) | (i > 0))
    dist = jnp.where(valid, dist_i.astype(_F32), -NEG / min(_SLOPES))
    low = lax.broadcasted_iota(jnp.int32, (BLOCK, pair_w), 1) < HEAD_DIM

    k_ops, v_ops = [], []
    for m in range(N_KV_HEADS // 2):
        ksl = slice(m * pair_w, (m + 1) * pair_w)
        vsl = slice(KV_WIDTH + m * pair_w, KV_WIDTH + (m + 1) * pair_w)
        kslab = jnp.concatenate([kvp_ref[:, ksl], kvc_ref[:, ksl]], axis=0)
        vslab = jnp.concatenate([kvp_ref[:, vsl], kvc_ref[:, vsl]], axis=0)
        for low_half in (True, False):
            k_ops.append(_half_lane_operands(kslab, low_half))
            v_ops.append(_half_lane_operands(vslab, low_half))

    def scores(c):
        qc = q_ref[:, c * pair_w:(c + 1) * pair_w]
        k_low, k_high = k_ops[c // 2]
        return _dot_nt(qc, k_low), _dot_nt(qc, k_high)

    n_pairs = N_HEADS // 2
    s_next = scores(0)
    for c in range(n_pairs):
        s_pair = s_next
        if c + 1 < n_pairs:
            s_next = scores(c + 1)
        v_low, v_high = v_ops[c // 2]
        p0, d0 = _softmax_sink(s_pair[0] - _SLOPES[2 * c] * dist, sinks_ref[0, 2 * c])
        p1, d1 = _softmax_sink(s_pair[1] - _SLOPES[2 * c + 1] * dist, sinks_ref[0, 2 * c + 1])
        o = _dot(p0.astype(_BF), v_low) + _dot(p1.astype(_BF), v_high)
        acc_ref[:, c * pair_w:(c + 1) * pair_w] = o / jnp.where(low, d0, d1)
    o_ref[...] = _rms(acc_ref[...], g_ref[...]).astype(_BF)
    _cast_rows(cast_refs)


def _cast_rows(cast_refs):
    n = len(cast_refs) // 2
    for src, dst in zip(cast_refs[:n], cast_refs[n:]):
        if len(dst.shape) == 2:
            dst[...] = src[...].astype(_BF)
        else:
            tw = dst.shape[2]
            for t in range(dst.shape[0]):
                dst[t] = src[:, t * tw:(t + 1) * tw].astype(_BF)


def _cast_specs(weights, steps, index):
    in_specs, out_specs, shapes = [], [], []
    for w, col_tile in weights:
        k, n = w.shape
        rows = k // steps
        assert rows * steps == k and rows % 16 == 0
        in_specs.append(pl.BlockSpec((rows, n), lambda *ids: (index(*ids), 0)))
        if col_tile is None:
            out_specs.append(pl.BlockSpec((rows, n), lambda *ids: (index(*ids), 0)))
            shapes.append(jax.ShapeDtypeStruct((k, n), _BF))
        else:
            out_specs.append(pl.BlockSpec((n // col_tile, rows, col_tile), lambda *ids: (0, index(*ids), 0)))
            shapes.append(jax.ShapeDtypeStruct((n // col_tile, k, col_tile), _BF))
    return in_specs, out_specs, shapes


def _attn_prompt_body(sinks_ref, q_ref, kvc_ref, kvp_ref, g_ref, *rest, n_cast):
    cast_in, (o_ref,), cast_out, (acc_ref,) = (rest[:n_cast], rest[n_cast:n_cast + 1],
                                               rest[n_cast + 1:2 * n_cast + 1], rest[2 * n_cast + 1:])
    _attn_prompt_kernel(sinks_ref, q_ref, kvc_ref, kvp_ref, g_ref, o_ref, acc_ref, cast_in + cast_out)


def _attn_prompt(q, kv, sinks, g, cast_weights, *, batch, seq):
    nb = seq // BLOCK
    cast_in, cast_out, cast_shapes = _cast_specs(cast_weights, batch * nb, lambda b, i: b * nb + i)
    return pl.pallas_call(
        functools.partial(_attn_prompt_body, n_cast=len(cast_weights)),
        grid=(batch, nb),
        in_specs=[
            pl.BlockSpec(memory_space=pltpu.SMEM),
            pl.BlockSpec((BLOCK, ATTN_WIDTH), lambda b, i: (b * nb + i, 0)),
            pl.BlockSpec((BLOCK, 2 * KV_WIDTH), lambda b, i: (b * nb + i, 0)),
            pl.BlockSpec((BLOCK, 2 * KV_WIDTH), lambda b, i: (b * nb + jnp.maximum(i - 1, 0), 0)),
            pl.BlockSpec((1, ATTN_WIDTH), lambda b, i: (0, 0)),
        ] + cast_in,
        out_specs=[pl.BlockSpec((BLOCK, ATTN_WIDTH), lambda b, i: (b * nb + i, 0))] + cast_out,
        out_shape=[jax.ShapeDtypeStruct((batch * seq, ATTN_WIDTH), _BF)] + cast_shapes,
        scratch_shapes=[pltpu.VMEM((BLOCK, ATTN_WIDTH), _F32)],
        compiler_params=pltpu.CompilerParams(dimension_semantics=("arbitrary", "arbitrary")),
        name="attn_prompt",
    )(sinks, q, kv, kv, g, *[w for w, _ in cast_weights])


def _round_bf16(x):
    return x.astype(_BF).astype(_F32)


def _attn_sample_kernel(sinks_ref, q_ref, kvn_ref, ck_ref, cv_ref, g_ref,
                        o_ref, kwin_ref, vwin_ref, acc_ref, s_scr, *, nseq, t_new):
    rows = GROUP * t_new
    nkeys = WINDOW + t_new
    ksl = [slice(kvh * HEAD_DIM, (kvh + 1) * HEAD_DIM) for kvh in range(N_KV_HEADS)]
    vsl = [slice(KV_WIDTH + kvh * HEAD_DIM, KV_WIDTH + (kvh + 1) * HEAD_DIM) for kvh in range(N_KV_HEADS)]

    for kvh in range(N_KV_HEADS):
        for b in range(nseq):
            rs = slice(b * t_new, (b + 1) * t_new)
            k = _round_bf16(jnp.concatenate([ck_ref[b, :, ksl[kvh]], kvn_ref[rs, ksl[kvh]]], axis=0))
            qs = jnp.concatenate(
                [q_ref[rs, (kvh * GROUP + gq) * HEAD_DIM:(kvh * GROUP + gq + 1) * HEAD_DIM]
                 for gq in range(GROUP)], axis=0)
            s_scr[kvh, b * rows:(b + 1) * rows, :] = _dot_nt(_round_bf16(qs), k)

    r = lax.broadcasted_iota(jnp.int32, (nseq * rows, 1), 0)
    gq_of_row = (r // t_new) % GROUP
    t = lax.broadcasted_iota(jnp.int32, (nseq * rows, nkeys), 0) % t_new
    kj = lax.broadcasted_iota(jnp.int32, (nseq * rows, nkeys), 1)
    dist_i = t + WINDOW - kj
    valid = (dist_i >= 0) & (dist_i <= WINDOW)
    dist = jnp.where(valid, dist_i.astype(_F32), -NEG / min(_SLOPES))
    for kvh in range(N_KV_HEADS):
        slope = jnp.full((nseq * rows, 1), _SLOPES[kvh * GROUP], _F32)
        sink = jnp.full((nseq * rows, 1), sinks_ref[0, kvh * GROUP], _F32)
        for gq in range(1, GROUP):
            slope = jnp.where(gq_of_row == gq, _SLOPES[kvh * GROUP + gq], slope)
            sink = jnp.where(gq_of_row == gq, sinks_ref[0, kvh * GROUP + gq], sink)
        p, denom = _softmax_sink(s_scr[kvh] - slope * dist, sink)
        s_scr[kvh] = _round_bf16(p / denom)

    for kvh in range(N_KV_HEADS):
        for b in range(nseq):
            rs = slice(b * t_new, (b + 1) * t_new)
            v = _round_bf16(jnp.concatenate([cv_ref[b, :, ksl[kvh]], kvn_ref[rs, vsl[kvh]]], axis=0))
            o = _dot(s_scr[kvh, b * rows:(b + 1) * rows, :], v)
            for gq in range(GROUP):
                h = kvh * GROUP + gq
                acc_ref[rs, h * HEAD_DIM:(h + 1) * HEAD_DIM] = o[gq * t_new:(gq + 1) * t_new, :]
    o_ref[...] = _rms(acc_ref[...], g_ref[...]).astype(_BF)
    kwin_ref[:, 0:WINDOW - t_new, :] = ck_ref[:, t_new:, :]
    vwin_ref[:, 0:WINDOW - t_new, :] = cv_ref[:, t_new:, :]
    kwin_ref[:, WINDOW - t_new:, :] = kvn_ref[:, 0:KV_WIDTH].reshape(nseq, t_new, KV_WIDTH)
    vwin_ref[:, WINDOW - t_new:, :] = kvn_ref[:, KV_WIDTH:].reshape(nseq, t_new, KV_WIDTH)


def _attn_sample(q, kv, cache_k, cache_v, sinks, g, *, nseq_total, t_new, nseq=8):
    rows = nseq * t_new
    cache_spec = pl.BlockSpec((nseq, WINDOW, KV_WIDTH), lambda i: (i, 0, 0))
    return pl.pallas_call(
        functools.partial(_attn_sample_kernel, nseq=nseq, t_new=t_new),
        grid=(nseq_total // nseq,),
        in_specs=[
            pl.BlockSpec(memory_space=pltpu.SMEM),
            pl.BlockSpec((rows, ATTN_WIDTH), lambda i: (i, 0)),
            pl.BlockSpec((rows, 2 * KV_WIDTH), lambda i: (i, 0)),
            cache_spec,
            cache_spec,
            pl.BlockSpec((1, ATTN_WIDTH), lambda i: (0, 0)),
        ],
        out_specs=[pl.BlockSpec((rows, ATTN_WIDTH), lambda i: (i, 0)), cache_spec, cache_spec],
        out_shape=[
            jax.ShapeDtypeStruct((nseq_total * t_new, ATTN_WIDTH), _BF),
            jax.ShapeDtypeStruct(cache_k.shape, _F32),
            jax.ShapeDtypeStruct(cache_v.shape, _F32),
        ],
        scratch_shapes=[pltpu.VMEM((rows, ATTN_WIDTH), _F32),
                        pltpu.VMEM((N_KV_HEADS, nseq * GROUP * t_new, WINDOW + t_new), _F32)],
        compiler_params=pltpu.CompilerParams(dimension_semantics=("arbitrary",)),
        name="attn_sample",
    )(sinks, q, kv, cache_k, cache_v, g)


def _ffn_kernel(*refs, seq_tiles):
    carry_mode = seq_tiles is not None
    if carry_mode:
        (x_ref, na_ref, nb_ref, woa_ref, wob_ref, g_ref, wg_ref, wu_ref, cw_ref, cb_ref, wd_ref, gf_ref,
         o_ref, st_ref, h_scr, acc_scr, ext_scr, carry_scr) = refs
    else:
        (x_ref, na_ref, nb_ref, woa_ref, wob_ref, g_ref, wg_ref, wu_ref, cw_ref, cb_ref, wd_ref, gf_ref, state_ref,
         o_ref, st_ref, h_scr, acc_scr) = refs
    i = pl.program_id(0)
    j = pl.program_id(1)

    @pl.when(j == 0)
    def _():
        x1 = x_ref[...] + _dot(na_ref[...], woa_ref[...]) + _dot(nb_ref[...], wob_ref[...])
        acc_scr[...] = x1
        h_scr[...] = _rms(x1, g_ref[...]).astype(_BF)

    h = h_scr[...]
    gate = _dot(h, wg_ref[...])
    if carry_mode:
        conv = _conv3_carry(gate, cw_ref, ext_scr, carry_scr.at[j], i % seq_tiles == 0)
        st_ref[0] = gate[gate.shape[0] - 2:, :]
    else:
        conv, last2 = _conv3_seqs(gate, cw_ref, state_ref)
        st_ref[...] = last2
    a = conv + cb_ref[...]
    act = ((a / (1.0 + jnp.exp(-a))) * _dot(h, wu_ref[...])).astype(_BF)
    acc_scr[...] += _dot(act, wd_ref[...])

    @pl.when(j == FF_STEPS - 1)
    def _():
        o_ref[...] = _rms(acc_scr[...], gf_ref[...])


def _ffn(x, na, nb, w_out, g, w_gate, w_up, conv_w, conv_b, w_down, g_final, state, *, seq_len):
    n = x.shape[0]
    tm = FF_ROWS
    nt = n // tm
    carry_mode = seq_len >= tm
    seq_tiles = seq_len // tm if carry_mode else None

    def st_map(i, j):
        return (i, 0, j)

    in_specs = [
        pl.BlockSpec((tm, D_MODEL), lambda i, j: (i, 0)),
        pl.BlockSpec((tm, ATTN_WIDTH), lambda i, j: (i, 0)),
        pl.BlockSpec((tm, CONV_CH), lambda i, j: (i, 0)),
        pl.BlockSpec((ATTN_WIDTH, D_MODEL), lambda i, j: (0, 0), pipeline_mode=pl.Buffered(1)),
        pl.BlockSpec((CONV_CH, D_MODEL), lambda i, j: (1, 0), pipeline_mode=pl.Buffered(1)),
        pl.BlockSpec((1, D_MODEL), lambda i, j: (0, 0)),
        pl.BlockSpec((None, D_MODEL, FF_TILE), lambda i, j: (j, 0, 0)),
        pl.BlockSpec((None, D_MODEL, FF_TILE), lambda i, j: (j, 0, 0)),
        pl.BlockSpec((3, FF_TILE), lambda i, j: (0, j)),
        pl.BlockSpec((1, FF_TILE), lambda i, j: (0, j)),
        pl.BlockSpec((FF_TILE, D_MODEL), lambda i, j: (j, 0)),
        pl.BlockSpec((1, D_MODEL), lambda i, j: (0, 0)),
    ]
    args = [x, na, nb, w_out, w_out, g, w_gate, w_up, conv_w, conv_b, w_down, g_final]
    scratch = [pltpu.VMEM((tm, D_MODEL), _BF), pltpu.VMEM((tm, D_MODEL), _F32)]
    if carry_mode:
        st_shape = jax.ShapeDtypeStruct((nt, 2, D_FF), _F32)
        st_spec = pl.BlockSpec((1, 2, FF_TILE), st_map)
        scratch += [pltpu.VMEM((tm + SUBLANES, FF_TILE), _F32),
                    pltpu.VMEM((FF_STEPS, SUBLANES, FF_TILE), _F32)]
    else:
        nseq = tm // seq_len
        st_shape = jax.ShapeDtypeStruct((n // seq_len, 2, D_FF), _F32)
        st_spec = pl.BlockSpec((nseq, 2, FF_TILE), st_map)
        in_specs.append(st_spec)
        args.append(state)
    return pl.pallas_call(
        functools.partial(_ffn_kernel, seq_tiles=seq_tiles),
        grid=(nt, FF_STEPS),
        in_specs=in_specs,
        out_specs=[pl.BlockSpec((tm, D_MODEL), lambda i, j: (i, 0)), st_spec],
        out_shape=[jax.ShapeDtypeStruct((n, D_MODEL), _F32), st_shape],
        scratch_shapes=scratch,
        compiler_params=pltpu.CompilerParams(
            dimension_semantics=("arbitrary", "arbitrary"), vmem_limit_bytes=VMEM_LIMIT),
        name="ffn_carry" if carry_mode else "ffn_seqs",
    )(*args)


def _tile_cast_kernel(w_ref, o_ref):
    o_ref[...] = w_ref[...].astype(_BF)


def _tile_cast(w, col_tile):
    k, n = w.shape
    return pl.pallas_call(
        _tile_cast_kernel,
        grid=(n // col_tile,),
        in_specs=[pl.BlockSpec((k, col_tile), lambda t: (0, t))],
        out_specs=pl.BlockSpec((None, k, col_tile), lambda t: (t, 0, 0)),
        out_shape=jax.ShapeDtypeStruct((n // col_tile, k, col_tile), _BF),
        compiler_params=pltpu.CompilerParams(dimension_semantics=("arbitrary",)),
        name="tile_cast",
    )(w)


def kernel(x_prompt, x_sample, cache_k_window, cache_v_window, state_conv, state_ffn_conv, g_attn_norm, w_in, attn_sinks, conv_w, g_out_attn, g_out_conv, w_out, g_ffn_norm, w_gate, w_up, ffn_conv_w, ffn_conv_b, w_down, g_final):
    assert w_in.shape[0] == 1, "single layer"
    bp, sp, _ = x_prompt.shape
    bs, ts, _ = x_sample.shape
    assert sp % IN_ROWS == 0 and (bs * ts) % IN_ROWS == 0 and ts == SUBLANES

    w_in_b = _tile_cast(w_in[0], COL_TILE)
    g_attn = g_attn_norm[0].reshape(1, D_MODEL)
    g_ffn = g_ffn_norm[0].reshape(1, D_MODEL)
    g_fin = g_final.reshape(1, D_MODEL)
    g_oa = g_out_attn[0].reshape(1, ATTN_WIDTH)
    g_oc = g_out_conv[0].reshape(1, CONV_CH)
    sinks = attn_sinks[0].reshape(1, N_HEADS)
    cw = conv_w[0]
    fcw = ffn_conv_w[0]
    fcb = ffn_conv_b[0].reshape(1, D_FF)

    xp = x_prompt.reshape(bp * sp, D_MODEL)
    q_p, kv_p, nb_p, cst_p, w_out_b, w_down_b = _inproj(
        xp, g_attn, w_in_b, cw, g_oc, None, [(w_out[0], None), (w_down[0], None)], seq_len=sp, q_dtype=_BF)
    na_p, w_gate_b, w_up_b = _attn_prompt(
        q_p, kv_p, sinks, g_oa, [(w_gate[0], FF_TILE), (w_up[0], FF_TILE)], batch=bp, seq=sp)
    y_p, fst_p = _ffn(xp, na_p, nb_p, w_out_b, g_ffn, w_gate_b, w_up_b, fcw, fcb, w_down_b, g_fin, None, seq_len=sp)

    seq_tiles = sp // min(IN_ROWS, sp)
    ff_seq_tiles = sp // FF_ROWS
    kv_p3 = kv_p.reshape(bp, sp, 2 * KV_WIDTH)[:, sp - WINDOW:, :]
    k_win_p = kv_p3[:, :, :KV_WIDTH].reshape(1, bp, WINDOW, N_KV_HEADS, HEAD_DIM)
    v_win_p = kv_p3[:, :, KV_WIDTH:].reshape(1, bp, WINDOW, N_KV_HEADS, HEAD_DIM)
    conv_p = cst_p[seq_tiles - 1::seq_tiles][None]
    ffn_conv_p = fst_p[ff_seq_tiles - 1::ff_seq_tiles][None]

    xs = x_sample.reshape(bs * ts, D_MODEL)
    q_s, kv_s, nb_s, cst_s = _inproj(xs, g_attn, w_in_b, cw, g_oc, state_conv[0], [], seq_len=ts, q_dtype=_F32)
    ck = cache_k_window[0].reshape(bs, WINDOW, KV_WIDTH)
    cv = cache_v_window[0].reshape(bs, WINDOW, KV_WIDTH)
    na_s, k_win_s, v_win_s = _attn_sample(q_s, kv_s, ck, cv, sinks, g_oa, nseq_total=bs, t_new=ts)
    y_s, fst_s = _ffn(xs, na_s, nb_s, w_out_b, g_ffn, w_gate_b, w_up_b, fcw, fcb, w_down_b, g_fin,
                      state_ffn_conv[0], seq_len=ts)

    win_shape = (1, bs, WINDOW, N_KV_HEADS, HEAD_DIM)
    return (y_p.reshape(bp, sp, D_MODEL), y_s.reshape(bs, ts, D_MODEL),
            k_win_p, v_win_p, conv_p, ffn_conv_p,
            k_win_s.reshape(win_shape), v_win_s.reshape(win_shape), cst_s[None], fst_s[None])
```

```python
import functools

import jax
import jax.numpy as jnp
from jax import lax
from jax.experimental import pallas as pl
from jax.experimental.pallas import tpu as pltpu

D_MODEL = 2048
ATTN_WIDTH = 1024
CONV_CH = 1024
HEAD_DIM = 64
N_HEADS = 16
N_KV_HEADS = 4
GROUP = 4
KV_WIDTH = 256
WINDOW = 128
BLOCK = 128
D_FF = 5632
EPS = 1e-6
NEG = -1e30

SUBLANES = 8
Q_BLOCKS = 2
SAMPLE_SEQS = 16
IN_ROWS = 512
COL_TILE = 512
assert ATTN_WIDTH == 2 * COL_TILE and 2 * KV_WIDTH == COL_TILE
CONV_STEP0 = 1
CONV_STEPS = CONV_CH // COL_TILE
IN_STEPS = CONV_STEP0 + CONV_STEPS
FF_ROWS = 512
FF_TILE = 512
FF_STEPS = D_FF // FF_TILE
B_BLK0 = (ATTN_WIDTH + 2 * KV_WIDTH) // COL_TILE
C_BLK0 = B_BLK0 + CONV_STEPS
U_BLK0 = C_BLK0 + CONV_STEPS
VMEM_LIMIT = 56 * 1024 * 1024

_SLOPES = [2.0 ** (-8.0 * (h + 1) / N_HEADS) for h in range(N_HEADS)]
_BF = jnp.bfloat16
_F32 = jnp.float32


def _rms(x, g):
    return x * lax.rsqrt(jnp.mean(x * x, axis=-1, keepdims=True) + EPS) * g


def _dot(a, b):
    return jnp.dot(a, b, preferred_element_type=_F32)


def _dot_nt(a, b):
    return lax.dot_general(a, b, (((1,), (1,)), ((), ())), preferred_element_type=_F32)


def _conv3_carry(cur, w_ref, ext_ref, carry_ref, is_seq_start):
    rows = cur.shape[0]
    prev = jnp.where(is_seq_start, 0.0, carry_ref[...])
    ext_ref[0:SUBLANES, :] = prev
    ext_ref[SUBLANES:SUBLANES + rows, :] = cur
    carry_ref[...] = cur[rows - SUBLANES:, :]
    p1 = ext_ref[SUBLANES - 1:SUBLANES - 1 + rows, :]
    p2 = ext_ref[SUBLANES - 2:SUBLANES - 2 + rows, :]
    return w_ref[2:3, :] * cur + w_ref[1:2, :] * p1 + w_ref[0:1, :] * p2


def _conv3_seqs(cur, w_ref, state_ref):
    rows, ch = cur.shape
    nseq = rows // SUBLANES
    cur3 = cur.reshape(nseq, SUBLANES, ch)
    st = state_ref[...]
    t = lax.broadcasted_iota(jnp.int32, cur3.shape, 1)
    s0 = jnp.broadcast_to(st[:, 0:1, :], cur3.shape)
    s1 = jnp.broadcast_to(st[:, 1:2, :], cur3.shape)
    p1 = jnp.where(t == 0, s1, pltpu.roll(cur3, 1, 1))
    p2 = jnp.where(t == 0, s0, jnp.where(t == 1, s1, pltpu.roll(cur3, 2, 1)))
    w = w_ref[...]
    out = w[2:3, :][None] * cur3 + w[1:2, :][None] * p1 + w[0:1, :][None] * p2
    return out.reshape(rows, ch), cur3[:, SUBLANES - 2:, :]


def _inproj_kernel(*refs, seq_tiles, q_dtype, n_cast):
    carry_mode = seq_tiles is not None
    n_in = 7 if carry_mode else 8
    ins, rest = refs[:n_in], refs[n_in:]
    cast_in, outs, cast_out, scr = (rest[:n_cast], rest[n_cast:n_cast + 4],
                                    rest[n_cast + 4:2 * n_cast + 4], rest[2 * n_cast + 4:])
    q_ref, kv_ref, nb_ref, st_ref = outs
    if carry_mode:
        x_ref, g_ref, wa_ref, wb_ref, wc_ref, cw_ref, gc_ref = ins
        h_scr, y_scr, ss_scr, ext_scr, carry_scr = scr
    else:
        x_ref, g_ref, wa_ref, wb_ref, wc_ref, cw_ref, gc_ref, state_ref = ins
        h_scr, y_scr, ss_scr = scr
    i = pl.program_id(0)
    j = pl.program_id(1)

    @pl.when(j == 0)
    def _():
        _cast_rows(cast_in + cast_out)
        h = _rms(x_ref[...], g_ref[...]).astype(_BF)
        h_scr[...] = h
        q_ref[:, :COL_TILE] = (_dot(h, wa_ref[...]) * (HEAD_DIM ** -0.5)).astype(q_dtype)
        q_ref[:, COL_TILE:] = (_dot(h, wb_ref[...]) * (HEAD_DIM ** -0.5)).astype(q_dtype)
        kv_ref[...] = _dot(h, wc_ref[...])

    for c in range(CONV_STEPS):
        @pl.when(j == CONV_STEP0 + c)
        def _(c=c):
            h = h_scr[...]
            cu = _dot(h, wb_ref[...]) * _dot(h, wc_ref[...])
            if carry_mode:
                conv = _conv3_carry(cu, cw_ref, ext_scr, carry_scr.at[c], i % seq_tiles == 0)
                st_ref[0] = cu[cu.shape[0] - 2:, :]
            else:
                conv, last2 = _conv3_seqs(cu, cw_ref, state_ref)
                st_ref[...] = last2
            y = _dot(h, wa_ref[...]) * conv
            y_scr[c] = y
            ss = jnp.sum(y * y, axis=-1, keepdims=True)
            if c == 0:
                ss_scr[...] = ss
            else:
                ss_scr[...] += ss
            if c == CONV_STEPS - 1:
                inv = lax.rsqrt(ss_scr[...] * (1.0 / CONV_CH) + EPS)
                for cc in range(CONV_STEPS):
                    sl = slice(cc * COL_TILE, (cc + 1) * COL_TILE)
                    nb_ref[:, sl] = (y_scr[cc] * inv * gc_ref[:, sl]).astype(_BF)


def _inproj(x, g, w_in, conv_w, g_conv, state, cast_weights, *, seq_len, q_dtype):
    n = x.shape[0]
    tm = IN_ROWS
    nt = n // tm
    carry_mode = seq_len >= tm
    seq_tiles = seq_len // tm if carry_mode else None

    def conv_col(j):
        return jnp.maximum(j - CONV_STEP0, 0)

    def w_map(first, slab0):
        return lambda i, j: (jnp.where(j == 0, first, slab0 + conv_col(j)), 0, 0)

    def st_map(i, j):
        return (i, 0, conv_col(j))

    in_specs = [
        pl.BlockSpec((tm, D_MODEL), lambda i, j: (i, 0)),
        pl.BlockSpec((1, D_MODEL), lambda i, j: (0, 0)),
        pl.BlockSpec((None, D_MODEL, COL_TILE), w_map(0, B_BLK0)),
        pl.BlockSpec((None, D_MODEL, COL_TILE), w_map(1, C_BLK0)),
        pl.BlockSpec((None, D_MODEL, COL_TILE), w_map(2, U_BLK0)),
        pl.BlockSpec((3, COL_TILE), lambda i, j: (0, conv_col(j))),
        pl.BlockSpec((1, CONV_CH), lambda i, j: (0, 0)),
    ]
    args = [x, g, w_in, w_in, w_in, conv_w, g_conv]
    scratch = [
        pltpu.VMEM((tm, D_MODEL), _BF),
        pltpu.VMEM((CONV_STEPS, tm, COL_TILE), _F32),
        pltpu.VMEM((tm, 1), _F32),
    ]
    if carry_mode:
        st_shape = jax.ShapeDtypeStruct((nt, 2, CONV_CH), _F32)
        st_spec = pl.BlockSpec((1, 2, COL_TILE), st_map)
        scratch += [pltpu.VMEM((tm + SUBLANES, COL_TILE), _F32),
                    pltpu.VMEM((CONV_STEPS, SUBLANES, COL_TILE), _F32)]
    else:
        nseq = tm // seq_len
        st_shape = jax.ShapeDtypeStruct((n // seq_len, 2, CONV_CH), _F32)
        st_spec = pl.BlockSpec((nseq, 2, COL_TILE), st_map)
        in_specs.append(st_spec)
        args.append(state)
    out_shape = [
        jax.ShapeDtypeStruct((n, ATTN_WIDTH), q_dtype),
        jax.ShapeDtypeStruct((n, 2 * KV_WIDTH), _F32),
        jax.ShapeDtypeStruct((n, CONV_CH), _BF),
        st_shape,
    ]
    out_specs = [
        pl.BlockSpec((tm, ATTN_WIDTH), lambda i, j: (i, 0)),
        pl.BlockSpec((tm, 2 * KV_WIDTH), lambda i, j: (i, 0)),
        pl.BlockSpec((tm, CONV_CH), lambda i, j: (i, 0)),
        st_spec,
    ]
    cast_in, cast_out, cast_shapes = _cast_specs(cast_weights, nt, lambda i, j: i)
    return pl.pallas_call(
        functools.partial(_inproj_kernel, seq_tiles=seq_tiles, q_dtype=q_dtype, n_cast=len(cast_weights)),
        grid=(nt, IN_STEPS),
        in_specs=in_specs + cast_in,
        out_specs=out_specs + cast_out,
        out_shape=out_shape + cast_shapes,
        scratch_shapes=scratch,
        compiler_params=pltpu.CompilerParams(
            dimension_semantics=("arbitrary", "arbitrary"), vmem_limit_bytes=VMEM_LIMIT),
        name="inproj_carry" if carry_mode else "inproj_seqs",
    )(*args, *[w for w, _ in cast_weights])


def _softmax_sink(s, sink):
    m = jnp.maximum(jnp.max(s, axis=-1, keepdims=True), sink)
    p = jnp.exp(s - m)
    return p, jnp.sum(p, axis=-1, keepdims=True) + jnp.exp(sink - m)


def _half_lane_operands(slab, low_half):
    lo = lax.broadcasted_iota(jnp.int32, slab.shape, 1) < HEAD_DIM
    moved = pltpu.roll(slab, HEAD_DIM, 1)
    in_low = jnp.where(lo, slab if low_half else moved, 0.0).astype(_BF)
    in_high = jnp.where(lo, 0.0, moved if low_half else slab).astype(_BF)
    return in_low, in_high


def _attn_prompt_kernel(sinks_ref, q_ref, kvc_ref, kvp_ref, g_ref, o_ref, acc_ref, cast_refs):
    i = pl.program_id(1)
    pair_w = 2 * HEAD_DIM
    qi = lax.broadcasted_iota(jnp.int32, (BLOCK, 2 * BLOCK), 0)
    kj = lax.broadcasted_iota(jnp.int32, (BLOCK, 2 * BLOCK), 1)
    dist_i = qi - kj + BLOCK
    in_window = (dist_i >= 0) & (dist_i <= WINDOW)
    far = -NEG / min(_SLOPES)
    dist_mid = jnp.where(in_window, dist_i.astype(_F32), far)
    dist_first = jnp.where(in_window & ((kj >= BLOCK) | (i > 0)), dist_i.astype(_F32), far)
    low = lax.broadcasted_iota(jnp.int32, (BLOCK, pair_w), 1) < HEAD_DIM

    k_ops, v_ops = [], []
    for m in range(N_KV_HEADS // 2):
        ksl = slice(m * pair_w, (m + 1) * pair_w)
        vsl = slice(KV_WIDTH + m * pair_w, KV_WIDTH + (m + 1) * pair_w)
        kslab = jnp.concatenate([kvp_ref[:, ksl], kvc_ref[:, ksl]], axis=0)
        vslab = jnp.concatenate([kvp_ref[:, vsl], kvc_ref[:, vsl]], axis=0)
        for low_half in (True, False):
            k_ops.append(_half_lane_operands(kslab, low_half))
            v_ops.append(_half_lane_operands(vslab, low_half))

    n_pairs = N_HEADS // 2
    work = [(u, c) for u in range(Q_BLOCKS) for c in range(n_pairs)]

    def scores(u, c):
        qc = q_ref[u * BLOCK:(u + 1) * BLOCK, c * pair_w:(c + 1) * pair_w]
        keys = slice(u * BLOCK, (u + 2) * BLOCK)
        k_low, k_high = k_ops[c // 2]
        return _dot_nt(qc, k_low[keys]), _dot_nt(qc, k_high[keys])

    s_next = scores(*work[0])
    for n, (u, c) in enumerate(work):
        s_pair = s_next
        if n + 1 < len(work):
            s_next = scores(*work[n + 1])
        keys = slice(u * BLOCK, (u + 2) * BLOCK)
        v_low, v_high = v_ops[c // 2]
        dist = dist_first if u == 0 else dist_mid
        p0, d0 = _softmax_sink(s_pair[0] - _SLOPES[2 * c] * dist, sinks_ref[0, 2 * c])
        p1, d1 = _softmax_sink(s_pair[1] - _SLOPES[2 * c + 1] * dist, sinks_ref[0, 2 * c + 1])
        o = _dot(p0.astype(_BF), v_low[keys]) + _dot(p1.astype(_BF), v_high[keys])
        acc_ref[u * BLOCK:(u + 1) * BLOCK, c * pair_w:(c + 1) * pair_w] = o / jnp.where(low, d0, d1)
    o_ref[...] = _rms(acc_ref[...], g_ref[...]).astype(_BF)
    _cast_rows(cast_refs)


def _cast_rows(cast_refs):
    n = len(cast_refs) // 2
    for src, dst in zip(cast_refs[:n], cast_refs[n:]):
        if len(dst.shape) == 2:
            dst[...] = src[...].astype(_BF)
        else:
            tw = dst.shape[2]
            for t in range(dst.shape[0]):
                dst[t] = src[:, t * tw:(t + 1) * tw].astype(_BF)


def _cast_specs(weights, steps, index):
    in_specs, out_specs, shapes = [], [], []
    for w, col_tile in weights:
        k, n = w.shape
        rows = k // steps
        assert rows * steps == k and rows % 16 == 0
        in_specs.append(pl.BlockSpec((rows, n), lambda *ids: (index(*ids), 0)))
        if col_tile is None:
            out_specs.append(pl.BlockSpec((rows, n), lambda *ids: (index(*ids), 0)))
            shapes.append(jax.ShapeDtypeStruct((k, n), _BF))
        else:
            out_specs.append(pl.BlockSpec((n // col_tile, rows, col_tile), lambda *ids: (0, index(*ids), 0)))
            shapes.append(jax.ShapeDtypeStruct((n // col_tile, k, col_tile), _BF))
    return in_specs, out_specs, shapes


def _attn_prompt_body(sinks_ref, q_ref, kvc_ref, kvp_ref, g_ref, *rest, n_cast):
    cast_in, (o_ref,), cast_out, (acc_ref,) = (rest[:n_cast], rest[n_cast:n_cast + 1],
                                               rest[n_cast + 1:2 * n_cast + 1], rest[2 * n_cast + 1:])
    _attn_prompt_kernel(sinks_ref, q_ref, kvc_ref, kvp_ref, g_ref, o_ref, acc_ref, cast_in + cast_out)


def _attn_prompt(q, kv, sinks, g, cast_weights, *, batch, seq):
    rows = Q_BLOCKS * BLOCK
    nb = seq // rows
    cast_in, cast_out, cast_shapes = _cast_specs(cast_weights, batch * nb, lambda b, i: b * nb + i)
    return pl.pallas_call(
        functools.partial(_attn_prompt_body, n_cast=len(cast_weights)),
        grid=(batch, nb),
        in_specs=[
            pl.BlockSpec(memory_space=pltpu.SMEM),
            pl.BlockSpec((rows, ATTN_WIDTH), lambda b, i: (b * nb + i, 0)),
            pl.BlockSpec((rows, 2 * KV_WIDTH), lambda b, i: (b * nb + i, 0)),
            pl.BlockSpec((BLOCK, 2 * KV_WIDTH), lambda b, i: (jnp.maximum((b * nb + i) * Q_BLOCKS - 1, 0), 0)),
            pl.BlockSpec((1, ATTN_WIDTH), lambda b, i: (0, 0)),
        ] + cast_in,
        out_specs=[pl.BlockSpec((rows, ATTN_WIDTH), lambda b, i: (b * nb + i, 0))] + cast_out,
        out_shape=[jax.ShapeDtypeStruct((batch * seq, ATTN_WIDTH), _BF)] + cast_shapes,
        scratch_shapes=[pltpu.VMEM((rows, ATTN_WIDTH), _F32)],
        compiler_params=pltpu.CompilerParams(dimension_semantics=("arbitrary", "arbitrary")),
        name="attn_prompt",
    )(sinks, q, kv, kv, g, *[w for w, _ in cast_weights])


def _round_bf16(x):
    return x.astype(_BF).astype(_F32)


def _attn_sample_kernel(sinks_ref, q_ref, kvn_ref, ck_ref, cv_ref, g_ref,
                        o_ref, kwin_ref, vwin_ref, acc_ref, s_scr, *, nseq, t_new):
    rows = GROUP * t_new
    nkeys = WINDOW + t_new
    ksl = [slice(kvh * HEAD_DIM, (kvh + 1) * HEAD_DIM) for kvh in range(N_KV_HEADS)]
    vsl = [slice(KV_WIDTH + kvh * HEAD_DIM, KV_WIDTH + (kvh + 1) * HEAD_DIM) for kvh in range(N_KV_HEADS)]

    for kvh in range(N_KV_HEADS):
        for b in range(nseq):
            rs = slice(b * t_new, (b + 1) * t_new)
            k = _round_bf16(jnp.concatenate([ck_ref[b, :, ksl[kvh]], kvn_ref[rs, ksl[kvh]]], axis=0))
            qs = jnp.concatenate(
                [q_ref[rs, (kvh * GROUP + gq) * HEAD_DIM:(kvh * GROUP + gq + 1) * HEAD_DIM]
                 for gq in range(GROUP)], axis=0)
            s_scr[kvh, b * rows:(b + 1) * rows, :] = _dot_nt(_round_bf16(qs), k)

    r = lax.broadcasted_iota(jnp.int32, (nseq * rows, 1), 0)
    gq_of_row = (r // t_new) % GROUP
    t = lax.broadcasted_iota(jnp.int32, (nseq * rows, nkeys), 0) % t_new
    kj = lax.broadcasted_iota(jnp.int32, (nseq * rows, nkeys), 1)
    dist_i = t + WINDOW - kj
    valid = (dist_i >= 0) & (dist_i <= WINDOW)
    dist = jnp.where(valid, dist_i.astype(_F32), -NEG / min(_SLOPES))
    for kvh in range(N_KV_HEADS):
        slope = jnp.full((nseq * rows, 1), _SLOPES[kvh * GROUP], _F32)
        sink = jnp.full((nseq * rows, 1), sinks_ref[0, kvh * GROUP], _F32)
        for gq in range(1, GROUP):
            slope = jnp.where(gq_of_row == gq, _SLOPES[kvh * GROUP + gq], slope)
            sink = jnp.where(gq_of_row == gq, sinks_ref[0, kvh * GROUP + gq], sink)
        p, denom = _softmax_sink(s_scr[kvh] - slope * dist, sink)
        s_scr[kvh] = _round_bf16(p / denom)

    for kvh in range(N_KV_HEADS):
        for b in range(nseq):
            rs = slice(b * t_new, (b + 1) * t_new)
            v = _round_bf16(jnp.concatenate([cv_ref[b, :, ksl[kvh]], kvn_ref[rs, vsl[kvh]]], axis=0))
            o = _dot(s_scr[kvh, b * rows:(b + 1) * rows, :], v)
            for gq in range(GROUP):
                h = kvh * GROUP + gq
                acc_ref[rs, h * HEAD_DIM:(h + 1) * HEAD_DIM] = o[gq * t_new:(gq + 1) * t_new, :]
    o_ref[...] = _rms(acc_ref[...], g_ref[...]).astype(_BF)
    kwin_ref[:, 0:WINDOW - t_new, :] = ck_ref[:, t_new:, :]
    vwin_ref[:, 0:WINDOW - t_new, :] = cv_ref[:, t_new:, :]
    kwin_ref[:, WINDOW - t_new:, :] = kvn_ref[:, 0:KV_WIDTH].reshape(nseq, t_new, KV_WIDTH)
    vwin_ref[:, WINDOW - t_new:, :] = kvn_ref[:, KV_WIDTH:].reshape(nseq, t_new, KV_WIDTH)


def _attn_sample(q, kv, cache_k, cache_v, sinks, g, *, nseq_total, t_new, nseq=SAMPLE_SEQS):
    rows = nseq * t_new
    cache_spec = pl.BlockSpec((nseq, WINDOW, KV_WIDTH), lambda i: (i, 0, 0))
    return pl.pallas_call(
        functools.partial(_attn_sample_kernel, nseq=nseq, t_new=t_new),
        grid=(nseq_total // nseq,),
        in_specs=[
            pl.BlockSpec(memory_space=pltpu.SMEM),
            pl.BlockSpec((rows, ATTN_WIDTH), lambda i: (i, 0)),
            pl.BlockSpec((rows, 2 * KV_WIDTH), lambda i: (i, 0)),
            cache_spec,
            cache_spec,
            pl.BlockSpec((1, ATTN_WIDTH), lambda i: (0, 0)),
        ],
        out_specs=[pl.BlockSpec((rows, ATTN_WIDTH), lambda i: (i, 0)), cache_spec, cache_spec],
        out_shape=[
            jax.ShapeDtypeStruct((nseq_total * t_new, ATTN_WIDTH), _BF),
            jax.ShapeDtypeStruct(cache_k.shape, _F32),
            jax.ShapeDtypeStruct(cache_v.shape, _F32),
        ],
        scratch_shapes=[pltpu.VMEM((rows, ATTN_WIDTH), _F32),
                        pltpu.VMEM((N_KV_HEADS, nseq * GROUP * t_new, WINDOW + t_new), _F32)],
        compiler_params=pltpu.CompilerParams(dimension_semantics=("arbitrary",)),
        name="attn_sample",
    )(sinks, q, kv, cache_k, cache_v, g)


def _ffn_kernel(*refs, seq_tiles):
    carry_mode = seq_tiles is not None
    if carry_mode:
        (x_ref, na_ref, nb_ref, woa_ref, wob_ref, g_ref, wg_ref, wu_ref, cw_ref, cb_ref, wd_ref, gf_ref,
         o_ref, st_ref, h_scr, acc_scr, ext_scr, carry_scr) = refs
    else:
        (x_ref, na_ref, nb_ref, woa_ref, wob_ref, g_ref, wg_ref, wu_ref, cw_ref, cb_ref, wd_ref, gf_ref, state_ref,
         o_ref, st_ref, h_scr, acc_scr) = refs
    i = pl.program_id(0)
    j = pl.program_id(1)

    @pl.when(j == 0)
    def _():
        x1 = x_ref[...] + _dot(na_ref[...], woa_ref[...]) + _dot(nb_ref[...], wob_ref[...])
        acc_scr[...] = x1
        h_scr[...] = _rms(x1, g_ref[...]).astype(_BF)

    h = h_scr[...]
    gate = _dot(h, wg_ref[...])
    if carry_mode:
        conv = _conv3_carry(gate, cw_ref, ext_scr, carry_scr.at[j], i % seq_tiles == 0)
        st_ref[0] = gate[gate.shape[0] - 2:, :]
    else:
        conv, last2 = _conv3_seqs(gate, cw_ref, state_ref)
        st_ref[...] = last2
    a = conv + cb_ref[...]
    act = ((a / (1.0 + jnp.exp(-a))) * _dot(h, wu_ref[...])).astype(_BF)
    acc_scr[...] += _dot(act, wd_ref[...])

    @pl.when(j == FF_STEPS - 1)
    def _():
        o_ref[...] = _rms(acc_scr[...], gf_ref[...])


def _ffn(x, na, nb, w_out, g, w_gate, w_up, conv_w, conv_b, w_down, g_final, state, *, seq_len):
    n = x.shape[0]
    tm = FF_ROWS
    nt = n // tm
    carry_mode = seq_len >= tm
    seq_tiles = seq_len // tm if carry_mode else None

    def st_map(i, j):
        return (i, 0, j)

    in_specs = [
        pl.BlockSpec((tm, D_MODEL), lambda i, j: (i, 0)),
        pl.BlockSpec((tm, ATTN_WIDTH), lambda i, j: (i, 0)),
        pl.BlockSpec((tm, CONV_CH), lambda i, j: (i, 0)),
        pl.BlockSpec((ATTN_WIDTH, D_MODEL), lambda i, j: (0, 0), pipeline_mode=pl.Buffered(1)),
        pl.BlockSpec((CONV_CH, D_MODEL), lambda i, j: (1, 0), pipeline_mode=pl.Buffered(1)),
        pl.BlockSpec((1, D_MODEL), lambda i, j: (0, 0)),
        pl.BlockSpec((None, D_MODEL, FF_TILE), lambda i, j: (j, 0, 0)),
        pl.BlockSpec((None, D_MODEL, FF_TILE), lambda i, j: (j, 0, 0)),
        pl.BlockSpec((3, FF_TILE), lambda i, j: (0, j)),
        pl.BlockSpec((1, FF_TILE), lambda i, j: (0, j)),
        pl.BlockSpec((FF_TILE, D_MODEL), lambda i, j: (j, 0)),
        pl.BlockSpec((1, D_MODEL), lambda i, j: (0, 0)),
    ]
    args = [x, na, nb, w_out, w_out, g, w_gate, w_up, conv_w, conv_b, w_down, g_final]
    scratch = [pltpu.VMEM((tm, D_MODEL), _BF), pltpu.VMEM((tm, D_MODEL), _F32)]
    if carry_mode:
        st_shape = jax.ShapeDtypeStruct((nt, 2, D_FF), _F32)
        st_spec = pl.BlockSpec((1, 2, FF_TILE), st_map)
        scratch += [pltpu.VMEM((tm + SUBLANES, FF_TILE), _F32),
                    pltpu.VMEM((FF_STEPS, SUBLANES, FF_TILE), _F32)]
    else:
        nseq = tm // seq_len
        st_shape = jax.ShapeDtypeStruct((n // seq_len, 2, D_FF), _F32)
        st_spec = pl.BlockSpec((nseq, 2, FF_TILE), st_map)
        in_specs.append(st_spec)
        args.append(state)
    return pl.pallas_call(
        functools.partial(_ffn_kernel, seq_tiles=seq_tiles),
        grid=(nt, FF_STEPS),
        in_specs=in_specs,
        out_specs=[pl.BlockSpec((tm, D_MODEL), lambda i, j: (i, 0)), st_spec],
        out_shape=[jax.ShapeDtypeStruct((n, D_MODEL), _F32), st_shape],
        scratch_shapes=scratch,
        compiler_params=pltpu.CompilerParams(
            dimension_semantics=("arbitrary", "arbitrary"), vmem_limit_bytes=VMEM_LIMIT),
        name="ffn_carry" if carry_mode else "ffn_seqs",
    )(*args)


def _tile_cast_kernel(w_ref, o_ref):
    o_ref[...] = w_ref[...].astype(_BF)


def _tile_cast(w, col_tile):
    k, n = w.shape
    return pl.pallas_call(
        _tile_cast_kernel,
        grid=(n // col_tile,),
        in_specs=[pl.BlockSpec((k, col_tile), lambda t: (0, t))],
        out_specs=pl.BlockSpec((None, k, col_tile), lambda t: (t, 0, 0)),
        out_shape=jax.ShapeDtypeStruct((n // col_tile, k, col_tile), _BF),
        compiler_params=pltpu.CompilerParams(dimension_semantics=("arbitrary",)),
        name="tile_cast",
    )(w)


def kernel(x_prompt, x_sample, cache_k_window, cache_v_window, state_conv, state_ffn_conv, g_attn_norm, w_in, attn_sinks, conv_w, g_out_attn, g_out_conv, w_out, g_ffn_norm, w_gate, w_up, ffn_conv_w, ffn_conv_b, w_down, g_final):
    assert w_in.shape[0] == 1, "single layer"
    bp, sp, _ = x_prompt.shape
    bs, ts, _ = x_sample.shape
    assert sp % IN_ROWS == 0 and (bs * ts) % IN_ROWS == 0 and ts == SUBLANES

    w_in_b = _tile_cast(w_in[0], COL_TILE)
    g_attn = g_attn_norm[0].reshape(1, D_MODEL)
    g_ffn = g_ffn_norm[0].reshape(1, D_MODEL)
    g_fin = g_final.reshape(1, D_MODEL)
    g_oa = g_out_attn[0].reshape(1, ATTN_WIDTH)
    g_oc = g_out_conv[0].reshape(1, CONV_CH)
    sinks = attn_sinks[0].reshape(1, N_HEADS)
    cw = conv_w[0]
    fcw = ffn_conv_w[0]
    fcb = ffn_conv_b[0].reshape(1, D_FF)

    xp = x_prompt.reshape(bp * sp, D_MODEL)
    q_p, kv_p, nb_p, cst_p, w_out_b, w_down_b = _inproj(
        xp, g_attn, w_in_b, cw, g_oc, None, [(w_out[0], None), (w_down[0], None)], seq_len=sp, q_dtype=_BF)
    na_p, w_gate_b, w_up_b = _attn_prompt(
        q_p, kv_p, sinks, g_oa, [(w_gate[0], FF_TILE), (w_up[0], FF_TILE)], batch=bp, seq=sp)
    y_p, fst_p = _ffn(xp, na_p, nb_p, w_out_b, g_ffn, w_gate_b, w_up_b, fcw, fcb, w_down_b, g_fin, None, seq_len=sp)

    seq_tiles = sp // min(IN_ROWS, sp)
    ff_seq_tiles = sp // FF_ROWS
    kv_p3 = kv_p.reshape(bp, sp, 2 * KV_WIDTH)[:, sp - WINDOW:, :]
    k_win_p = kv_p3[:, :, :KV_WIDTH].reshape(1, bp, WINDOW, N_KV_HEADS, HEAD_DIM)
    v_win_p = kv_p3[:, :, KV_WIDTH:].reshape(1, bp, WINDOW, N_KV_HEADS, HEAD_DIM)
    conv_p = cst_p[seq_tiles - 1::seq_tiles][None]
    ffn_conv_p = fst_p[ff_seq_tiles - 1::ff_seq_tiles][None]

    xs = x_sample.reshape(bs * ts, D_MODEL)
    q_s, kv_s, nb_s, cst_s = _inproj(xs, g_attn, w_in_b, cw, g_oc, state_conv[0], [], seq_len=ts, q_dtype=_F32)
    ck = cache_k_window[0].reshape(bs, WINDOW, KV_WIDTH)
    cv = cache_v_window[0].reshape(bs, WINDOW, KV_WIDTH)
    na_s, k_win_s, v_win_s = _attn_sample(q_s, kv_s, ck, cv, sinks, g_oa, nseq_total=bs, t_new=ts)
    y_s, fst_s = _ffn(xs, na_s, nb_s, w_out_b, g_ffn, w_gate_b, w_up_b, fcw, fcb, w_down_b, g_fin,
                      state_ffn_conv[0], seq_len=ts)

    win_shape = (1, bs, WINDOW, N_KV_HEADS, HEAD_DIM)
    return (y_p.reshape(bp, sp, D_MODEL), y_s.reshape(bs, ts, D_MODEL),
            k_win_p, v_win_p, conv_p, ffn_conv_p,
            k_win_s.reshape(win_shape), v_win_s.reshape(win_shape), cst_s[None], fst_s[None])
```

```python
import functools

import jax
import jax.numpy as jnp
from jax import lax
from jax.experimental import pallas as pl
from jax.experimental.pallas import tpu as pltpu

D_MODEL = 2048
ATTN_WIDTH = 1024
CONV_CH = 1024
HEAD_DIM = 64
N_HEADS = 16
N_KV_HEADS = 4
GROUP = 4
KV_WIDTH = 256
WINDOW = 128
BLOCK = 128
D_FF = 5632
EPS = 1e-6
NEG = -1e30

SUBLANES = 8
Q_BLOCKS = 4
SAMPLE_SEQS = 32
IN_ROWS = 512
COL_TILE = 512
assert ATTN_WIDTH == 2 * COL_TILE and 2 * KV_WIDTH == COL_TILE
CONV_STEP0 = 1
CONV_STEPS = CONV_CH // COL_TILE
IN_STEPS = CONV_STEP0 + CONV_STEPS
FF_ROWS = 512
FF_TILE = 512
FF_STEPS = D_FF // FF_TILE
B_BLK0 = (ATTN_WIDTH + 2 * KV_WIDTH) // COL_TILE
C_BLK0 = B_BLK0 + CONV_STEPS
U_BLK0 = C_BLK0 + CONV_STEPS
VMEM_LIMIT = 56 * 1024 * 1024

_SLOPES = [2.0 ** (-8.0 * (h + 1) / N_HEADS) for h in range(N_HEADS)]
_BF = jnp.bfloat16
_F32 = jnp.float32


def _rms(x, g):
    return x * lax.rsqrt(jnp.mean(x * x, axis=-1, keepdims=True) + EPS) * g


def _dot(a, b):
    return jnp.dot(a, b, preferred_element_type=_F32)


def _dot_nt(a, b):
    return lax.dot_general(a, b, (((1,), (1,)), ((), ())), preferred_element_type=_F32)


def _conv3_carry(cur, w_ref, ext_ref, carry_ref, is_seq_start):
    rows = cur.shape[0]
    prev = jnp.where(is_seq_start, 0.0, carry_ref[...])
    ext_ref[0:SUBLANES, :] = prev
    ext_ref[SUBLANES:SUBLANES + rows, :] = cur
    carry_ref[...] = cur[rows - SUBLANES:, :]
    p1 = ext_ref[SUBLANES - 1:SUBLANES - 1 + rows, :]
    p2 = ext_ref[SUBLANES - 2:SUBLANES - 2 + rows, :]
    return w_ref[2:3, :] * cur + w_ref[1:2, :] * p1 + w_ref[0:1, :] * p2


def _conv3_seqs(cur, w_ref, state_ref):
    rows, ch = cur.shape
    nseq = rows // SUBLANES
    cur3 = cur.reshape(nseq, SUBLANES, ch)
    st = state_ref[...]
    t = lax.broadcasted_iota(jnp.int32, cur3.shape, 1)
    s0 = jnp.broadcast_to(st[:, 0:1, :], cur3.shape)
    s1 = jnp.broadcast_to(st[:, 1:2, :], cur3.shape)
    p1 = jnp.where(t == 0, s1, pltpu.roll(cur3, 1, 1))
    p2 = jnp.where(t == 0, s0, jnp.where(t == 1, s1, pltpu.roll(cur3, 2, 1)))
    w = w_ref[...]
    out = w[2:3, :][None] * cur3 + w[1:2, :][None] * p1 + w[0:1, :][None] * p2
    return out.reshape(rows, ch), cur3[:, SUBLANES - 2:, :]


def _inproj_kernel(*refs, seq_tiles, q_dtype, n_cast):
    carry_mode = seq_tiles is not None
    n_in = 7 if carry_mode else 8
    ins, rest = refs[:n_in], refs[n_in:]
    cast_in, outs, cast_out, scr = (rest[:n_cast], rest[n_cast:n_cast + 4],
                                    rest[n_cast + 4:2 * n_cast + 4], rest[2 * n_cast + 4:])
    q_ref, kv_ref, nb_ref, st_ref = outs
    if carry_mode:
        x_ref, g_ref, wa_ref, wb_ref, wc_ref, cw_ref, gc_ref = ins
        h_scr, y_scr, ss_scr, ext_scr, carry_scr = scr
    else:
        x_ref, g_ref, wa_ref, wb_ref, wc_ref, cw_ref, gc_ref, state_ref = ins
        h_scr, y_scr, ss_scr = scr
    i = pl.program_id(0)
    j = pl.program_id(1)

    @pl.when(j == 0)
    def _():
        _cast_rows(cast_in + cast_out)
        h = _rms(x_ref[...], g_ref[...]).astype(_BF)
        h_scr[...] = h
        q_ref[:, :COL_TILE] = (_dot(h, wa_ref[...]) * (HEAD_DIM ** -0.5)).astype(q_dtype)
        q_ref[:, COL_TILE:] = (_dot(h, wb_ref[...]) * (HEAD_DIM ** -0.5)).astype(q_dtype)
        kv_ref[...] = _dot(h, wc_ref[...])

    for c in range(CONV_STEPS):
        @pl.when(j == CONV_STEP0 + c)
        def _(c=c):
            h = h_scr[...]
            cu = _dot(h, wb_ref[...]) * _dot(h, wc_ref[...])
            if carry_mode:
                conv = _conv3_carry(cu, cw_ref, ext_scr, carry_scr.at[c], i % seq_tiles == 0)
                st_ref[0] = cu[cu.shape[0] - 2:, :]
            else:
                conv, last2 = _conv3_seqs(cu, cw_ref, state_ref)
                st_ref[...] = last2
            y = _dot(h, wa_ref[...]) * conv
            y_scr[c] = y
            ss = jnp.sum(y * y, axis=-1, keepdims=True)
            if c == 0:
                ss_scr[...] = ss
            else:
                ss_scr[...] += ss
            if c == CONV_STEPS - 1:
                inv = lax.rsqrt(ss_scr[...] * (1.0 / CONV_CH) + EPS)
                for cc in range(CONV_STEPS):
                    sl = slice(cc * COL_TILE, (cc + 1) * COL_TILE)
                    nb_ref[:, sl] = (y_scr[cc] * inv * gc_ref[:, sl]).astype(_BF)


def _inproj(x, g, w_in, conv_w, g_conv, state, cast_weights, *, seq_len, q_dtype):
    n = x.shape[0]
    tm = IN_ROWS
    nt = n // tm
    carry_mode = seq_len >= tm
    seq_tiles = seq_len // tm if carry_mode else None

    def conv_col(j):
        return jnp.maximum(j - CONV_STEP0, 0)

    def w_map(first, slab0):
        return lambda i, j: (jnp.where(j == 0, first, slab0 + conv_col(j)), 0, 0)

    def st_map(i, j):
        return (i, 0, conv_col(j))

    in_specs = [
        pl.BlockSpec((tm, D_MODEL), lambda i, j: (i, 0)),
        pl.BlockSpec((1, D_MODEL), lambda i, j: (0, 0)),
        pl.BlockSpec((None, D_MODEL, COL_TILE), w_map(0, B_BLK0)),
        pl.BlockSpec((None, D_MODEL, COL_TILE), w_map(1, C_BLK0)),
        pl.BlockSpec((None, D_MODEL, COL_TILE), w_map(2, U_BLK0)),
        pl.BlockSpec((3, COL_TILE), lambda i, j: (0, conv_col(j))),
        pl.BlockSpec((1, CONV_CH), lambda i, j: (0, 0)),
    ]
    args = [x, g, w_in, w_in, w_in, conv_w, g_conv]
    scratch = [
        pltpu.VMEM((tm, D_MODEL), _BF),
        pltpu.VMEM((CONV_STEPS, tm, COL_TILE), _F32),
        pltpu.VMEM((tm, 1), _F32),
    ]
    if carry_mode:
        st_shape = jax.ShapeDtypeStruct((nt, 2, CONV_CH), _F32)
        st_spec = pl.BlockSpec((1, 2, COL_TILE), st_map)
        scratch += [pltpu.VMEM((tm + SUBLANES, COL_TILE), _F32),
                    pltpu.VMEM((CONV_STEPS, SUBLANES, COL_TILE), _F32)]
    else:
        nseq = tm // seq_len
        st_shape = jax.ShapeDtypeStruct((n // seq_len, 2, CONV_CH), _F32)
        st_spec = pl.BlockSpec((nseq, 2, COL_TILE), st_map)
        in_specs.append(st_spec)
        args.append(state)
    out_shape = [
        jax.ShapeDtypeStruct((n, ATTN_WIDTH), q_dtype),
        jax.ShapeDtypeStruct((n, 2 * KV_WIDTH), _F32),
        jax.ShapeDtypeStruct((n, CONV_CH), _BF),
        st_shape,
    ]
    out_specs = [
        pl.BlockSpec((tm, ATTN_WIDTH), lambda i, j: (i, 0)),
        pl.BlockSpec((tm, 2 * KV_WIDTH), lambda i, j: (i, 0)),
        pl.BlockSpec((tm, CONV_CH), lambda i, j: (i, 0)),
        st_spec,
    ]
    cast_in, cast_out, cast_shapes = _cast_specs(cast_weights, nt, lambda i, j: i)
    return pl.pallas_call(
        functools.partial(_inproj_kernel, seq_tiles=seq_tiles, q_dtype=q_dtype, n_cast=len(cast_weights)),
        grid=(nt, IN_STEPS),
        in_specs=in_specs + cast_in,
        out_specs=out_specs + cast_out,
        out_shape=out_shape + cast_shapes,
        scratch_shapes=scratch,
        compiler_params=pltpu.CompilerParams(
            dimension_semantics=("arbitrary", "arbitrary"), vmem_limit_bytes=VMEM_LIMIT),
        name="inproj_carry" if carry_mode else "inproj_seqs",
    )(*args, *[w for w, _ in cast_weights])


def _softmax_sink(s, sink):
    m = jnp.maximum(jnp.max(s, axis=-1, keepdims=True), sink)
    p = jnp.exp(s - m)
    return p, jnp.sum(p, axis=-1, keepdims=True) + jnp.exp(sink - m)


def _half_lane_operands(slab, low_half):
    lo = lax.broadcasted_iota(jnp.int32, slab.shape, 1) < HEAD_DIM
    moved = pltpu.roll(slab, HEAD_DIM, 1)
    in_low = jnp.where(lo, slab if low_half else moved, 0.0).astype(_BF)
    in_high = jnp.where(lo, 0.0, moved if low_half else slab).astype(_BF)
    return in_low, in_high


def _attn_prompt_kernel(sinks_ref, q_ref, kvc_ref, kvp_ref, g_ref, o_ref, acc_ref, cast_refs):
    i = pl.program_id(1)
    pair_w = 2 * HEAD_DIM
    qi = lax.broadcasted_iota(jnp.int32, (BLOCK, 2 * BLOCK), 0)
    kj = lax.broadcasted_iota(jnp.int32, (BLOCK, 2 * BLOCK), 1)
    dist_i = qi - kj + BLOCK
    in_window = (dist_i >= 0) & (dist_i <= WINDOW)
    far = -NEG / min(_SLOPES)
    dist_mid = jnp.where(in_window, dist_i.astype(_F32), far)
    dist_first = jnp.where(in_window & ((kj >= BLOCK) | (i > 0)), dist_i.astype(_F32), far)
    low = lax.broadcasted_iota(jnp.int32, (BLOCK, pair_w), 1) < HEAD_DIM

    k_ops, v_ops = [], []
    for m in range(N_KV_HEADS // 2):
        ksl = slice(m * pair_w, (m + 1) * pair_w)
        vsl = slice(KV_WIDTH + m * pair_w, KV_WIDTH + (m + 1) * pair_w)
        kslab = jnp.concatenate([kvp_ref[:, ksl], kvc_ref[:, ksl]], axis=0)
        vslab = jnp.concatenate([kvp_ref[:, vsl], kvc_ref[:, vsl]], axis=0)
        for low_half in (True, False):
            k_ops.append(_half_lane_operands(kslab, low_half))
            v_ops.append(_half_lane_operands(vslab, low_half))

    n_pairs = N_HEADS // 2
    work = [(u, c) for u in range(Q_BLOCKS) for c in range(n_pairs)]

    def scores(u, c):
        qc = q_ref[u * BLOCK:(u + 1) * BLOCK, c * pair_w:(c + 1) * pair_w]
        keys = slice(u * BLOCK, (u + 2) * BLOCK)
        k_low, k_high = k_ops[c // 2]
        return _dot_nt(qc, k_low[keys]), _dot_nt(qc, k_high[keys])

    s_next = scores(*work[0])
    for n, (u, c) in enumerate(work):
        s_pair = s_next
        if n + 1 < len(work):
            s_next = scores(*work[n + 1])
        keys = slice(u * BLOCK, (u + 2) * BLOCK)
        v_low, v_high = v_ops[c // 2]
        dist = dist_first if u == 0 else dist_mid
        p0, d0 = _softmax_sink(s_pair[0] - _SLOPES[2 * c] * dist, sinks_ref[0, 2 * c])
        p1, d1 = _softmax_sink(s_pair[1] - _SLOPES[2 * c + 1] * dist, sinks_ref[0, 2 * c + 1])
        o = _dot(p0.astype(_BF), v_low[keys]) + _dot(p1.astype(_BF), v_high[keys])
        acc_ref[u * BLOCK:(u + 1) * BLOCK, c * pair_w:(c + 1) * pair_w] = o / jnp.where(low, d0, d1)
    o_ref[...] = _rms(acc_ref[...], g_ref[...]).astype(_BF)
    _cast_rows(cast_refs)


def _cast_rows(cast_refs):
    n = len(cast_refs) // 2
    for src, dst in zip(cast_refs[:n], cast_refs[n:]):
        if len(dst.shape) == 2:
            dst[...] = src[...].astype(_BF)
        else:
            tw = dst.shape[2]
            for t in range(dst.shape[0]):
                dst[t] = src[:, t * tw:(t + 1) * tw].astype(_BF)


def _cast_specs(weights, steps, index):
    in_specs, out_specs, shapes = [], [], []
    for w, col_tile in weights:
        k, n = w.shape
        rows = k // steps
        assert rows * steps == k and rows % 16 == 0
        in_specs.append(pl.BlockSpec((rows, n), lambda *ids: (index(*ids), 0)))
        if col_tile is None:
            out_specs.append(pl.BlockSpec((rows, n), lambda *ids: (index(*ids), 0)))
            shapes.append(jax.ShapeDtypeStruct((k, n), _BF))
        else:
            out_specs.append(pl.BlockSpec((n // col_tile, rows, col_tile), lambda *ids: (0, index(*ids), 0)))
            shapes.append(jax.ShapeDtypeStruct((n // col_tile, k, col_tile), _BF))
    return in_specs, out_specs, shapes


def _attn_prompt_body(sinks_ref, q_ref, kvc_ref, kvp_ref, g_ref, *rest, n_cast):
    cast_in, (o_ref,), cast_out, (acc_ref,) = (rest[:n_cast], rest[n_cast:n_cast + 1],
                                               rest[n_cast + 1:2 * n_cast + 1], rest[2 * n_cast + 1:])
    _attn_prompt_kernel(sinks_ref, q_ref, kvc_ref, kvp_ref, g_ref, o_ref, acc_ref, cast_in + cast_out)


def _attn_prompt(q, kv, sinks, g, cast_weights, *, batch, seq):
    rows = Q_BLOCKS * BLOCK
    nb = seq // rows
    cast_in, cast_out, cast_shapes = _cast_specs(cast_weights, batch * nb, lambda b, i: b * nb + i)
    return pl.pallas_call(
        functools.partial(_attn_prompt_body, n_cast=len(cast_weights)),
        grid=(batch, nb),
        in_specs=[
            pl.BlockSpec(memory_space=pltpu.SMEM),
            pl.BlockSpec((rows, ATTN_WIDTH), lambda b, i: (b * nb + i, 0)),
            pl.BlockSpec((rows, 2 * KV_WIDTH), lambda b, i: (b * nb + i, 0)),
            pl.BlockSpec((BLOCK, 2 * KV_WIDTH), lambda b, i: (jnp.maximum((b * nb + i) * Q_BLOCKS - 1, 0), 0)),
            pl.BlockSpec((1, ATTN_WIDTH), lambda b, i: (0, 0)),
        ] + cast_in,
        out_specs=[pl.BlockSpec((rows, ATTN_WIDTH), lambda b, i: (b * nb + i, 0))] + cast_out,
        out_shape=[jax.ShapeDtypeStruct((batch * seq, ATTN_WIDTH), _BF)] + cast_shapes,
        scratch_shapes=[pltpu.VMEM((rows, ATTN_WIDTH), _F32)],
        compiler_params=pltpu.CompilerParams(
            dimension_semantics=("arbitrary", "arbitrary"), vmem_limit_bytes=VMEM_LIMIT),
        name="attn_prompt",
    )(sinks, q, kv, kv, g, *[w for w, _ in cast_weights])


def _round_bf16(x):
    return x.astype(_BF).astype(_F32)


def _attn_sample_kernel(sinks_ref, q_ref, kvn_ref, ck_ref, cv_ref, g_ref,
                        o_ref, kwin_ref, vwin_ref, acc_ref, s_scr, *, nseq, t_new):
    rows = GROUP * t_new
    nkeys = WINDOW + t_new
    ksl = [slice(kvh * HEAD_DIM, (kvh + 1) * HEAD_DIM) for kvh in range(N_KV_HEADS)]
    vsl = [slice(KV_WIDTH + kvh * HEAD_DIM, KV_WIDTH + (kvh + 1) * HEAD_DIM) for kvh in range(N_KV_HEADS)]

    for kvh in range(N_KV_HEADS):
        for b in range(nseq):
            rs = slice(b * t_new, (b + 1) * t_new)
            k = _round_bf16(jnp.concatenate([ck_ref[b, :, ksl[kvh]], kvn_ref[rs, ksl[kvh]]], axis=0))
            qs = jnp.concatenate(
                [q_ref[rs, (kvh * GROUP + gq) * HEAD_DIM:(kvh * GROUP + gq + 1) * HEAD_DIM]
                 for gq in range(GROUP)], axis=0)
            s_scr[kvh, b * rows:(b + 1) * rows, :] = _dot_nt(_round_bf16(qs), k)

    r = lax.broadcasted_iota(jnp.int32, (nseq * rows, 1), 0)
    gq_of_row = (r // t_new) % GROUP
    t = lax.broadcasted_iota(jnp.int32, (nseq * rows, nkeys), 0) % t_new
    kj = lax.broadcasted_iota(jnp.int32, (nseq * rows, nkeys), 1)
    dist_i = t + WINDOW - kj
    valid = (dist_i >= 0) & (dist_i <= WINDOW)
    dist = jnp.where(valid, dist_i.astype(_F32), -NEG / min(_SLOPES))
    for kvh in range(N_KV_HEADS):
        slope = jnp.full((nseq * rows, 1), _SLOPES[kvh * GROUP], _F32)
        sink = jnp.full((nseq * rows, 1), sinks_ref[0, kvh * GROUP], _F32)
        for gq in range(1, GROUP):
            slope = jnp.where(gq_of_row == gq, _SLOPES[kvh * GROUP + gq], slope)
            sink = jnp.where(gq_of_row == gq, sinks_ref[0, kvh * GROUP + gq], sink)
        p, denom = _softmax_sink(s_scr[kvh] - slope * dist, sink)
        s_scr[kvh] = _round_bf16(p / denom)

    for kvh in range(N_KV_HEADS):
        for b in range(nseq):
            rs = slice(b * t_new, (b + 1) * t_new)
            v = _round_bf16(jnp.concatenate([cv_ref[b, :, ksl[kvh]], kvn_ref[rs, vsl[kvh]]], axis=0))
            o = _dot(s_scr[kvh, b * rows:(b + 1) * rows, :], v)
            for gq in range(GROUP):
                h = kvh * GROUP + gq
                acc_ref[rs, h * HEAD_DIM:(h + 1) * HEAD_DIM] = o[gq * t_new:(gq + 1) * t_new, :]
    o_ref[...] = _rms(acc_ref[...], g_ref[...]).astype(_BF)
    kwin_ref[:, 0:WINDOW - t_new, :] = ck_ref[:, t_new:, :]
    vwin_ref[:, 0:WINDOW - t_new, :] = cv_ref[:, t_new:, :]
    kwin_ref[:, WINDOW - t_new:, :] = kvn_ref[:, 0:KV_WIDTH].reshape(nseq, t_new, KV_WIDTH)
    vwin_ref[:, WINDOW - t_new:, :] = kvn_ref[:, KV_WIDTH:].reshape(nseq, t_new, KV_WIDTH)


def _attn_sample(q, kv, cache_k, cache_v, sinks, g, *, nseq_total, t_new, nseq=SAMPLE_SEQS):
    rows = nseq * t_new
    cache_spec = pl.BlockSpec((nseq, WINDOW, KV_WIDTH), lambda i: (i, 0, 0))
    return pl.pallas_call(
        functools.partial(_attn_sample_kernel, nseq=nseq, t_new=t_new),
        grid=(nseq_total // nseq,),
        in_specs=[
            pl.BlockSpec(memory_space=pltpu.SMEM),
            pl.BlockSpec((rows, ATTN_WIDTH), lambda i: (i, 0)),
            pl.BlockSpec((rows, 2 * KV_WIDTH), lambda i: (i, 0)),
            cache_spec,
            cache_spec,
            pl.BlockSpec((1, ATTN_WIDTH), lambda i: (0, 0)),
        ],
        out_specs=[pl.BlockSpec((rows, ATTN_WIDTH), lambda i: (i, 0)), cache_spec, cache_spec],
        out_shape=[
            jax.ShapeDtypeStruct((nseq_total * t_new, ATTN_WIDTH), _BF),
            jax.ShapeDtypeStruct(cache_k.shape, _F32),
            jax.ShapeDtypeStruct(cache_v.shape, _F32),
        ],
        scratch_shapes=[pltpu.VMEM((rows, ATTN_WIDTH), _F32),
                        pltpu.VMEM((N_KV_HEADS, nseq * GROUP * t_new, WINDOW + t_new), _F32)],
        compiler_params=pltpu.CompilerParams(dimension_semantics=("arbitrary",), vmem_limit_bytes=VMEM_LIMIT),
        name="attn_sample",
    )(sinks, q, kv, cache_k, cache_v, g)


def _ffn_kernel(*refs, seq_tiles):
    carry_mode = seq_tiles is not None
    if carry_mode:
        (x_ref, na_ref, nb_ref, woa_ref, wob_ref, g_ref, wg_ref, wu_ref, cw_ref, cb_ref, wd_ref, gf_ref,
         o_ref, st_ref, h_scr, acc_scr, ext_scr, carry_scr) = refs
    else:
        (x_ref, na_ref, nb_ref, woa_ref, wob_ref, g_ref, wg_ref, wu_ref, cw_ref, cb_ref, wd_ref, gf_ref, state_ref,
         o_ref, st_ref, h_scr, acc_scr) = refs
    i = pl.program_id(0)
    j = pl.program_id(1)

    @pl.when(j == 0)
    def _():
        x1 = x_ref[...] + _dot(na_ref[...], woa_ref[...]) + _dot(nb_ref[...], wob_ref[...])
        acc_scr[...] = x1
        h_scr[...] = _rms(x1, g_ref[...]).astype(_BF)

    h = h_scr[...]
    gate = _dot(h, wg_ref[...])
    if carry_mode:
        conv = _conv3_carry(gate, cw_ref, ext_scr, carry_scr.at[j], i % seq_tiles == 0)
        st_ref[0] = gate[gate.shape[0] - 2:, :]
    else:
        conv, last2 = _conv3_seqs(gate, cw_ref, state_ref)
        st_ref[...] = last2
    a = conv + cb_ref[...]
    act = ((a / (1.0 + jnp.exp(-a))) * _dot(h, wu_ref[...])).astype(_BF)
    acc_scr[...] += _dot(act, wd_ref[...])

    @pl.when(j == FF_STEPS - 1)
    def _():
        o_ref[...] = _rms(acc_scr[...], gf_ref[...])


def _ffn(x, na, nb, w_out, g, w_gate, w_up, conv_w, conv_b, w_down, g_final, state, *, seq_len):
    n = x.shape[0]
    tm = FF_ROWS
    nt = n // tm
    carry_mode = seq_len >= tm
    seq_tiles = seq_len // tm if carry_mode else None

    def st_map(i, j):
        return (i, 0, j)

    in_specs = [
        pl.BlockSpec((tm, D_MODEL), lambda i, j: (i, 0)),
        pl.BlockSpec((tm, ATTN_WIDTH), lambda i, j: (i, 0)),
        pl.BlockSpec((tm, CONV_CH), lambda i, j: (i, 0)),
        pl.BlockSpec((ATTN_WIDTH, D_MODEL), lambda i, j: (0, 0), pipeline_mode=pl.Buffered(1)),
        pl.BlockSpec((CONV_CH, D_MODEL), lambda i, j: (1, 0), pipeline_mode=pl.Buffered(1)),
        pl.BlockSpec((1, D_MODEL), lambda i, j: (0, 0)),
        pl.BlockSpec((None, D_MODEL, FF_TILE), lambda i, j: (j, 0, 0)),
        pl.BlockSpec((None, D_MODEL, FF_TILE), lambda i, j: (j, 0, 0)),
        pl.BlockSpec((3, FF_TILE), lambda i, j: (0, j)),
        pl.BlockSpec((1, FF_TILE), lambda i, j: (0, j)),
        pl.BlockSpec((FF_TILE, D_MODEL), lambda i, j: (j, 0)),
        pl.BlockSpec((1, D_MODEL), lambda i, j: (0, 0)),
    ]
    args = [x, na, nb, w_out, w_out, g, w_gate, w_up, conv_w, conv_b, w_down, g_final]
    scratch = [pltpu.VMEM((tm, D_MODEL), _BF), pltpu.VMEM((tm, D_MODEL), _F32)]
    if carry_mode:
        st_shape = jax.ShapeDtypeStruct((nt, 2, D_FF), _F32)
        st_spec = pl.BlockSpec((1, 2, FF_TILE), st_map)
        scratch += [pltpu.VMEM((tm + SUBLANES, FF_TILE), _F32),
                    pltpu.VMEM((FF_STEPS, SUBLANES, FF_TILE), _F32)]
    else:
        nseq = tm // seq_len
        st_shape = jax.ShapeDtypeStruct((n // seq_len, 2, D_FF), _F32)
        st_spec = pl.BlockSpec((nseq, 2, FF_TILE), st_map)
        in_specs.append(st_spec)
        args.append(state)
    return pl.pallas_call(
        functools.partial(_ffn_kernel, seq_tiles=seq_tiles),
        grid=(nt, FF_STEPS),
        in_specs=in_specs,
        out_specs=[pl.BlockSpec((tm, D_MODEL), lambda i, j: (i, 0)), st_spec],
        out_shape=[jax.ShapeDtypeStruct((n, D_MODEL), _F32), st_shape],
        scratch_shapes=scratch,
        compiler_params=pltpu.CompilerParams(
            dimension_semantics=("arbitrary", "arbitrary"), vmem_limit_bytes=VMEM_LIMIT),
        name="ffn_carry" if carry_mode else "ffn_seqs",
    )(*args)


def _tile_cast_kernel(w_ref, o_ref):
    o_ref[...] = w_ref[...].astype(_BF)


def _tile_cast(w, col_tile):
    k, n = w.shape
    return pl.pallas_call(
        _tile_cast_kernel,
        grid=(n // col_tile,),
        in_specs=[pl.BlockSpec((k, col_tile), lambda t: (0, t))],
        out_specs=pl.BlockSpec((None, k, col_tile), lambda t: (t, 0, 0)),
        out_shape=jax.ShapeDtypeStruct((n // col_tile, k, col_tile), _BF),
        compiler_params=pltpu.CompilerParams(dimension_semantics=("arbitrary",)),
        name="tile_cast",
    )(w)


def kernel(x_prompt, x_sample, cache_k_window, cache_v_window, state_conv, state_ffn_conv, g_attn_norm, w_in, attn_sinks, conv_w, g_out_attn, g_out_conv, w_out, g_ffn_norm, w_gate, w_up, ffn_conv_w, ffn_conv_b, w_down, g_final):
    assert w_in.shape[0] == 1, "single layer"
    bp, sp, _ = x_prompt.shape
    bs, ts, _ = x_sample.shape
    assert sp % IN_ROWS == 0 and (bs * ts) % IN_ROWS == 0 and ts == SUBLANES

    w_in_b = _tile_cast(w_in[0], COL_TILE)
    g_attn = g_attn_norm[0].reshape(1, D_MODEL)
    g_ffn = g_ffn_norm[0].reshape(1, D_MODEL)
    g_fin = g_final.reshape(1, D_MODEL)
    g_oa = g_out_attn[0].reshape(1, ATTN_WIDTH)
    g_oc = g_out_conv[0].reshape(1, CONV_CH)
    sinks = attn_sinks[0].reshape(1, N_HEADS)
    cw = conv_w[0]
    fcw = ffn_conv_w[0]
    fcb = ffn_conv_b[0].reshape(1, D_FF)

    xp = x_prompt.reshape(bp * sp, D_MODEL)
    q_p, kv_p, nb_p, cst_p, w_out_b, w_down_b = _inproj(
        xp, g_attn, w_in_b, cw, g_oc, None, [(w_out[0], None), (w_down[0], None)], seq_len=sp, q_dtype=_BF)
    na_p, w_gate_b, w_up_b = _attn_prompt(
        q_p, kv_p, sinks, g_oa, [(w_gate[0], FF_TILE), (w_up[0], FF_TILE)], batch=bp, seq=sp)
    y_p, fst_p = _ffn(xp, na_p, nb_p, w_out_b, g_ffn, w_gate_b, w_up_b, fcw, fcb, w_down_b, g_fin, None, seq_len=sp)

    seq_tiles = sp // min(IN_ROWS, sp)
    ff_seq_tiles = sp // FF_ROWS
    kv_p3 = kv_p.reshape(bp, sp, 2 * KV_WIDTH)[:, sp - WINDOW:, :]
    k_win_p = kv_p3[:, :, :KV_WIDTH].reshape(1, bp, WINDOW, N_KV_HEADS, HEAD_DIM)
    v_win_p = kv_p3[:, :, KV_WIDTH:].reshape(1, bp, WINDOW, N_KV_HEADS, HEAD_DIM)
    conv_p = cst_p[seq_tiles - 1::seq_tiles][None]
    ffn_conv_p = fst_p[ff_seq_tiles - 1::ff_seq_tiles][None]

    xs = x_sample.reshape(bs * ts, D_MODEL)
    q_s, kv_s, nb_s, cst_s = _inproj(xs, g_attn, w_in_b, cw, g_oc, state_conv[0], [], seq_len=ts, q_dtype=_F32)
    ck = cache_k_window[0].reshape(bs, WINDOW, KV_WIDTH)
    cv = cache_v_window[0].reshape(bs, WINDOW, KV_WIDTH)
    na_s, k_win_s, v_win_s = _attn_sample(q_s, kv_s, ck, cv, sinks, g_oa, nseq_total=bs, t_new=ts)
    y_s, fst_s = _ffn(xs, na_s, nb_s, w_out_b, g_ffn, w_gate_b, w_up_b, fcw, fcb, w_down_b, g_fin,
                      state_ffn_conv[0], seq_len=ts)

    win_shape = (1, bs, WINDOW, N_KV_HEADS, HEAD_DIM)
    return (y_p.reshape(bp, sp, D_MODEL), y_s.reshape(bs, ts, D_MODEL),
            k_win_p, v_win_p, conv_p, ffn_conv_p,
            k_win_s.reshape(win_shape), v_win_s.reshape(win_shape), cst_s[None], fst_s[None])
```

```python
import functools

import jax
import jax.numpy as jnp
from jax import lax
from jax.experimental import pallas as pl
from jax.experimental.pallas import tpu as pltpu

D_MODEL = 2048
ATTN_WIDTH = 1024
CONV_CH = 1024
HEAD_DIM = 64
N_HEADS = 16
N_KV_HEADS = 4
GROUP = 4
KV_WIDTH = 256
WINDOW = 128
BLOCK = 128
D_FF = 5632
EPS = 1e-6
NEG = -1e30

SUBLANES = 8
BF16_ROWS = 16
Q_BLOCKS = 4
SAMPLE_SEQS = 32
IN_ROWS = 512
COL_TILE = 512
assert ATTN_WIDTH == 2 * COL_TILE and 2 * KV_WIDTH == COL_TILE
CONV_STEP0 = 1
CONV_STEPS = CONV_CH // COL_TILE
IN_STEPS = CONV_STEP0 + CONV_STEPS
FF_ROWS = 512
FF_TILE = 512
FF_STEPS = D_FF // FF_TILE
B_BLK0 = (ATTN_WIDTH + 2 * KV_WIDTH) // COL_TILE
C_BLK0 = B_BLK0 + CONV_STEPS
U_BLK0 = C_BLK0 + CONV_STEPS
VMEM_LIMIT = 56 * 1024 * 1024

_SLOPES = [2.0 ** (-8.0 * (h + 1) / N_HEADS) for h in range(N_HEADS)]
_BF = jnp.bfloat16
_F32 = jnp.float32


def _rms(x, g):
    return x * lax.rsqrt(jnp.mean(x * x, axis=-1, keepdims=True) + EPS) * g


def _dot(a, b):
    return jnp.dot(a, b, preferred_element_type=_F32)


def _dot_nt(a, b):
    return lax.dot_general(a, b, (((1,), (1,)), ((), ())), preferred_element_type=_F32)


def _conv3_carry(cur, w_ref, ext_ref, carry_ref, is_seq_start):
    rows = cur.shape[0]
    prev = jnp.where(is_seq_start, 0.0, carry_ref[...])
    ext_ref[0:SUBLANES, :] = prev
    ext_ref[SUBLANES:SUBLANES + rows, :] = cur
    carry_ref[...] = cur[rows - SUBLANES:, :]
    p1 = ext_ref[SUBLANES - 1:SUBLANES - 1 + rows, :]
    p2 = ext_ref[SUBLANES - 2:SUBLANES - 2 + rows, :]
    return w_ref[2:3, :] * cur + w_ref[1:2, :] * p1 + w_ref[0:1, :] * p2


def _conv3_seqs(cur, w_ref, state_ref):
    rows, ch = cur.shape
    nseq = rows // SUBLANES
    cur3 = cur.reshape(nseq, SUBLANES, ch)
    st = state_ref[...]
    t = lax.broadcasted_iota(jnp.int32, cur3.shape, 1)
    s0 = jnp.broadcast_to(st[:, 0:1, :], cur3.shape)
    s1 = jnp.broadcast_to(st[:, 1:2, :], cur3.shape)
    p1 = jnp.where(t == 0, s1, pltpu.roll(cur3, 1, 1))
    p2 = jnp.where(t == 0, s0, jnp.where(t == 1, s1, pltpu.roll(cur3, 2, 1)))
    w = w_ref[...]
    out = w[2:3, :][None] * cur3 + w[1:2, :][None] * p1 + w[0:1, :][None] * p2
    return out.reshape(rows, ch), cur3[:, SUBLANES - 2:, :]


def _inproj_kernel(*refs, seq_tiles, q_dtype, n_cast):
    carry_mode = seq_tiles is not None
    n_in = 7 if carry_mode else 8
    ins, rest = refs[:n_in], refs[n_in:]
    cast_in, outs, cast_out, scr = (rest[:n_cast], rest[n_cast:n_cast + 4],
                                    rest[n_cast + 4:2 * n_cast + 4], rest[2 * n_cast + 4:])
    q_ref, kv_ref, nb_ref, st_ref = outs
    if carry_mode:
        x_ref, g_ref, wa_ref, wb_ref, wc_ref, cw_ref, gc_ref = ins
        h_scr, y_scr, ss_scr, ext_scr, carry_scr = scr
    else:
        x_ref, g_ref, wa_ref, wb_ref, wc_ref, cw_ref, gc_ref, state_ref = ins
        h_scr, y_scr, ss_scr = scr
    i = pl.program_id(0)
    j = pl.program_id(1)

    @pl.when(j == 0)
    def _():
        _cast_rows(cast_in + cast_out)
        h = _rms(x_ref[...], g_ref[...]).astype(_BF)
        h_scr[...] = h
        q_ref[:, :COL_TILE] = (_dot(h, wa_ref[...]) * (HEAD_DIM ** -0.5)).astype(q_dtype)
        q_ref[:, COL_TILE:] = (_dot(h, wb_ref[...]) * (HEAD_DIM ** -0.5)).astype(q_dtype)
        kv_ref[...] = _dot(h, wc_ref[...])

    for c in range(CONV_STEPS):
        @pl.when(j == CONV_STEP0 + c)
        def _(c=c):
            h = h_scr[...]
            cu = _dot(h, wb_ref[...]) * _dot(h, wc_ref[...])
            if carry_mode:
                conv = _conv3_carry(cu, cw_ref, ext_scr, carry_scr.at[c], i % seq_tiles == 0)
                st_ref[0] = cu[cu.shape[0] - 2:, :]
            else:
                conv, last2 = _conv3_seqs(cu, cw_ref, state_ref)
                st_ref[...] = last2
            y = _dot(h, wa_ref[...]) * conv
            y_scr[c] = y
            ss = jnp.sum(y * y, axis=-1, keepdims=True)
            if c == 0:
                ss_scr[...] = ss
            else:
                ss_scr[...] += ss
            if c == CONV_STEPS - 1:
                inv = lax.rsqrt(ss_scr[...] * (1.0 / CONV_CH) + EPS)
                for cc in range(CONV_STEPS):
                    sl = slice(cc * COL_TILE, (cc + 1) * COL_TILE)
                    nb_ref[:, sl] = (y_scr[cc] * inv * gc_ref[:, sl]).astype(_BF)


def _inproj(x, g, w_in, conv_w, g_conv, state, cast_weights, *, seq_len, q_dtype):
    n = x.shape[0]
    tm = IN_ROWS
    nt = n // tm
    carry_mode = seq_len >= tm
    seq_tiles = seq_len // tm if carry_mode else None

    def conv_col(j):
        return jnp.maximum(j - CONV_STEP0, 0)

    def w_map(first, slab0):
        return lambda i, j: (jnp.where(j == 0, first, slab0 + conv_col(j)), 0, 0)

    def st_map(i, j):
        return (i, 0, conv_col(j))

    in_specs = [
        pl.BlockSpec((tm, D_MODEL), lambda i, j: (i, 0)),
        pl.BlockSpec((1, D_MODEL), lambda i, j: (0, 0)),
        pl.BlockSpec((None, D_MODEL, COL_TILE), w_map(0, B_BLK0)),
        pl.BlockSpec((None, D_MODEL, COL_TILE), w_map(1, C_BLK0)),
        pl.BlockSpec((None, D_MODEL, COL_TILE), w_map(2, U_BLK0)),
        pl.BlockSpec((3, COL_TILE), lambda i, j: (0, conv_col(j))),
        pl.BlockSpec((1, CONV_CH), lambda i, j: (0, 0)),
    ]
    args = [x, g, w_in, w_in, w_in, conv_w, g_conv]
    scratch = [
        pltpu.VMEM((tm, D_MODEL), _BF),
        pltpu.VMEM((CONV_STEPS, tm, COL_TILE), _F32),
        pltpu.VMEM((tm, 1), _F32),
    ]
    if carry_mode:
        st_shape = jax.ShapeDtypeStruct((nt, 2, CONV_CH), _F32)
        st_spec = pl.BlockSpec((1, 2, COL_TILE), st_map)
        scratch += [pltpu.VMEM((tm + SUBLANES, COL_TILE), _F32),
                    pltpu.VMEM((CONV_STEPS, SUBLANES, COL_TILE), _F32)]
    else:
        nseq = tm // seq_len
        st_shape = jax.ShapeDtypeStruct((n // seq_len, 2, CONV_CH), _F32)
        st_spec = pl.BlockSpec((nseq, 2, COL_TILE), st_map)
        in_specs.append(st_spec)
        args.append(state)
    out_shape = [
        jax.ShapeDtypeStruct((n, ATTN_WIDTH), q_dtype),
        jax.ShapeDtypeStruct((n, 2 * KV_WIDTH), _F32),
        jax.ShapeDtypeStruct((n, CONV_CH), _BF),
        st_shape,
    ]
    out_specs = [
        pl.BlockSpec((tm, ATTN_WIDTH), lambda i, j: (i, 0)),
        pl.BlockSpec((tm, 2 * KV_WIDTH), lambda i, j: (i, 0)),
        pl.BlockSpec((tm, CONV_CH), lambda i, j: (i, 0)),
        st_spec,
    ]
    cast_in, cast_out, cast_shapes = _cast_specs(cast_weights, nt, lambda i, j: i)
    return pl.pallas_call(
        functools.partial(_inproj_kernel, seq_tiles=seq_tiles, q_dtype=q_dtype, n_cast=len(cast_weights)),
        grid=(nt, IN_STEPS),
        in_specs=in_specs + cast_in,
        out_specs=out_specs + cast_out,
        out_shape=out_shape + cast_shapes,
        scratch_shapes=scratch,
        compiler_params=pltpu.CompilerParams(
            dimension_semantics=("arbitrary", "arbitrary"), vmem_limit_bytes=VMEM_LIMIT),
        name="inproj_carry" if carry_mode else "inproj_seqs",
    )(*args, *[w for w, _ in cast_weights])


def _softmax_sink(s, sink):
    m = jnp.maximum(jnp.max(s, axis=-1, keepdims=True), sink)
    p = jnp.exp(s - m)
    return p, jnp.sum(p, axis=-1, keepdims=True) + jnp.exp(sink - m)


def _half_lane_operands(slab, low_half):
    lo = lax.broadcasted_iota(jnp.int32, slab.shape, 1) < HEAD_DIM
    moved = pltpu.roll(slab, HEAD_DIM, 1)
    in_low = jnp.where(lo, slab if low_half else moved, 0.0).astype(_BF)
    in_high = jnp.where(lo, 0.0, moved if low_half else slab).astype(_BF)
    return in_low, in_high


def _attn_prompt_kernel(sinks_ref, q_ref, kvc_ref, kvp_ref, g_ref, o_ref, acc_ref, cast_refs):
    i = pl.program_id(1)
    pair_w = 2 * HEAD_DIM
    qi = lax.broadcasted_iota(jnp.int32, (BLOCK, 2 * BLOCK), 0)
    kj = lax.broadcasted_iota(jnp.int32, (BLOCK, 2 * BLOCK), 1)
    dist_i = qi - kj + BLOCK
    in_window = (dist_i >= 0) & (dist_i <= WINDOW)
    far = -NEG / min(_SLOPES)
    dist_mid = jnp.where(in_window, dist_i.astype(_F32), far)
    dist_first = jnp.where(in_window & ((kj >= BLOCK) | (i > 0)), dist_i.astype(_F32), far)
    low = lax.broadcasted_iota(jnp.int32, (BLOCK, pair_w), 1) < HEAD_DIM

    k_ops, v_ops = [], []
    for m in range(N_KV_HEADS // 2):
        ksl = slice(m * pair_w, (m + 1) * pair_w)
        vsl = slice(KV_WIDTH + m * pair_w, KV_WIDTH + (m + 1) * pair_w)
        kslab = jnp.concatenate([kvp_ref[:, ksl], kvc_ref[:, ksl]], axis=0)
        vslab = jnp.concatenate([kvp_ref[:, vsl], kvc_ref[:, vsl]], axis=0)
        for low_half in (True, False):
            k_ops.append(_half_lane_operands(kslab, low_half))
            v_ops.append(_half_lane_operands(vslab, low_half))

    n_pairs = N_HEADS // 2
    work = [(u, c) for u in range(Q_BLOCKS) for c in range(n_pairs)]

    def scores(u, c):
        qc = q_ref[u * BLOCK:(u + 1) * BLOCK, c * pair_w:(c + 1) * pair_w]
        keys = slice(u * BLOCK, (u + 2) * BLOCK)
        k_low, k_high = k_ops[c // 2]
        return _dot_nt(qc, k_low[keys]), _dot_nt(qc, k_high[keys])

    s_next = scores(*work[0])
    for n, (u, c) in enumerate(work):
        s_pair = s_next
        if n + 1 < len(work):
            s_next = scores(*work[n + 1])
        keys = slice(u * BLOCK, (u + 2) * BLOCK)
        v_low, v_high = v_ops[c // 2]
        dist = dist_first if u == 0 else dist_mid
        p0, d0 = _softmax_sink(s_pair[0] - _SLOPES[2 * c] * dist, sinks_ref[0, 2 * c])
        p1, d1 = _softmax_sink(s_pair[1] - _SLOPES[2 * c + 1] * dist, sinks_ref[0, 2 * c + 1])
        o = _dot(p0.astype(_BF), v_low[keys]) + _dot(p1.astype(_BF), v_high[keys])
        acc_ref[u * BLOCK:(u + 1) * BLOCK, c * pair_w:(c + 1) * pair_w] = o / jnp.where(low, d0, d1)
    o_ref[...] = _rms(acc_ref[...], g_ref[...]).astype(_BF)
    _cast_rows(cast_refs)


def _cast_rows(cast_refs):
    n = len(cast_refs) // 2
    for src, dst in zip(cast_refs[:n], cast_refs[n:]):
        if len(dst.shape) == 2:
            dst[...] = src[...].astype(_BF)
        else:
            tw = dst.shape[2]
            for t in range(dst.shape[0]):
                dst[t] = src[:, t * tw:(t + 1) * tw].astype(_BF)


def _cast_specs(weights, steps, index):
    in_specs, out_specs, shapes = [], [], []
    for w, col_tile in weights:
        k, n = w.shape
        rows = k // steps
        assert rows * steps == k and rows % BF16_ROWS == 0
        in_specs.append(pl.BlockSpec((rows, n), lambda *ids: (index(*ids), 0)))
        if col_tile is None:
            out_specs.append(pl.BlockSpec((rows, n), lambda *ids: (index(*ids), 0)))
            shapes.append(jax.ShapeDtypeStruct((k, n), _BF))
        else:
            out_specs.append(pl.BlockSpec((n // col_tile, rows, col_tile), lambda *ids: (0, index(*ids), 0)))
            shapes.append(jax.ShapeDtypeStruct((n // col_tile, k, col_tile), _BF))
    return in_specs, out_specs, shapes


def _attn_prompt_body(sinks_ref, q_ref, kvc_ref, kvp_ref, g_ref, *rest, n_cast):
    cast_in, (o_ref,), cast_out, (acc_ref,) = (rest[:n_cast], rest[n_cast:n_cast + 1],
                                               rest[n_cast + 1:2 * n_cast + 1], rest[2 * n_cast + 1:])
    _attn_prompt_kernel(sinks_ref, q_ref, kvc_ref, kvp_ref, g_ref, o_ref, acc_ref, cast_in + cast_out)


def _attn_prompt(q, kv, sinks, g, cast_weights, *, batch, seq):
    rows = Q_BLOCKS * BLOCK
    nb = seq // rows
    cast_in, cast_out, cast_shapes = _cast_specs(cast_weights, batch * nb, lambda b, i: b * nb + i)
    return pl.pallas_call(
        functools.partial(_attn_prompt_body, n_cast=len(cast_weights)),
        grid=(batch, nb),
        in_specs=[
            pl.BlockSpec(memory_space=pltpu.SMEM),
            pl.BlockSpec((rows, ATTN_WIDTH), lambda b, i: (b * nb + i, 0)),
            pl.BlockSpec((rows, 2 * KV_WIDTH), lambda b, i: (b * nb + i, 0)),
            pl.BlockSpec((BLOCK, 2 * KV_WIDTH), lambda b, i: (jnp.maximum((b * nb + i) * Q_BLOCKS - 1, 0), 0)),
            pl.BlockSpec((1, ATTN_WIDTH), lambda b, i: (0, 0)),
        ] + cast_in,
        out_specs=[pl.BlockSpec((rows, ATTN_WIDTH), lambda b, i: (b * nb + i, 0))] + cast_out,
        out_shape=[jax.ShapeDtypeStruct((batch * seq, ATTN_WIDTH), _BF)] + cast_shapes,
        scratch_shapes=[pltpu.VMEM((rows, ATTN_WIDTH), _F32)],
        compiler_params=pltpu.CompilerParams(
            dimension_semantics=("arbitrary", "arbitrary"), vmem_limit_bytes=VMEM_LIMIT),
        name="attn_prompt",
    )(sinks, q, kv, kv, g, *[w for w, _ in cast_weights])


def _round_bf16(x):
    return x.astype(_BF).astype(_F32)


def _attn_sample_kernel(sinks_ref, q_ref, kvn_ref, ck_ref, cv_ref, g_ref,
                        o_ref, kwin_ref, vwin_ref, acc_ref, s_scr, *, nseq, t_new):
    rows = GROUP * t_new
    nkeys = WINDOW + t_new
    ksl = [slice(kvh * HEAD_DIM, (kvh + 1) * HEAD_DIM) for kvh in range(N_KV_HEADS)]
    vsl = [slice(KV_WIDTH + kvh * HEAD_DIM, KV_WIDTH + (kvh + 1) * HEAD_DIM) for kvh in range(N_KV_HEADS)]

    for kvh in range(N_KV_HEADS):
        for b in range(nseq):
            rs = slice(b * t_new, (b + 1) * t_new)
            k = _round_bf16(jnp.concatenate([ck_ref[b, :, ksl[kvh]], kvn_ref[rs, ksl[kvh]]], axis=0))
            qs = jnp.concatenate(
                [q_ref[rs, (kvh * GROUP + gq) * HEAD_DIM:(kvh * GROUP + gq + 1) * HEAD_DIM]
                 for gq in range(GROUP)], axis=0)
            s_scr[kvh, b * rows:(b + 1) * rows, :] = _dot_nt(_round_bf16(qs), k)

    r = lax.broadcasted_iota(jnp.int32, (nseq * rows, 1), 0)
    gq_of_row = (r // t_new) % GROUP
    t = lax.broadcasted_iota(jnp.int32, (nseq * rows, nkeys), 0) % t_new
    kj = lax.broadcasted_iota(jnp.int32, (nseq * rows, nkeys), 1)
    dist_i = t + WINDOW - kj
    valid = (dist_i >= 0) & (dist_i <= WINDOW)
    dist = jnp.where(valid, dist_i.astype(_F32), -NEG / min(_SLOPES))
    for kvh in range(N_KV_HEADS):
        slope = jnp.full((nseq * rows, 1), _SLOPES[kvh * GROUP], _F32)
        sink = jnp.full((nseq * rows, 1), sinks_ref[0, kvh * GROUP], _F32)
        for gq in range(1, GROUP):
            slope = jnp.where(gq_of_row == gq, _SLOPES[kvh * GROUP + gq], slope)
            sink = jnp.where(gq_of_row == gq, sinks_ref[0, kvh * GROUP + gq], sink)
        p, denom = _softmax_sink(s_scr[kvh] - slope * dist, sink)
        s_scr[kvh] = _round_bf16(p / denom)

    for kvh in range(N_KV_HEADS):
        for b in range(nseq):
            rs = slice(b * t_new, (b + 1) * t_new)
            v = _round_bf16(jnp.concatenate([cv_ref[b, :, ksl[kvh]], kvn_ref[rs, vsl[kvh]]], axis=0))
            o = _dot(s_scr[kvh, b * rows:(b + 1) * rows, :], v)
            for gq in range(GROUP):
                h = kvh * GROUP + gq
                acc_ref[rs, h * HEAD_DIM:(h + 1) * HEAD_DIM] = o[gq * t_new:(gq + 1) * t_new, :]
    o_ref[...] = _rms(acc_ref[...], g_ref[...]).astype(_BF)
    kwin_ref[:, 0:WINDOW - t_new, :] = ck_ref[:, t_new:, :]
    vwin_ref[:, 0:WINDOW - t_new, :] = cv_ref[:, t_new:, :]
    kwin_ref[:, WINDOW - t_new:, :] = kvn_ref[:, 0:KV_WIDTH].reshape(nseq, t_new, KV_WIDTH)
    vwin_ref[:, WINDOW - t_new:, :] = kvn_ref[:, KV_WIDTH:].reshape(nseq, t_new, KV_WIDTH)


def _attn_sample(q, kv, cache_k, cache_v, sinks, g, *, nseq_total, t_new, nseq=SAMPLE_SEQS):
    rows = nseq * t_new
    cache_spec = pl.BlockSpec((nseq, WINDOW, KV_WIDTH), lambda i: (i, 0, 0))
    return pl.pallas_call(
        functools.partial(_attn_sample_kernel, nseq=nseq, t_new=t_new),
        grid=(nseq_total // nseq,),
        in_specs=[
            pl.BlockSpec(memory_space=pltpu.SMEM),
            pl.BlockSpec((rows, ATTN_WIDTH), lambda i: (i, 0)),
            pl.BlockSpec((rows, 2 * KV_WIDTH), lambda i: (i, 0)),
            cache_spec,
            cache_spec,
            pl.BlockSpec((1, ATTN_WIDTH), lambda i: (0, 0)),
        ],
        out_specs=[pl.BlockSpec((rows, ATTN_WIDTH), lambda i: (i, 0)), cache_spec, cache_spec],
        out_shape=[
            jax.ShapeDtypeStruct((nseq_total * t_new, ATTN_WIDTH), _BF),
            jax.ShapeDtypeStruct(cache_k.shape, _F32),
            jax.ShapeDtypeStruct(cache_v.shape, _F32),
        ],
        scratch_shapes=[pltpu.VMEM((rows, ATTN_WIDTH), _F32),
                        pltpu.VMEM((N_KV_HEADS, nseq * GROUP * t_new, WINDOW + t_new), _F32)],
        compiler_params=pltpu.CompilerParams(dimension_semantics=("arbitrary",), vmem_limit_bytes=VMEM_LIMIT),
        name="attn_sample",
    )(sinks, q, kv, cache_k, cache_v, g)


def _ffn_kernel(*refs, seq_tiles):
    carry_mode = seq_tiles is not None
    if carry_mode:
        (x_ref, na_ref, nb_ref, woa_ref, wob_ref, g_ref, wg_ref, wu_ref, cw_ref, cb_ref, wd_ref, gf_ref,
         o_ref, st_ref, h_scr, acc_scr, ext_scr, carry_scr) = refs
    else:
        (x_ref, na_ref, nb_ref, woa_ref, wob_ref, g_ref, wg_ref, wu_ref, cw_ref, cb_ref, wd_ref, gf_ref, state_ref,
         o_ref, st_ref, h_scr, acc_scr) = refs
    i = pl.program_id(0)
    j = pl.program_id(1)

    @pl.when(j == 0)
    def _():
        x1 = x_ref[...] + _dot(na_ref[...], woa_ref[...]) + _dot(nb_ref[...], wob_ref[...])
        acc_scr[...] = x1
        h_scr[...] = _rms(x1, g_ref[...]).astype(_BF)

    h = h_scr[...]
    gate = _dot(h, wg_ref[...])
    if carry_mode:
        conv = _conv3_carry(gate, cw_ref, ext_scr, carry_scr.at[j], i % seq_tiles == 0)
        st_ref[0] = gate[gate.shape[0] - 2:, :]
    else:
        conv, last2 = _conv3_seqs(gate, cw_ref, state_ref)
        st_ref[...] = last2
    a = conv + cb_ref[...]
    act = ((a / (1.0 + jnp.exp(-a))) * _dot(h, wu_ref[...])).astype(_BF)
    acc_scr[...] += _dot(act, wd_ref[...])

    @pl.when(j == FF_STEPS - 1)
    def _():
        o_ref[...] = _rms(acc_scr[...], gf_ref[...])


def _ffn(x, na, nb, w_out, g, w_gate, w_up, conv_w, conv_b, w_down, g_final, state, *, seq_len):
    n = x.shape[0]
    tm = FF_ROWS
    nt = n // tm
    carry_mode = seq_len >= tm
    seq_tiles = seq_len // tm if carry_mode else None

    def st_map(i, j):
        return (i, 0, j)

    in_specs = [
        pl.BlockSpec((tm, D_MODEL), lambda i, j: (i, 0)),
        pl.BlockSpec((tm, ATTN_WIDTH), lambda i, j: (i, 0)),
        pl.BlockSpec((tm, CONV_CH), lambda i, j: (i, 0)),
        pl.BlockSpec((ATTN_WIDTH, D_MODEL), lambda i, j: (0, 0), pipeline_mode=pl.Buffered(1)),
        pl.BlockSpec((CONV_CH, D_MODEL), lambda i, j: (1, 0), pipeline_mode=pl.Buffered(1)),
        pl.BlockSpec((1, D_MODEL), lambda i, j: (0, 0)),
        pl.BlockSpec((None, D_MODEL, FF_TILE), lambda i, j: (j, 0, 0)),
        pl.BlockSpec((None, D_MODEL, FF_TILE), lambda i, j: (j, 0, 0)),
        pl.BlockSpec((3, FF_TILE), lambda i, j: (0, j)),
        pl.BlockSpec((1, FF_TILE), lambda i, j: (0, j)),
        pl.BlockSpec((FF_TILE, D_MODEL), lambda i, j: (j, 0)),
        pl.BlockSpec((1, D_MODEL), lambda i, j: (0, 0)),
    ]
    args = [x, na, nb, w_out, w_out, g, w_gate, w_up, conv_w, conv_b, w_down, g_final]
    scratch = [pltpu.VMEM((tm, D_MODEL), _BF), pltpu.VMEM((tm, D_MODEL), _F32)]
    if carry_mode:
        st_shape = jax.ShapeDtypeStruct((nt, 2, D_FF), _F32)
        st_spec = pl.BlockSpec((1, 2, FF_TILE), st_map)
        scratch += [pltpu.VMEM((tm + SUBLANES, FF_TILE), _F32),
                    pltpu.VMEM((FF_STEPS, SUBLANES, FF_TILE), _F32)]
    else:
        nseq = tm // seq_len
        st_shape = jax.ShapeDtypeStruct((n // seq_len, 2, D_FF), _F32)
        st_spec = pl.BlockSpec((nseq, 2, FF_TILE), st_map)
        in_specs.append(st_spec)
        args.append(state)
    return pl.pallas_call(
        functools.partial(_ffn_kernel, seq_tiles=seq_tiles),
        grid=(nt, FF_STEPS),
        in_specs=in_specs,
        out_specs=[pl.BlockSpec((tm, D_MODEL), lambda i, j: (i, 0)), st_spec],
        out_shape=[jax.ShapeDtypeStruct((n, D_MODEL), _F32), st_shape],
        scratch_shapes=scratch,
        compiler_params=pltpu.CompilerParams(
            dimension_semantics=("arbitrary", "arbitrary"), vmem_limit_bytes=VMEM_LIMIT),
        name="ffn_carry" if carry_mode else "ffn_seqs",
    )(*args)


def _tile_cast_kernel(w_ref, o_ref):
    o_ref[...] = w_ref[...].astype(_BF)


def _tile_cast(w, col_tile):
    k, n = w.shape
    return pl.pallas_call(
        _tile_cast_kernel,
        grid=(n // col_tile,),
        in_specs=[pl.BlockSpec((k, col_tile), lambda t: (0, t))],
        out_specs=pl.BlockSpec((None, k, col_tile), lambda t: (t, 0, 0)),
        out_shape=jax.ShapeDtypeStruct((n // col_tile, k, col_tile), _BF),
        compiler_params=pltpu.CompilerParams(dimension_semantics=("arbitrary",)),
        name="tile_cast",
    )(w)


def kernel(x_prompt, x_sample, cache_k_window, cache_v_window, state_conv, state_ffn_conv, g_attn_norm, w_in, attn_sinks, conv_w, g_out_attn, g_out_conv, w_out, g_ffn_norm, w_gate, w_up, ffn_conv_w, ffn_conv_b, w_down, g_final):
    assert w_in.shape[0] == 1, "single layer"
    bp, sp, _ = x_prompt.shape
    bs, ts, _ = x_sample.shape
    assert sp % IN_ROWS == 0 and (bs * ts) % IN_ROWS == 0 and ts == SUBLANES

    w_in_b = _tile_cast(w_in[0], COL_TILE)
    g_attn = g_attn_norm[0].reshape(1, D_MODEL)
    g_ffn = g_ffn_norm[0].reshape(1, D_MODEL)
    g_fin = g_final.reshape(1, D_MODEL)
    g_oa = g_out_attn[0].reshape(1, ATTN_WIDTH)
    g_oc = g_out_conv[0].reshape(1, CONV_CH)
    sinks = attn_sinks[0].reshape(1, N_HEADS)
    cw = conv_w[0]
    fcw = ffn_conv_w[0]
    fcb = ffn_conv_b[0].reshape(1, D_FF)

    xp = x_prompt.reshape(bp * sp, D_MODEL)
    q_p, kv_p, nb_p, cst_p, w_out_b, w_down_b = _inproj(
        xp, g_attn, w_in_b, cw, g_oc, None, [(w_out[0], None), (w_down[0], None)], seq_len=sp, q_dtype=_BF)
    na_p, w_gate_b, w_up_b = _attn_prompt(
        q_p, kv_p, sinks, g_oa, [(w_gate[0], FF_TILE), (w_up[0], FF_TILE)], batch=bp, seq=sp)
    y_p, fst_p = _ffn(xp, na_p, nb_p, w_out_b, g_ffn, w_gate_b, w_up_b, fcw, fcb, w_down_b, g_fin, None, seq_len=sp)

    seq_tiles = sp // min(IN_ROWS, sp)
    ff_seq_tiles = sp // FF_ROWS
    kv_p3 = kv_p.reshape(bp, sp, 2 * KV_WIDTH)[:, sp - WINDOW:, :]
    k_win_p = kv_p3[:, :, :KV_WIDTH].reshape(1, bp, WINDOW, N_KV_HEADS, HEAD_DIM)
    v_win_p = kv_p3[:, :, KV_WIDTH:].reshape(1, bp, WINDOW, N_KV_HEADS, HEAD_DIM)
    conv_p = cst_p[seq_tiles - 1::seq_tiles][None]
    ffn_conv_p = fst_p[ff_seq_tiles - 1::ff_seq_tiles][None]

    xs = x_sample.reshape(bs * ts, D_MODEL)
    q_s, kv_s, nb_s, cst_s = _inproj(xs, g_attn, w_in_b, cw, g_oc, state_conv[0], [], seq_len=ts, q_dtype=_F32)
    ck = cache_k_window[0].reshape(bs, WINDOW, KV_WIDTH)
    cv = cache_v_window[0].reshape(bs, WINDOW, KV_WIDTH)
    na_s, k_win_s, v_win_s = _attn_sample(q_s, kv_s, ck, cv, sinks, g_oa, nseq_total=bs, t_new=ts)
    y_s, fst_s = _ffn(xs, na_s, nb_s, w_out_b, g_ffn, w_gate_b, w_up_b, fcw, fcb, w_down_b, g_fin,
                      state_ffn_conv[0], seq_len=ts)

    win_shape = (1, bs, WINDOW, N_KV_HEADS, HEAD_DIM)
    return (y_p.reshape(bp, sp, D_MODEL), y_s.reshape(bs, ts, D_MODEL),
            k_win_p, v_win_p, conv_p, ffn_conv_p,
            k_win_s.reshape(win_shape), v_win_s.reshape(win_shape), cst_s[None], fst_s[None])
```

```python
import functools

import jax
import jax.numpy as jnp
from jax import lax
from jax.experimental import pallas as pl
from jax.experimental.pallas import tpu as pltpu

D_MODEL = 2048
ATTN_WIDTH = 1024
CONV_CH = 1024
HEAD_DIM = 64
N_HEADS = 16
N_KV_HEADS = 4
GROUP = 4
KV_WIDTH = 256
WINDOW = 128
BLOCK = 128
D_FF = 5632
EPS = 1e-6
NEG = -1e30

SUBLANES = 8
BF16_ROWS = 16
Q_BLOCKS = 4
SAMPLE_SEQS = 32
IN_ROWS = 512
COL_TILE = 512
assert ATTN_WIDTH == 2 * COL_TILE and 2 * KV_WIDTH == COL_TILE
CONV_STEPS = CONV_CH // COL_TILE
FF_ROWS = 512
FF_TILE = 512
FF_STEPS = D_FF // FF_TILE
B_BLK0 = (ATTN_WIDTH + 2 * KV_WIDTH) // COL_TILE
C_BLK0 = B_BLK0 + CONV_STEPS
U_BLK0 = C_BLK0 + CONV_STEPS
VMEM_LIMIT = 56 * 1024 * 1024

_SLOPES = [2.0 ** (-8.0 * (h + 1) / N_HEADS) for h in range(N_HEADS)]
_BF = jnp.bfloat16
_F32 = jnp.float32


def _rms(x, g):
    return x * lax.rsqrt(jnp.mean(x * x, axis=-1, keepdims=True) + EPS) * g


def _dot(a, b):
    return jnp.dot(a, b, preferred_element_type=_F32)


def _dot_nt(a, b):
    return lax.dot_general(a, b, (((1,), (1,)), ((), ())), preferred_element_type=_F32)


def _conv3_carry(cur, w_ref, ext_ref, carry_ref, is_seq_start):
    rows = cur.shape[0]
    prev = jnp.where(is_seq_start, 0.0, carry_ref[...])
    ext_ref[0:SUBLANES, :] = prev
    ext_ref[SUBLANES:SUBLANES + rows, :] = cur
    carry_ref[...] = cur[rows - SUBLANES:, :]
    p1 = ext_ref[SUBLANES - 1:SUBLANES - 1 + rows, :]
    p2 = ext_ref[SUBLANES - 2:SUBLANES - 2 + rows, :]
    return w_ref[2:3, :] * cur + w_ref[1:2, :] * p1 + w_ref[0:1, :] * p2


def _conv3_seqs(cur, w_ref, state_ref):
    rows, ch = cur.shape
    nseq = rows // SUBLANES
    cur3 = cur.reshape(nseq, SUBLANES, ch)
    st = state_ref[...]
    t = lax.broadcasted_iota(jnp.int32, cur3.shape, 1)
    s0 = jnp.broadcast_to(st[:, 0:1, :], cur3.shape)
    s1 = jnp.broadcast_to(st[:, 1:2, :], cur3.shape)
    p1 = jnp.where(t == 0, s1, pltpu.roll(cur3, 1, 1))
    p2 = jnp.where(t == 0, s0, jnp.where(t == 1, s1, pltpu.roll(cur3, 2, 1)))
    w = w_ref[...]
    out = w[2:3, :][None] * cur3 + w[1:2, :][None] * p1 + w[0:1, :][None] * p2
    return out.reshape(rows, ch), cur3[:, SUBLANES - 2:, :]


def _inproj_kernel(*refs, seq_tiles, q_dtype, n_cast):
    carry_mode = seq_tiles is not None
    n_in = 5 if carry_mode else 6
    ins, rest = refs[:n_in], refs[n_in:]
    cast_in, outs, cast_out, scr = (rest[:n_cast], rest[n_cast:n_cast + 4],
                                    rest[n_cast + 4:2 * n_cast + 4], rest[2 * n_cast + 4:])
    q_ref, kv_ref, nb_ref, st_ref = outs
    if carry_mode:
        x_ref, g_ref, w_ref, cw_ref, gc_ref = ins
        ext_scr, carry_scr = scr
    else:
        x_ref, g_ref, w_ref, cw_ref, gc_ref, state_ref = ins
    i = pl.program_id(0)

    _cast_rows(cast_in + cast_out)
    h = _rms(x_ref[...], g_ref[...]).astype(_BF)
    q_ref[:, :COL_TILE] = (_dot(h, w_ref[0]) * (HEAD_DIM ** -0.5)).astype(q_dtype)
    q_ref[:, COL_TILE:] = (_dot(h, w_ref[1]) * (HEAD_DIM ** -0.5)).astype(q_dtype)
    kv_ref[...] = _dot(h, w_ref[2])

    ys, ss = [], 0.0
    for c in range(CONV_STEPS):
        sl = slice(c * COL_TILE, (c + 1) * COL_TILE)
        cu = _dot(h, w_ref[C_BLK0 + c]) * _dot(h, w_ref[U_BLK0 + c])
        if carry_mode:
            conv = _conv3_carry(cu, cw_ref.at[:, sl], ext_scr, carry_scr.at[c], i % seq_tiles == 0)
            st_ref[0, :, sl] = cu[cu.shape[0] - 2:, :]
        else:
            conv, last2 = _conv3_seqs(cu, cw_ref.at[:, sl], state_ref.at[:, :, sl])
            st_ref[:, :, sl] = last2
        y = _dot(h, w_ref[B_BLK0 + c]) * conv
        ys.append(y)
        ss = ss + jnp.sum(y * y, axis=-1, keepdims=True)
    inv = lax.rsqrt(ss * (1.0 / CONV_CH) + EPS)
    for c in range(CONV_STEPS):
        sl = slice(c * COL_TILE, (c + 1) * COL_TILE)
        nb_ref[:, sl] = (ys[c] * inv * gc_ref[:, sl]).astype(_BF)


def _inproj(x, g, w_in, conv_w, g_conv, state, cast_weights, *, seq_len, q_dtype):
    n = x.shape[0]
    tm = IN_ROWS
    nt = n // tm
    carry_mode = seq_len >= tm
    seq_tiles = seq_len // tm if carry_mode else None

    in_specs = [
        pl.BlockSpec((tm, D_MODEL), lambda i: (i, 0)),
        pl.BlockSpec((1, D_MODEL), lambda i: (0, 0)),
        pl.BlockSpec(w_in.shape, lambda i: (0, 0, 0), pipeline_mode=pl.Buffered(1)),
        pl.BlockSpec((3, CONV_CH), lambda i: (0, 0)),
        pl.BlockSpec((1, CONV_CH), lambda i: (0, 0)),
    ]
    args = [x, g, w_in, conv_w, g_conv]
    scratch = []
    if carry_mode:
        st_shape = jax.ShapeDtypeStruct((nt, 2, CONV_CH), _F32)
        st_spec = pl.BlockSpec((1, 2, CONV_CH), lambda i: (i, 0, 0))
        scratch += [pltpu.VMEM((tm + SUBLANES, COL_TILE), _F32),
                    pltpu.VMEM((CONV_STEPS, SUBLANES, COL_TILE), _F32)]
    else:
        nseq = tm // seq_len
        st_shape = jax.ShapeDtypeStruct((n // seq_len, 2, CONV_CH), _F32)
        st_spec = pl.BlockSpec((nseq, 2, CONV_CH), lambda i: (i, 0, 0))
        in_specs.append(st_spec)
        args.append(state)
    out_shape = [
        jax.ShapeDtypeStruct((n, ATTN_WIDTH), q_dtype),
        jax.ShapeDtypeStruct((n, 2 * KV_WIDTH), _F32),
        jax.ShapeDtypeStruct((n, CONV_CH), _BF),
        st_shape,
    ]
    out_specs = [
        pl.BlockSpec((tm, ATTN_WIDTH), lambda i: (i, 0)),
        pl.BlockSpec((tm, 2 * KV_WIDTH), lambda i: (i, 0)),
        pl.BlockSpec((tm, CONV_CH), lambda i: (i, 0)),
        st_spec,
    ]
    cast_in, cast_out, cast_shapes = _cast_specs(cast_weights, nt, lambda i: i)
    return pl.pallas_call(
        functools.partial(_inproj_kernel, seq_tiles=seq_tiles, q_dtype=q_dtype, n_cast=len(cast_weights)),
        grid=(nt,),
        in_specs=in_specs + cast_in,
        out_specs=out_specs + cast_out,
        out_shape=out_shape + cast_shapes,
        scratch_shapes=scratch,
        compiler_params=pltpu.CompilerParams(
            dimension_semantics=("arbitrary",), vmem_limit_bytes=VMEM_LIMIT),
        name="inproj_carry" if carry_mode else "inproj_seqs",
    )(*args, *[w for w, _ in cast_weights])


def _softmax_sink(s, sink):
    m = jnp.maximum(jnp.max(s, axis=-1, keepdims=True), sink)
    p = jnp.exp(s - m)
    return p, jnp.sum(p, axis=-1, keepdims=True) + jnp.exp(sink - m)


def _half_lane_operands(slab, low_half):
    lo = lax.broadcasted_iota(jnp.int32, slab.shape, 1) < HEAD_DIM
    moved = pltpu.roll(slab, HEAD_DIM, 1)
    in_low = jnp.where(lo, slab if low_half else moved, 0.0).astype(_BF)
    in_high = jnp.where(lo, 0.0, moved if low_half else slab).astype(_BF)
    return in_low, in_high


def _attn_prompt_kernel(sinks_ref, q_ref, kvc_ref, kvp_ref, g_ref, o_ref, acc_ref, cast_refs):
    i = pl.program_id(1)
    pair_w = 2 * HEAD_DIM
    qi = lax.broadcasted_iota(jnp.int32, (BLOCK, 2 * BLOCK), 0)
    kj = lax.broadcasted_iota(jnp.int32, (BLOCK, 2 * BLOCK), 1)
    dist_i = qi - kj + BLOCK
    in_window = (dist_i >= 0) & (dist_i <= WINDOW)
    far = -NEG / min(_SLOPES)
    dist_mid = jnp.where(in_window, dist_i.astype(_F32), far)
    dist_first = jnp.where(in_window & ((kj >= BLOCK) | (i > 0)), dist_i.astype(_F32), far)
    low = lax.broadcasted_iota(jnp.int32, (BLOCK, pair_w), 1) < HEAD_DIM

    k_ops, v_ops = [], []
    for m in range(N_KV_HEADS // 2):
        ksl = slice(m * pair_w, (m + 1) * pair_w)
        vsl = slice(KV_WIDTH + m * pair_w, KV_WIDTH + (m + 1) * pair_w)
        kslab = jnp.concatenate([kvp_ref[:, ksl], kvc_ref[:, ksl]], axis=0)
        vslab = jnp.concatenate([kvp_ref[:, vsl], kvc_ref[:, vsl]], axis=0)
        for low_half in (True, False):
            k_ops.append(_half_lane_operands(kslab, low_half))
            v_ops.append(_half_lane_operands(vslab, low_half))

    n_pairs = N_HEADS // 2
    work = [(u, c) for u in range(Q_BLOCKS) for c in range(n_pairs)]

    def scores(u, c):
        qc = q_ref[u * BLOCK:(u + 1) * BLOCK, c * pair_w:(c + 1) * pair_w]
        keys = slice(u * BLOCK, (u + 2) * BLOCK)
        k_low, k_high = k_ops[c // 2]
        return _dot_nt(qc, k_low[keys]), _dot_nt(qc, k_high[keys])

    s_next = scores(*work[0])
    for n, (u, c) in enumerate(work):
        s_pair = s_next
        if n + 1 < len(work):
            s_next = scores(*work[n + 1])
        keys = slice(u * BLOCK, (u + 2) * BLOCK)
        v_low, v_high = v_ops[c // 2]
        dist = dist_first if u == 0 else dist_mid
        p0, d0 = _softmax_sink(s_pair[0] - _SLOPES[2 * c] * dist, sinks_ref[0, 2 * c])
        p1, d1 = _softmax_sink(s_pair[1] - _SLOPES[2 * c + 1] * dist, sinks_ref[0, 2 * c + 1])
        o = _dot(p0.astype(_BF), v_low[keys]) + _dot(p1.astype(_BF), v_high[keys])
        acc_ref[u * BLOCK:(u + 1) * BLOCK, c * pair_w:(c + 1) * pair_w] = o / jnp.where(low, d0, d1)
    o_ref[...] = _rms(acc_ref[...], g_ref[...]).astype(_BF)
    _cast_rows(cast_refs)


def _cast_rows(cast_refs):
    n = len(cast_refs) // 2
    for src, dst in zip(cast_refs[:n], cast_refs[n:]):
        if len(dst.shape) == 2:
            dst[...] = src[...].astype(_BF)
        else:
            tw = dst.shape[2]
            for t in range(dst.shape[0]):
                dst[t] = src[:, t * tw:(t + 1) * tw].astype(_BF)


def _cast_specs(weights, steps, index):
    in_specs, out_specs, shapes = [], [], []
    for w, col_tile in weights:
        k, n = w.shape
        rows = k // steps
        assert rows * steps == k and rows % BF16_ROWS == 0
        in_specs.append(pl.BlockSpec((rows, n), lambda *ids: (index(*ids), 0)))
        if col_tile is None:
            out_specs.append(pl.BlockSpec((rows, n), lambda *ids: (index(*ids), 0)))
            shapes.append(jax.ShapeDtypeStruct((k, n), _BF))
        else:
            out_specs.append(pl.BlockSpec((n // col_tile, rows, col_tile), lambda *ids: (0, index(*ids), 0)))
            shapes.append(jax.ShapeDtypeStruct((n // col_tile, k, col_tile), _BF))
    return in_specs, out_specs, shapes


def _attn_prompt_body(sinks_ref, q_ref, kvc_ref, kvp_ref, g_ref, *rest, n_cast):
    cast_in, (o_ref,), cast_out, (acc_ref,) = (rest[:n_cast], rest[n_cast:n_cast + 1],
                                               rest[n_cast + 1:2 * n_cast + 1], rest[2 * n_cast + 1:])
    _attn_prompt_kernel(sinks_ref, q_ref, kvc_ref, kvp_ref, g_ref, o_ref, acc_ref, cast_in + cast_out)


def _attn_prompt(q, kv, sinks, g, cast_weights, *, batch, seq):
    rows = Q_BLOCKS * BLOCK
    nb = seq // rows
    cast_in, cast_out, cast_shapes = _cast_specs(cast_weights, batch * nb, lambda b, i: b * nb + i)
    return pl.pallas_call(
        functools.partial(_attn_prompt_body, n_cast=len(cast_weights)),
        grid=(batch, nb),
        in_specs=[
            pl.BlockSpec(memory_space=pltpu.SMEM),
            pl.BlockSpec((rows, ATTN_WIDTH), lambda b, i: (b * nb + i, 0)),
            pl.BlockSpec((rows, 2 * KV_WIDTH), lambda b, i: (b * nb + i, 0)),
            pl.BlockSpec((BLOCK, 2 * KV_WIDTH), lambda b, i: (jnp.maximum((b * nb + i) * Q_BLOCKS - 1, 0), 0)),
            pl.BlockSpec((1, ATTN_WIDTH), lambda b, i: (0, 0)),
        ] + cast_in,
        out_specs=[pl.BlockSpec((rows, ATTN_WIDTH), lambda b, i: (b * nb + i, 0))] + cast_out,
        out_shape=[jax.ShapeDtypeStruct((batch * seq, ATTN_WIDTH), _BF)] + cast_shapes,
        scratch_shapes=[pltpu.VMEM((rows, ATTN_WIDTH), _F32)],
        compiler_params=pltpu.CompilerParams(
            dimension_semantics=("arbitrary", "arbitrary"), vmem_limit_bytes=VMEM_LIMIT),
        name="attn_prompt",
    )(sinks, q, kv, kv, g, *[w for w, _ in cast_weights])


def _round_bf16(x):
    return x.astype(_BF).astype(_F32)


def _attn_sample_kernel(sinks_ref, q_ref, kvn_ref, ck_ref, cv_ref, g_ref,
                        o_ref, kwin_ref, vwin_ref, acc_ref, s_scr, *, nseq, t_new):
    rows = GROUP * t_new
    nkeys = WINDOW + t_new
    ksl = [slice(kvh * HEAD_DIM, (kvh + 1) * HEAD_DIM) for kvh in range(N_KV_HEADS)]
    vsl = [slice(KV_WIDTH + kvh * HEAD_DIM, KV_WIDTH + (kvh + 1) * HEAD_DIM) for kvh in range(N_KV_HEADS)]

    for kvh in range(N_KV_HEADS):
        for b in range(nseq):
            rs = slice(b * t_new, (b + 1) * t_new)
            k = _round_bf16(jnp.concatenate([ck_ref[b, :, ksl[kvh]], kvn_ref[rs, ksl[kvh]]], axis=0))
            qs = jnp.concatenate(
                [q_ref[rs, (kvh * GROUP + gq) * HEAD_DIM:(kvh * GROUP + gq + 1) * HEAD_DIM]
                 for gq in range(GROUP)], axis=0)
            s_scr[kvh, b * rows:(b + 1) * rows, :] = _dot_nt(_round_bf16(qs), k)

    r = lax.broadcasted_iota(jnp.int32, (nseq * rows, 1), 0)
    gq_of_row = (r // t_new) % GROUP
    t = lax.broadcasted_iota(jnp.int32, (nseq * rows, nkeys), 0) % t_new
    kj = lax.broadcasted_iota(jnp.int32, (nseq * rows, nkeys), 1)
    dist_i = t + WINDOW - kj
    valid = (dist_i >= 0) & (dist_i <= WINDOW)
    dist = jnp.where(valid, dist_i.astype(_F32), -NEG / min(_SLOPES))
    for kvh in range(N_KV_HEADS):
        slope = jnp.full((nseq * rows, 1), _SLOPES[kvh * GROUP], _F32)
        sink = jnp.full((nseq * rows, 1), sinks_ref[0, kvh * GROUP], _F32)
        for gq in range(1, GROUP):
            slope = jnp.where(gq_of_row == gq, _SLOPES[kvh * GROUP + gq], slope)
            sink = jnp.where(gq_of_row == gq, sinks_ref[0, kvh * GROUP + gq], sink)
        p, denom = _softmax_sink(s_scr[kvh] - slope * dist, sink)
        s_scr[kvh] = _round_bf16(p / denom)

    for kvh in range(N_KV_HEADS):
        for b in range(nseq):
            rs = slice(b * t_new, (b + 1) * t_new)
            v = _round_bf16(jnp.concatenate([cv_ref[b, :, ksl[kvh]], kvn_ref[rs, vsl[kvh]]], axis=0))
            o = _dot(s_scr[kvh, b * rows:(b + 1) * rows, :], v)
            for gq in range(GROUP):
                h = kvh * GROUP + gq
                acc_ref[rs, h * HEAD_DIM:(h + 1) * HEAD_DIM] = o[gq * t_new:(gq + 1) * t_new, :]
    o_ref[...] = _rms(acc_ref[...], g_ref[...]).astype(_BF)
    kwin_ref[:, 0:WINDOW - t_new, :] = ck_ref[:, t_new:, :]
    vwin_ref[:, 0:WINDOW - t_new, :] = cv_ref[:, t_new:, :]
    kwin_ref[:, WINDOW - t_new:, :] = kvn_ref[:, 0:KV_WIDTH].reshape(nseq, t_new, KV_WIDTH)
    vwin_ref[:, WINDOW - t_new:, :] = kvn_ref[:, KV_WIDTH:].reshape(nseq, t_new, KV_WIDTH)


def _attn_sample(q, kv, cache_k, cache_v, sinks, g, *, nseq_total, t_new, nseq=SAMPLE_SEQS):
    rows = nseq * t_new
    cache_spec = pl.BlockSpec((nseq, WINDOW, KV_WIDTH), lambda i: (i, 0, 0))
    return pl.pallas_call(
        functools.partial(_attn_sample_kernel, nseq=nseq, t_new=t_new),
        grid=(nseq_total // nseq,),
        in_specs=[
            pl.BlockSpec(memory_space=pltpu.SMEM),
            pl.BlockSpec((rows, ATTN_WIDTH), lambda i: (i, 0)),
            pl.BlockSpec((rows, 2 * KV_WIDTH), lambda i: (i, 0)),
            cache_spec,
            cache_spec,
            pl.BlockSpec((1, ATTN_WIDTH), lambda i: (0, 0)),
        ],
        out_specs=[pl.BlockSpec((rows, ATTN_WIDTH), lambda i: (i, 0)), cache_spec, cache_spec],
        out_shape=[
            jax.ShapeDtypeStruct((nseq_total * t_new, ATTN_WIDTH), _BF),
            jax.ShapeDtypeStruct(cache_k.shape, _F32),
            jax.ShapeDtypeStruct(cache_v.shape, _F32),
        ],
        scratch_shapes=[pltpu.VMEM((rows, ATTN_WIDTH), _F32),
                        pltpu.VMEM((N_KV_HEADS, nseq * GROUP * t_new, WINDOW + t_new), _F32)],
        compiler_params=pltpu.CompilerParams(dimension_semantics=("arbitrary",), vmem_limit_bytes=VMEM_LIMIT),
        name="attn_sample",
    )(sinks, q, kv, cache_k, cache_v, g)


def _ffn_kernel(*refs, seq_tiles):
    carry_mode = seq_tiles is not None
    if carry_mode:
        (x_ref, na_ref, nb_ref, woa_ref, wob_ref, g_ref, wg_ref, wu_ref, cw_ref, cb_ref, wd_ref, gf_ref,
         o_ref, st_ref, h_scr, acc_scr, ext_scr, carry_scr) = refs
    else:
        (x_ref, na_ref, nb_ref, woa_ref, wob_ref, g_ref, wg_ref, wu_ref, cw_ref, cb_ref, wd_ref, gf_ref, state_ref,
         o_ref, st_ref, h_scr, acc_scr) = refs
    i = pl.program_id(0)
    j = pl.program_id(1)

    @pl.when(j == 0)
    def _():
        x1 = x_ref[...] + _dot(na_ref[...], woa_ref[...]) + _dot(nb_ref[...], wob_ref[...])
        acc_scr[...] = x1
        h_scr[...] = _rms(x1, g_ref[...]).astype(_BF)

    h = h_scr[...]
    gate = _dot(h, wg_ref[...])
    if carry_mode:
        conv = _conv3_carry(gate, cw_ref, ext_scr, carry_scr.at[j], i % seq_tiles == 0)
        st_ref[0] = gate[gate.shape[0] - 2:, :]
    else:
        conv, last2 = _conv3_seqs(gate, cw_ref, state_ref)
        st_ref[...] = last2
    a = conv + cb_ref[...]
    act = ((a / (1.0 + jnp.exp(-a))) * _dot(h, wu_ref[...])).astype(_BF)
    acc_scr[...] += _dot(act, wd_ref[...])

    @pl.when(j == FF_STEPS - 1)
    def _():
        o_ref[...] = _rms(acc_scr[...], gf_ref[...])


def _ffn(x, na, nb, w_out, g, w_gate, w_up, conv_w, conv_b, w_down, g_final, state, *, seq_len):
    n = x.shape[0]
    tm = FF_ROWS
    nt = n // tm
    carry_mode = seq_len >= tm
    seq_tiles = seq_len // tm if carry_mode else None

    def st_map(i, j):
        return (i, 0, j)

    in_specs = [
        pl.BlockSpec((tm, D_MODEL), lambda i, j: (i, 0)),
        pl.BlockSpec((tm, ATTN_WIDTH), lambda i, j: (i, 0)),
        pl.BlockSpec((tm, CONV_CH), lambda i, j: (i, 0)),
        pl.BlockSpec((ATTN_WIDTH, D_MODEL), lambda i, j: (0, 0), pipeline_mode=pl.Buffered(1)),
        pl.BlockSpec((CONV_CH, D_MODEL), lambda i, j: (1, 0), pipeline_mode=pl.Buffered(1)),
        pl.BlockSpec((1, D_MODEL), lambda i, j: (0, 0)),
        pl.BlockSpec((None, D_MODEL, FF_TILE), lambda i, j: (j, 0, 0)),
        pl.BlockSpec((None, D_MODEL, FF_TILE), lambda i, j: (j, 0, 0)),
        pl.BlockSpec((3, FF_TILE), lambda i, j: (0, j)),
        pl.BlockSpec((1, FF_TILE), lambda i, j: (0, j)),
        pl.BlockSpec((FF_TILE, D_MODEL), lambda i, j: (j, 0)),
        pl.BlockSpec((1, D_MODEL), lambda i, j: (0, 0)),
    ]
    args = [x, na, nb, w_out, w_out, g, w_gate, w_up, conv_w, conv_b, w_down, g_final]
    scratch = [pltpu.VMEM((tm, D_MODEL), _BF), pltpu.VMEM((tm, D_MODEL), _F32)]
    if carry_mode:
        st_shape = jax.ShapeDtypeStruct((nt, 2, D_FF), _F32)
        st_spec = pl.BlockSpec((1, 2, FF_TILE), st_map)
        scratch += [pltpu.VMEM((tm + SUBLANES, FF_TILE), _F32),
                    pltpu.VMEM((FF_STEPS, SUBLANES, FF_TILE), _F32)]
    else:
        nseq = tm // seq_len
        st_shape = jax.ShapeDtypeStruct((n // seq_len, 2, D_FF), _F32)
        st_spec = pl.BlockSpec((nseq, 2, FF_TILE), st_map)
        in_specs.append(st_spec)
        args.append(state)
    return pl.pallas_call(
        functools.partial(_ffn_kernel, seq_tiles=seq_tiles),
        grid=(nt, FF_STEPS),
        in_specs=in_specs,
        out_specs=[pl.BlockSpec((tm, D_MODEL), lambda i, j: (i, 0)), st_spec],
        out_shape=[jax.ShapeDtypeStruct((n, D_MODEL), _F32), st_shape],
        scratch_shapes=scratch,
        compiler_params=pltpu.CompilerParams(
            dimension_semantics=("arbitrary", "arbitrary"), vmem_limit_bytes=VMEM_LIMIT),
        name="ffn_carry" if carry_mode else "ffn_seqs",
    )(*args)


def _tile_cast_kernel(w_ref, o_ref):
    o_ref[...] = w_ref[...].astype(_BF)


def _tile_cast(w, col_tile):
    k, n = w.shape
    return pl.pallas_call(
        _tile_cast_kernel,
        grid=(n // col_tile,),
        in_specs=[pl.BlockSpec((k, col_tile), lambda t: (0, t))],
        out_specs=pl.BlockSpec((None, k, col_tile), lambda t: (t, 0, 0)),
        out_shape=jax.ShapeDtypeStruct((n // col_tile, k, col_tile), _BF),
        compiler_params=pltpu.CompilerParams(dimension_semantics=("arbitrary",)),
        name="tile_cast",
    )(w)


def kernel(x_prompt, x_sample, cache_k_window, cache_v_window, state_conv, state_ffn_conv, g_attn_norm, w_in, attn_sinks, conv_w, g_out_attn, g_out_conv, w_out, g_ffn_norm, w_gate, w_up, ffn_conv_w, ffn_conv_b, w_down, g_final):
    assert w_in.shape[0] == 1, "single layer"
    bp, sp, _ = x_prompt.shape
    bs, ts, _ = x_sample.shape
    assert sp % IN_ROWS == 0 and (bs * ts) % IN_ROWS == 0 and ts == SUBLANES

    w_in_b = _tile_cast(w_in[0], COL_TILE)
    g_attn = g_attn_norm[0].reshape(1, D_MODEL)
    g_ffn = g_ffn_norm[0].reshape(1, D_MODEL)
    g_fin = g_final.reshape(1, D_MODEL)
    g_oa = g_out_attn[0].reshape(1, ATTN_WIDTH)
    g_oc = g_out_conv[0].reshape(1, CONV_CH)
    sinks = attn_sinks[0].reshape(1, N_HEADS)
    cw = conv_w[0]
    fcw = ffn_conv_w[0]
    fcb = ffn_conv_b[0].reshape(1, D_FF)

    xp = x_prompt.reshape(bp * sp, D_MODEL)
    q_p, kv_p, nb_p, cst_p, w_out_b, w_down_b = _inproj(
        xp, g_attn, w_in_b, cw, g_oc, None, [(w_out[0], None), (w_down[0], None)], seq_len=sp, q_dtype=_BF)
    na_p, w_gate_b, w_up_b = _attn_prompt(
        q_p, kv_p, sinks, g_oa, [(w_gate[0], FF_TILE), (w_up[0], FF_TILE)], batch=bp, seq=sp)
    y_p, fst_p = _ffn(xp, na_p, nb_p, w_out_b, g_ffn, w_gate_b, w_up_b, fcw, fcb, w_down_b, g_fin, None, seq_len=sp)

    seq_tiles = sp // min(IN_ROWS, sp)
    ff_seq_tiles = sp // FF_ROWS
    kv_p3 = kv_p.reshape(bp, sp, 2 * KV_WIDTH)[:, sp - WINDOW:, :]
    k_win_p = kv_p3[:, :, :KV_WIDTH].reshape(1, bp, WINDOW, N_KV_HEADS, HEAD_DIM)
    v_win_p = kv_p3[:, :, KV_WIDTH:].reshape(1, bp, WINDOW, N_KV_HEADS, HEAD_DIM)
    conv_p = cst_p[seq_tiles - 1::seq_tiles][None]
    ffn_conv_p = fst_p[ff_seq_tiles - 1::ff_seq_tiles][None]

    xs = x_sample.reshape(bs * ts, D_MODEL)
    q_s, kv_s, nb_s, cst_s = _inproj(xs, g_attn, w_in_b, cw, g_oc, state_conv[0], [], seq_len=ts, q_dtype=_F32)
    ck = cache_k_window[0].reshape(bs, WINDOW, KV_WIDTH)
    cv = cache_v_window[0].reshape(bs, WINDOW, KV_WIDTH)
    na_s, k_win_s, v_win_s = _attn_sample(q_s, kv_s, ck, cv, sinks, g_oa, nseq_total=bs, t_new=ts)
    y_s, fst_s = _ffn(xs, na_s, nb_s, w_out_b, g_ffn, w_gate_b, w_up_b, fcw, fcb, w_down_b, g_fin,
                      state_ffn_conv[0], seq_len=ts)

    win_shape = (1, bs, WINDOW, N_KV_HEADS, HEAD_DIM)
    return (y_p.reshape(bp, sp, D_MODEL), y_s.reshape(bs, ts, D_MODEL),
            k_win_p, v_win_p, conv_p, ffn_conv_p,
            k_win_s.reshape(win_shape), v_win_s.reshape(win_shape), cst_s[None], fst_s[None])
```

```python
import functools

import jax
import jax.numpy as jnp
from jax import lax
from jax.experimental import pallas as pl
from jax.experimental.pallas import tpu as pltpu

D_MODEL = 2048
ATTN_WIDTH = 1024
CONV_CH = 1024
HEAD_DIM = 64
N_HEADS = 16
N_KV_HEADS = 4
GROUP = 4
KV_WIDTH = 256
WINDOW = 128
BLOCK = 128
D_FF = 5632
EPS = 1e-6
NEG = -1e30

SUBLANES = 8
BF16_ROWS = 16
Q_BLOCKS = 4
SAMPLE_SEQS = 32
IN_ROWS = 512
COL_TILE = 512
assert ATTN_WIDTH == 2 * COL_TILE and 2 * KV_WIDTH == COL_TILE
CONV_STEPS = CONV_CH // COL_TILE
FF_ROWS = 512
FF_TILE = 512
FF_STEPS = D_FF // FF_TILE
B_BLK0 = (ATTN_WIDTH + 2 * KV_WIDTH) // COL_TILE
C_BLK0 = B_BLK0 + CONV_STEPS
U_BLK0 = C_BLK0 + CONV_STEPS
VMEM_LIMIT = 56 * 1024 * 1024

_SLOPES = [2.0 ** (-8.0 * (h + 1) / N_HEADS) for h in range(N_HEADS)]
_BF = jnp.bfloat16
_F32 = jnp.float32


def _rms(x, g):
    return x * lax.rsqrt(jnp.mean(x * x, axis=-1, keepdims=True) + EPS) * g


def _dot(a, b):
    return jnp.dot(a, b, preferred_element_type=_F32)


def _dot_nt(a, b):
    return lax.dot_general(a, b, (((1,), (1,)), ((), ())), preferred_element_type=_F32)


def _conv3_carry(cur, w_ref, ext_ref, carry_ref, is_seq_start):
    rows = cur.shape[0]
    prev = jnp.where(is_seq_start, 0.0, carry_ref[...])
    ext_ref[0:SUBLANES, :] = prev
    ext_ref[SUBLANES:SUBLANES + rows, :] = cur
    carry_ref[...] = cur[rows - SUBLANES:, :]
    p1 = ext_ref[SUBLANES - 1:SUBLANES - 1 + rows, :]
    p2 = ext_ref[SUBLANES - 2:SUBLANES - 2 + rows, :]
    return w_ref[2:3, :] * cur + w_ref[1:2, :] * p1 + w_ref[0:1, :] * p2


def _conv3_seqs(cur, w_ref, state_ref):
    rows, ch = cur.shape
    nseq = rows // SUBLANES
    cur3 = cur.reshape(nseq, SUBLANES, ch)
    st = state_ref[...]
    t = lax.broadcasted_iota(jnp.int32, cur3.shape, 1)
    s0 = jnp.broadcast_to(st[:, 0:1, :], cur3.shape)
    s1 = jnp.broadcast_to(st[:, 1:2, :], cur3.shape)
    p1 = jnp.where(t == 0, s1, pltpu.roll(cur3, 1, 1))
    p2 = jnp.where(t == 0, s0, jnp.where(t == 1, s1, pltpu.roll(cur3, 2, 1)))
    w = w_ref[...]
    out = w[2:3, :][None] * cur3 + w[1:2, :][None] * p1 + w[0:1, :][None] * p2
    return out.reshape(rows, ch), cur3[:, SUBLANES - 2:, :]


def _inproj_kernel(*refs, seq_tiles, q_dtype, n_cast):
    carry_mode = seq_tiles is not None
    n_in = 5 if carry_mode else 6
    ins, rest = refs[:n_in], refs[n_in:]
    cast_in, outs, cast_out, scr = (rest[:n_cast], rest[n_cast:n_cast + 4],
                                    rest[n_cast + 4:2 * n_cast + 4], rest[2 * n_cast + 4:])
    q_ref, kv_ref, nb_ref, st_ref = outs
    if carry_mode:
        x_ref, g_ref, w_ref, cw_ref, gc_ref = ins
        ext_scr, carry_scr = scr
    else:
        x_ref, g_ref, w_ref, cw_ref, gc_ref, state_ref = ins
    i = pl.program_id(0)

    _cast_rows(cast_in + cast_out)
    h = _rms(x_ref[...], g_ref[...]).astype(_BF)
    q_ref[:, :COL_TILE] = (_dot(h, w_ref[0]) * (HEAD_DIM ** -0.5)).astype(q_dtype)
    q_ref[:, COL_TILE:] = (_dot(h, w_ref[1]) * (HEAD_DIM ** -0.5)).astype(q_dtype)
    kv_ref[...] = _dot(h, w_ref[2])

    ys, ss = [], 0.0
    for c in range(CONV_STEPS):
        sl = slice(c * COL_TILE, (c + 1) * COL_TILE)
        cu = _dot(h, w_ref[C_BLK0 + c]) * _dot(h, w_ref[U_BLK0 + c])
        if carry_mode:
            conv = _conv3_carry(cu, cw_ref.at[:, sl], ext_scr, carry_scr.at[c], i % seq_tiles == 0)
            st_ref[0, :, sl] = cu[cu.shape[0] - 2:, :]
        else:
            conv, last2 = _conv3_seqs(cu, cw_ref.at[:, sl], state_ref.at[:, :, sl])
            st_ref[:, :, sl] = last2
        y = _dot(h, w_ref[B_BLK0 + c]) * conv
        ys.append(y)
        ss = ss + jnp.sum(y * y, axis=-1, keepdims=True)
    inv = lax.rsqrt(ss * (1.0 / CONV_CH) + EPS)
    for c in range(CONV_STEPS):
        sl = slice(c * COL_TILE, (c + 1) * COL_TILE)
        nb_ref[:, sl] = (ys[c] * inv * gc_ref[:, sl]).astype(_BF)


def _inproj(x, g, w_in, conv_w, g_conv, state, cast_weights, *, seq_len, q_dtype):
    n = x.shape[0]
    tm = IN_ROWS
    nt = n // tm
    carry_mode = seq_len >= tm
    seq_tiles = seq_len // tm if carry_mode else None

    in_specs = [
        pl.BlockSpec((tm, D_MODEL), lambda i: (i, 0)),
        pl.BlockSpec((1, D_MODEL), lambda i: (0, 0)),
        pl.BlockSpec(w_in.shape, lambda i: (0, 0, 0), pipeline_mode=pl.Buffered(1)),
        pl.BlockSpec((3, CONV_CH), lambda i: (0, 0)),
        pl.BlockSpec((1, CONV_CH), lambda i: (0, 0)),
    ]
    args = [x, g, w_in, conv_w, g_conv]
    scratch = []
    if carry_mode:
        st_shape = jax.ShapeDtypeStruct((nt, 2, CONV_CH), _F32)
        st_spec = pl.BlockSpec((1, 2, CONV_CH), lambda i: (i, 0, 0))
        scratch += [pltpu.VMEM((tm + SUBLANES, COL_TILE), _F32),
                    pltpu.VMEM((CONV_STEPS, SUBLANES, COL_TILE), _F32)]
    else:
        nseq = tm // seq_len
        st_shape = jax.ShapeDtypeStruct((n // seq_len, 2, CONV_CH), _F32)
        st_spec = pl.BlockSpec((nseq, 2, CONV_CH), lambda i: (i, 0, 0))
        in_specs.append(st_spec)
        args.append(state)
    out_shape = [
        jax.ShapeDtypeStruct((n, ATTN_WIDTH), q_dtype),
        jax.ShapeDtypeStruct((n, 2 * KV_WIDTH), _F32),
        jax.ShapeDtypeStruct((n, CONV_CH), _BF),
        st_shape,
    ]
    out_specs = [
        pl.BlockSpec((tm, ATTN_WIDTH), lambda i: (i, 0)),
        pl.BlockSpec((tm, 2 * KV_WIDTH), lambda i: (i, 0)),
        pl.BlockSpec((tm, CONV_CH), lambda i: (i, 0)),
        st_spec,
    ]
    cast_in, cast_out, cast_shapes = _cast_specs(cast_weights, nt, lambda i: i)
    return pl.pallas_call(
        functools.partial(_inproj_kernel, seq_tiles=seq_tiles, q_dtype=q_dtype, n_cast=len(cast_weights)),
        grid=(nt,),
        in_specs=in_specs + cast_in,
        out_specs=out_specs + cast_out,
        out_shape=out_shape + cast_shapes,
        scratch_shapes=scratch,
        compiler_params=pltpu.CompilerParams(
            dimension_semantics=("arbitrary",), vmem_limit_bytes=VMEM_LIMIT),
        name="inproj_carry" if carry_mode else "inproj_seqs",
    )(*args, *[w for w, _ in cast_weights])


def _softmax_sink(s, sink):
    m = jnp.maximum(jnp.max(s, axis=-1, keepdims=True), sink)
    p = jnp.exp(s - m)
    return p, jnp.sum(p, axis=-1, keepdims=True) + jnp.exp(sink - m)


def _half_lane_operands(slab, low_half):
    lo = lax.broadcasted_iota(jnp.int32, slab.shape, 1) < HEAD_DIM
    moved = pltpu.roll(slab, HEAD_DIM, 1)
    in_low = jnp.where(lo, slab if low_half else moved, 0.0).astype(_BF)
    in_high = jnp.where(lo, 0.0, moved if low_half else slab).astype(_BF)
    return in_low, in_high


def _attn_prompt_kernel(sinks_ref, q_ref, kvc_ref, kvp_ref, g_ref, o_ref, acc_ref, cast_refs):
    i = pl.program_id(1)
    pair_w = 2 * HEAD_DIM
    qi = lax.broadcasted_iota(jnp.int32, (BLOCK, 2 * BLOCK), 0)
    kj = lax.broadcasted_iota(jnp.int32, (BLOCK, 2 * BLOCK), 1)
    dist_i = qi - kj + BLOCK
    in_window = (dist_i >= 0) & (dist_i <= WINDOW)
    far = -NEG / min(_SLOPES)
    dist_mid = jnp.where(in_window, dist_i.astype(_F32), far)
    dist_first = jnp.where(in_window & ((kj >= BLOCK) | (i > 0)), dist_i.astype(_F32), far)
    low = lax.broadcasted_iota(jnp.int32, (BLOCK, pair_w), 1) < HEAD_DIM

    k_ops, v_ops = [], []
    for m in range(N_KV_HEADS // 2):
        ksl = slice(m * pair_w, (m + 1) * pair_w)
        vsl = slice(KV_WIDTH + m * pair_w, KV_WIDTH + (m + 1) * pair_w)
        kslab = jnp.concatenate([kvp_ref[:, ksl], kvc_ref[:, ksl]], axis=0)
        vslab = jnp.concatenate([kvp_ref[:, vsl], kvc_ref[:, vsl]], axis=0)
        for low_half in (True, False):
            k_ops.append(_half_lane_operands(kslab, low_half))
            v_ops.append(_half_lane_operands(vslab, low_half))

    n_pairs = N_HEADS // 2
    work = [(u, c) for u in range(Q_BLOCKS) for c in range(n_pairs)]

    def scores(u, c):
        qc = q_ref[u * BLOCK:(u + 1) * BLOCK, c * pair_w:(c + 1) * pair_w]
        keys = slice(u * BLOCK, (u + 2) * BLOCK)
        k_low, k_high = k_ops[c // 2]
        return _dot_nt(qc, k_low[keys]), _dot_nt(qc, k_high[keys])

    s_next = scores(*work[0])
    for n, (u, c) in enumerate(work):
        s_pair = s_next
        if n + 1 < len(work):
            s_next = scores(*work[n + 1])
        keys = slice(u * BLOCK, (u + 2) * BLOCK)
        v_low, v_high = v_ops[c // 2]
        dist = dist_first if u == 0 else dist_mid
        p0, d0 = _softmax_sink(s_pair[0] - _SLOPES[2 * c] * dist, sinks_ref[0, 2 * c])
        p1, d1 = _softmax_sink(s_pair[1] - _SLOPES[2 * c + 1] * dist, sinks_ref[0, 2 * c + 1])
        o = _dot(p0.astype(_BF), v_low[keys]) + _dot(p1.astype(_BF), v_high[keys])
        acc_ref[u * BLOCK:(u + 1) * BLOCK, c * pair_w:(c + 1) * pair_w] = o / jnp.where(low, d0, d1)
    o_ref[...] = _rms(acc_ref[...], g_ref[...]).astype(_BF)
    _cast_rows(cast_refs)


def _cast_rows(cast_refs):
    n = len(cast_refs) // 2
    for src, dst in zip(cast_refs[:n], cast_refs[n:]):
        if len(dst.shape) == 2:
            dst[...] = src[...].astype(_BF)
        else:
            tw = dst.shape[2]
            for t in range(dst.shape[0]):
                dst[t] = src[:, t * tw:(t + 1) * tw].astype(_BF)


def _cast_specs(weights, steps, index):
    in_specs, out_specs, shapes = [], [], []
    for w, col_tile in weights:
        k, n = w.shape
        rows = k // steps
        assert rows * steps == k and rows % BF16_ROWS == 0
        in_specs.append(pl.BlockSpec((rows, n), lambda *ids: (index(*ids), 0)))
        if col_tile is None:
            out_specs.append(pl.BlockSpec((rows, n), lambda *ids: (index(*ids), 0)))
            shapes.append(jax.ShapeDtypeStruct((k, n), _BF))
        else:
            out_specs.append(pl.BlockSpec((n // col_tile, rows, col_tile), lambda *ids: (0, index(*ids), 0)))
            shapes.append(jax.ShapeDtypeStruct((n // col_tile, k, col_tile), _BF))
    return in_specs, out_specs, shapes


def _attn_prompt_body(sinks_ref, q_ref, kvc_ref, kvp_ref, g_ref, *rest, n_cast):
    cast_in, (o_ref,), cast_out, (acc_ref,) = (rest[:n_cast], rest[n_cast:n_cast + 1],
                                               rest[n_cast + 1:2 * n_cast + 1], rest[2 * n_cast + 1:])
    _attn_prompt_kernel(sinks_ref, q_ref, kvc_ref, kvp_ref, g_ref, o_ref, acc_ref, cast_in + cast_out)


def _attn_prompt(q, kv, sinks, g, cast_weights, *, batch, seq):
    rows = Q_BLOCKS * BLOCK
    nb = seq // rows
    cast_in, cast_out, cast_shapes = _cast_specs(cast_weights, batch * nb, lambda b, i: b * nb + i)
    return pl.pallas_call(
        functools.partial(_attn_prompt_body, n_cast=len(cast_weights)),
        grid=(batch, nb),
        in_specs=[
            pl.BlockSpec(memory_space=pltpu.SMEM),
            pl.BlockSpec((rows, ATTN_WIDTH), lambda b, i: (b * nb + i, 0)),
            pl.BlockSpec((rows, 2 * KV_WIDTH), lambda b, i: (b * nb + i, 0)),
            pl.BlockSpec((BLOCK, 2 * KV_WIDTH), lambda b, i: (jnp.maximum((b * nb + i) * Q_BLOCKS - 1, 0), 0)),
            pl.BlockSpec((1, ATTN_WIDTH), lambda b, i: (0, 0)),
        ] + cast_in,
        out_specs=[pl.BlockSpec((rows, ATTN_WIDTH), lambda b, i: (b * nb + i, 0))] + cast_out,
        out_shape=[jax.ShapeDtypeStruct((batch * seq, ATTN_WIDTH), _BF)] + cast_shapes,
        scratch_shapes=[pltpu.VMEM((rows, ATTN_WIDTH), _F32)],
        compiler_params=pltpu.CompilerParams(
            dimension_semantics=("arbitrary", "arbitrary"), vmem_limit_bytes=VMEM_LIMIT),
        name="attn_prompt",
    )(sinks, q, kv, kv, g, *[w for w, _ in cast_weights])


def _round_bf16(x):
    return x.astype(_BF).astype(_F32)


def _attn_sample_kernel(sinks_ref, q_ref, kvn_ref, ck_ref, cv_ref, g_ref,
                        o_ref, kwin_ref, vwin_ref, acc_ref, s_scr, *, nseq, t_new):
    rows = GROUP * t_new
    nkeys = WINDOW + t_new
    ksl = [slice(kvh * HEAD_DIM, (kvh + 1) * HEAD_DIM) for kvh in range(N_KV_HEADS)]
    vsl = [slice(KV_WIDTH + kvh * HEAD_DIM, KV_WIDTH + (kvh + 1) * HEAD_DIM) for kvh in range(N_KV_HEADS)]

    for kvh in range(N_KV_HEADS):
        for b in range(nseq):
            rs = slice(b * t_new, (b + 1) * t_new)
            qs = _round_bf16(jnp.concatenate(
                [q_ref[rs, (kvh * GROUP + gq) * HEAD_DIM:(kvh * GROUP + gq + 1) * HEAD_DIM]
                 for gq in range(GROUP)], axis=0))
            s_scr[kvh, b * rows:(b + 1) * rows, 0:WINDOW] = _dot(qs, _round_bf16(ck_ref[b, ksl[kvh], :]))
            s_scr[kvh, b * rows:(b + 1) * rows, WINDOW:] = _dot_nt(qs, _round_bf16(kvn_ref[rs, ksl[kvh]]))

    r = lax.broadcasted_iota(jnp.int32, (nseq * rows, 1), 0)
    gq_of_row = (r // t_new) % GROUP
    t = lax.broadcasted_iota(jnp.int32, (nseq * rows, nkeys), 0) % t_new
    kj = lax.broadcasted_iota(jnp.int32, (nseq * rows, nkeys), 1)
    dist_i = t + WINDOW - kj
    valid = (dist_i >= 0) & (dist_i <= WINDOW)
    dist = jnp.where(valid, dist_i.astype(_F32), -NEG / min(_SLOPES))
    for kvh in range(N_KV_HEADS):
        slope = jnp.full((nseq * rows, 1), _SLOPES[kvh * GROUP], _F32)
        sink = jnp.full((nseq * rows, 1), sinks_ref[0, kvh * GROUP], _F32)
        for gq in range(1, GROUP):
            slope = jnp.where(gq_of_row == gq, _SLOPES[kvh * GROUP + gq], slope)
            sink = jnp.where(gq_of_row == gq, sinks_ref[0, kvh * GROUP + gq], sink)
        p, denom = _softmax_sink(s_scr[kvh] - slope * dist, sink)
        s_scr[kvh] = _round_bf16(p / denom)

    for kvh in range(N_KV_HEADS):
        for b in range(nseq):
            rs = slice(b * t_new, (b + 1) * t_new)
            w = s_scr[kvh, b * rows:(b + 1) * rows, :]
            o = (_dot_nt(w[:, 0:WINDOW], _round_bf16(cv_ref[b, ksl[kvh], :]))
                 + _dot(w[:, WINDOW:], _round_bf16(kvn_ref[rs, vsl[kvh]])))
            for gq in range(GROUP):
                h = kvh * GROUP + gq
                acc_ref[rs, h * HEAD_DIM:(h + 1) * HEAD_DIM] = o[gq * t_new:(gq + 1) * t_new, :]
    o_ref[...] = _rms(acc_ref[...], g_ref[...]).astype(_BF)
    keep = lax.broadcasted_iota(jnp.int32, (KV_WIDTH, WINDOW), 1) < WINDOW - t_new
    for b in range(nseq):
        rs = slice(b * t_new, (b + 1) * t_new)
        for cache_ref, win_ref, cols in ((ck_ref, kwin_ref, slice(0, KV_WIDTH)),
                                         (cv_ref, vwin_ref, slice(KV_WIDTH, 2 * KV_WIDTH))):
            new_t = jnp.transpose(jnp.concatenate([kvn_ref[rs, cols]] * (WINDOW // t_new), axis=0))
            win_ref[b] = jnp.where(keep, pltpu.roll(cache_ref[b], WINDOW - t_new, 1), new_t)


def _attn_sample(q, kv, cache_k, cache_v, sinks, g, *, nseq_total, t_new, nseq=SAMPLE_SEQS):
    rows = nseq * t_new
    cache_spec = pl.BlockSpec((nseq, KV_WIDTH, WINDOW), lambda i: (i, 0, 0))
    return pl.pallas_call(
        functools.partial(_attn_sample_kernel, nseq=nseq, t_new=t_new),
        grid=(nseq_total // nseq,),
        in_specs=[
            pl.BlockSpec(memory_space=pltpu.SMEM),
            pl.BlockSpec((rows, ATTN_WIDTH), lambda i: (i, 0)),
            pl.BlockSpec((rows, 2 * KV_WIDTH), lambda i: (i, 0)),
            cache_spec,
            cache_spec,
            pl.BlockSpec((1, ATTN_WIDTH), lambda i: (0, 0)),
        ],
        out_specs=[pl.BlockSpec((rows, ATTN_WIDTH), lambda i: (i, 0)), cache_spec, cache_spec],
        out_shape=[
            jax.ShapeDtypeStruct((nseq_total * t_new, ATTN_WIDTH), _BF),
            jax.ShapeDtypeStruct(cache_k.shape, _F32),
            jax.ShapeDtypeStruct(cache_v.shape, _F32),
        ],
        scratch_shapes=[pltpu.VMEM((rows, ATTN_WIDTH), _F32),
                        pltpu.VMEM((N_KV_HEADS, nseq * GROUP * t_new, WINDOW + t_new), _F32)],
        compiler_params=pltpu.CompilerParams(dimension_semantics=("arbitrary",), vmem_limit_bytes=VMEM_LIMIT),
        name="attn_sample",
    )(sinks, q, kv, cache_k, cache_v, g)


def _ffn_kernel(*refs, seq_tiles):
    carry_mode = seq_tiles is not None
    if carry_mode:
        (x_ref, na_ref, nb_ref, woa_ref, wob_ref, g_ref, wg_ref, wu_ref, cw_ref, cb_ref, wd_ref, gf_ref,
         o_ref, st_ref, h_scr, acc_scr, ext_scr, carry_scr) = refs
    else:
        (x_ref, na_ref, nb_ref, woa_ref, wob_ref, g_ref, wg_ref, wu_ref, cw_ref, cb_ref, wd_ref, gf_ref, state_ref,
         o_ref, st_ref, h_scr, acc_scr) = refs
    i = pl.program_id(0)
    j = pl.program_id(1)

    @pl.when(j == 0)
    def _():
        x1 = x_ref[...] + _dot(na_ref[...], woa_ref[...]) + _dot(nb_ref[...], wob_ref[...])
        acc_scr[...] = x1
        h_scr[...] = _rms(x1, g_ref[...]).astype(_BF)

    h = h_scr[...]
    gate = _dot(h, wg_ref[...])
    if carry_mode:
        conv = _conv3_carry(gate, cw_ref, ext_scr, carry_scr.at[j], i % seq_tiles == 0)
        st_ref[0] = gate[gate.shape[0] - 2:, :]
    else:
        conv, last2 = _conv3_seqs(gate, cw_ref, state_ref)
        st_ref[...] = last2
    a = conv + cb_ref[...]
    act = ((a / (1.0 + jnp.exp(-a))) * _dot(h, wu_ref[...])).astype(_BF)
    acc_scr[...] += _dot(act, wd_ref[...])

    @pl.when(j == FF_STEPS - 1)
    def _():
        o_ref[...] = _rms(acc_scr[...], gf_ref[...])


def _ffn(x, na, nb, w_out, g, w_gate, w_up, conv_w, conv_b, w_down, g_final, state, *, seq_len):
    n = x.shape[0]
    tm = FF_ROWS
    nt = n // tm
    carry_mode = seq_len >= tm
    seq_tiles = seq_len // tm if carry_mode else None

    def st_map(i, j):
        return (i, 0, j)

    in_specs = [
        pl.BlockSpec((tm, D_MODEL), lambda i, j: (i, 0)),
        pl.BlockSpec((tm, ATTN_WIDTH), lambda i, j: (i, 0)),
        pl.BlockSpec((tm, CONV_CH), lambda i, j: (i, 0)),
        pl.BlockSpec((ATTN_WIDTH, D_MODEL), lambda i, j: (0, 0), pipeline_mode=pl.Buffered(1)),
        pl.BlockSpec((CONV_CH, D_MODEL), lambda i, j: (1, 0), pipeline_mode=pl.Buffered(1)),
        pl.BlockSpec((1, D_MODEL), lambda i, j: (0, 0)),
        pl.BlockSpec((None, D_MODEL, FF_TILE), lambda i, j: (j, 0, 0)),
        pl.BlockSpec((None, D_MODEL, FF_TILE), lambda i, j: (j, 0, 0)),
        pl.BlockSpec((3, FF_TILE), lambda i, j: (0, j)),
        pl.BlockSpec((1, FF_TILE), lambda i, j: (0, j)),
        pl.BlockSpec((FF_TILE, D_MODEL), lambda i, j: (j, 0)),
        pl.BlockSpec((1, D_MODEL), lambda i, j: (0, 0)),
    ]
    args = [x, na, nb, w_out, w_out, g, w_gate, w_up, conv_w, conv_b, w_down, g_final]
    scratch = [pltpu.VMEM((tm, D_MODEL), _BF), pltpu.VMEM((tm, D_MODEL), _F32)]
    if carry_mode:
        st_shape = jax.ShapeDtypeStruct((nt, 2, D_FF), _F32)
        st_spec = pl.BlockSpec((1, 2, FF_TILE), st_map)
        scratch += [pltpu.VMEM((tm + SUBLANES, FF_TILE), _F32),
                    pltpu.VMEM((FF_STEPS, SUBLANES, FF_TILE), _F32)]
    else:
        nseq = tm // seq_len
        st_shape = jax.ShapeDtypeStruct((n // seq_len, 2, D_FF), _F32)
        st_spec = pl.BlockSpec((nseq, 2, FF_TILE), st_map)
        in_specs.append(st_spec)
        args.append(state)
    return pl.pallas_call(
        functools.partial(_ffn_kernel, seq_tiles=seq_tiles),
        grid=(nt, FF_STEPS),
        in_specs=in_specs,
        out_specs=[pl.BlockSpec((tm, D_MODEL), lambda i, j: (i, 0)), st_spec],
        out_shape=[jax.ShapeDtypeStruct((n, D_MODEL), _F32), st_shape],
        scratch_shapes=scratch,
        compiler_params=pltpu.CompilerParams(
            dimension_semantics=("arbitrary", "arbitrary"), vmem_limit_bytes=VMEM_LIMIT),
        name="ffn_carry" if carry_mode else "ffn_seqs",
    )(*args)


def _tile_cast_kernel(w_ref, o_ref):
    o_ref[...] = w_ref[...].astype(_BF)


def _tile_cast(w, col_tile):
    k, n = w.shape
    return pl.pallas_call(
        _tile_cast_kernel,
        grid=(n // col_tile,),
        in_specs=[pl.BlockSpec((k, col_tile), lambda t: (0, t))],
        out_specs=pl.BlockSpec((None, k, col_tile), lambda t: (t, 0, 0)),
        out_shape=jax.ShapeDtypeStruct((n // col_tile, k, col_tile), _BF),
        compiler_params=pltpu.CompilerParams(dimension_semantics=("arbitrary",)),
        name="tile_cast",
    )(w)


def _keys_minor(cache):
    return jnp.transpose(cache, (0, 2, 3, 1)).reshape(cache.shape[0], KV_WIDTH, WINDOW)


def _keys_major(win):
    return jnp.transpose(win.reshape(win.shape[0], N_KV_HEADS, HEAD_DIM, WINDOW), (0, 3, 1, 2))[None]


def kernel(x_prompt, x_sample, cache_k_window, cache_v_window, state_conv, state_ffn_conv, g_attn_norm, w_in, attn_sinks, conv_w, g_out_attn, g_out_conv, w_out, g_ffn_norm, w_gate, w_up, ffn_conv_w, ffn_conv_b, w_down, g_final):
    assert w_in.shape[0] == 1, "single layer"
    bp, sp, _ = x_prompt.shape
    bs, ts, _ = x_sample.shape
    assert sp % IN_ROWS == 0 and (bs * ts) % IN_ROWS == 0 and ts == SUBLANES

    w_in_b = _tile_cast(w_in[0], COL_TILE)
    g_attn = g_attn_norm[0].reshape(1, D_MODEL)
    g_ffn = g_ffn_norm[0].reshape(1, D_MODEL)
    g_fin = g_final.reshape(1, D_MODEL)
    g_oa = g_out_attn[0].reshape(1, ATTN_WIDTH)
    g_oc = g_out_conv[0].reshape(1, CONV_CH)
    sinks = attn_sinks[0].reshape(1, N_HEADS)
    cw = conv_w[0]
    fcw = ffn_conv_w[0]
    fcb = ffn_conv_b[0].reshape(1, D_FF)

    xp = x_prompt.reshape(bp * sp, D_MODEL)
    q_p, kv_p, nb_p, cst_p, w_out_b, w_down_b = _inproj(
        xp, g_attn, w_in_b, cw, g_oc, None, [(w_out[0], None), (w_down[0], None)], seq_len=sp, q_dtype=_BF)
    na_p, w_gate_b, w_up_b = _attn_prompt(
        q_p, kv_p, sinks, g_oa, [(w_gate[0], FF_TILE), (w_up[0], FF_TILE)], batch=bp, seq=sp)
    y_p, fst_p = _ffn(xp, na_p, nb_p, w_out_b, g_ffn, w_gate_b, w_up_b, fcw, fcb, w_down_b, g_fin, None, seq_len=sp)

    seq_tiles = sp // min(IN_ROWS, sp)
    ff_seq_tiles = sp // FF_ROWS
    kv_p3 = kv_p.reshape(bp, sp, 2 * KV_WIDTH)[:, sp - WINDOW:, :]
    k_win_p = kv_p3[:, :, :KV_WIDTH].reshape(1, bp, WINDOW, N_KV_HEADS, HEAD_DIM)
    v_win_p = kv_p3[:, :, KV_WIDTH:].reshape(1, bp, WINDOW, N_KV_HEADS, HEAD_DIM)
    conv_p = cst_p[seq_tiles - 1::seq_tiles][None]
    ffn_conv_p = fst_p[ff_seq_tiles - 1::ff_seq_tiles][None]

    xs = x_sample.reshape(bs * ts, D_MODEL)
    q_s, kv_s, nb_s, cst_s = _inproj(xs, g_attn, w_in_b, cw, g_oc, state_conv[0], [], seq_len=ts, q_dtype=_F32)
    ck = _keys_minor(cache_k_window[0])
    cv = _keys_minor(cache_v_window[0])
    na_s, k_win_s, v_win_s = _attn_sample(q_s, kv_s, ck, cv, sinks, g_oa, nseq_total=bs, t_new=ts)
    y_s, fst_s = _ffn(xs, na_s, nb_s, w_out_b, g_ffn, w_gate_b, w_up_b, fcw, fcb, w_down_b, g_fin,
                      state_ffn_conv[0], seq_len=ts)

    win_shape = (1, bs, WINDOW, N_KV_HEADS, HEAD_DIM)
    return (y_p.reshape(bp, sp, D_MODEL), y_s.reshape(bs, ts, D_MODEL),
            k_win_p, v_win_p, conv_p, ffn_conv_p,
            _keys_major(k_win_s), _keys_major(v_win_s), cst_s[None], fst_s[None])
```

```python
import functools

import jax
import jax.numpy as jnp
from jax import lax
from jax.experimental import pallas as pl
from jax.experimental.pallas import tpu as pltpu

D_MODEL = 2048
ATTN_WIDTH = 1024
CONV_CH = 1024
HEAD_DIM = 64
N_HEADS = 16
N_KV_HEADS = 4
GROUP = 4
KV_WIDTH = 256
WINDOW = 128
BLOCK = 128
D_FF = 5632
EPS = 1e-6
NEG = -1e30

SUBLANES = 8
BF16_ROWS = 16
Q_BLOCKS = 4
SAMPLE_SEQS = 32
IN_ROWS = 512
COL_TILE = 512
assert ATTN_WIDTH == 2 * COL_TILE and 2 * KV_WIDTH == COL_TILE
CONV_STEPS = CONV_CH // COL_TILE
FF_ROWS = 512
FF_TILE = 512
FF_STEPS = D_FF // FF_TILE
B_BLK0 = (ATTN_WIDTH + 2 * KV_WIDTH) // COL_TILE
C_BLK0 = B_BLK0 + CONV_STEPS
U_BLK0 = C_BLK0 + CONV_STEPS
VMEM_LIMIT = 56 * 1024 * 1024

_SLOPES = [2.0 ** (-8.0 * (h + 1) / N_HEADS) for h in range(N_HEADS)]
_BF = jnp.bfloat16
_F32 = jnp.float32


def _rms(x, g):
    return x * lax.rsqrt(jnp.mean(x * x, axis=-1, keepdims=True) + EPS) * g


def _dot(a, b):
    return jnp.dot(a, b, preferred_element_type=_F32)


def _dot_nt(a, b):
    return lax.dot_general(a, b, (((1,), (1,)), ((), ())), preferred_element_type=_F32)


def _conv3_carry(cur, w_ref, ext_ref, carry_ref, is_seq_start):
    rows = cur.shape[0]
    prev = jnp.where(is_seq_start, 0.0, carry_ref[...])
    ext_ref[0:SUBLANES, :] = prev
    ext_ref[SUBLANES:SUBLANES + rows, :] = cur
    carry_ref[...] = cur[rows - SUBLANES:, :]
    p1 = ext_ref[SUBLANES - 1:SUBLANES - 1 + rows, :]
    p2 = ext_ref[SUBLANES - 2:SUBLANES - 2 + rows, :]
    return w_ref[2:3, :] * cur + w_ref[1:2, :] * p1 + w_ref[0:1, :] * p2


def _conv3_seqs(cur, w_ref, state_ref):
    rows, ch = cur.shape
    nseq = rows // SUBLANES
    cur3 = cur.reshape(nseq, SUBLANES, ch)
    st = state_ref[...]
    t = lax.broadcasted_iota(jnp.int32, cur3.shape, 1)
    s0 = jnp.broadcast_to(st[:, 0:1, :], cur3.shape)
    s1 = jnp.broadcast_to(st[:, 1:2, :], cur3.shape)
    p1 = jnp.where(t == 0, s1, pltpu.roll(cur3, 1, 1))
    p2 = jnp.where(t == 0, s0, jnp.where(t == 1, s1, pltpu.roll(cur3, 2, 1)))
    w = w_ref[...]
    out = w[2:3, :][None] * cur3 + w[1:2, :][None] * p1 + w[0:1, :][None] * p2
    return out.reshape(rows, ch), cur3[:, SUBLANES - 2:, :]


def _inproj_kernel(*refs, seq_tiles, q_dtype, n_cast):
    carry_mode = seq_tiles is not None
    n_in = 5 if carry_mode else 6
    ins, rest = refs[:n_in], refs[n_in:]
    cast_in, outs, cast_out, scr = (rest[:n_cast], rest[n_cast:n_cast + 4],
                                    rest[n_cast + 4:2 * n_cast + 4], rest[2 * n_cast + 4:])
    q_ref, kv_ref, nb_ref, st_ref = outs
    if carry_mode:
        x_ref, g_ref, w_ref, cw_ref, gc_ref = ins
        ext_scr, carry_scr = scr
    else:
        x_ref, g_ref, w_ref, cw_ref, gc_ref, state_ref = ins
    i = pl.program_id(0)

    _cast_rows(cast_in + cast_out)
    h = _rms(x_ref[...], g_ref[...]).astype(_BF)
    q_ref[:, :COL_TILE] = (_dot(h, w_ref[0]) * (HEAD_DIM ** -0.5)).astype(q_dtype)
    q_ref[:, COL_TILE:] = (_dot(h, w_ref[1]) * (HEAD_DIM ** -0.5)).astype(q_dtype)
    kv_ref[...] = _dot(h, w_ref[2])

    ys, ss = [], 0.0
    for c in range(CONV_STEPS):
        sl = slice(c * COL_TILE, (c + 1) * COL_TILE)
        cu = _dot(h, w_ref[C_BLK0 + c]) * _dot(h, w_ref[U_BLK0 + c])
        if carry_mode:
            conv = _conv3_carry(cu, cw_ref.at[:, sl], ext_scr, carry_scr.at[c], i % seq_tiles == 0)
            st_ref[0, :, sl] = cu[cu.shape[0] - 2:, :]
        else:
            conv, last2 = _conv3_seqs(cu, cw_ref.at[:, sl], state_ref.at[:, :, sl])
            st_ref[:, :, sl] = last2
        y = _dot(h, w_ref[B_BLK0 + c]) * conv
        ys.append(y)
        ss = ss + jnp.sum(y * y, axis=-1, keepdims=True)
    inv = lax.rsqrt(ss * (1.0 / CONV_CH) + EPS)
    for c in range(CONV_STEPS):
        sl = slice(c * COL_TILE, (c + 1) * COL_TILE)
        nb_ref[:, sl] = (ys[c] * inv * gc_ref[:, sl]).astype(_BF)


def _inproj(x, g, w_in, conv_w, g_conv, state, cast_weights, *, seq_len, q_dtype):
    n = x.shape[0]
    tm = IN_ROWS
    nt = n // tm
    carry_mode = seq_len >= tm
    seq_tiles = seq_len // tm if carry_mode else None

    in_specs = [
        pl.BlockSpec((tm, D_MODEL), lambda i: (i, 0)),
        pl.BlockSpec((1, D_MODEL), lambda i: (0, 0)),
        pl.BlockSpec(w_in.shape, lambda i: (0, 0, 0), pipeline_mode=pl.Buffered(1)),
        pl.BlockSpec((3, CONV_CH), lambda i: (0, 0)),
        pl.BlockSpec((1, CONV_CH), lambda i: (0, 0)),
    ]
    args = [x, g, w_in, conv_w, g_conv]
    scratch = []
    if carry_mode:
        st_shape = jax.ShapeDtypeStruct((nt, 2, CONV_CH), _F32)
        st_spec = pl.BlockSpec((1, 2, CONV_CH), lambda i: (i, 0, 0))
        scratch += [pltpu.VMEM((tm + SUBLANES, COL_TILE), _F32),
                    pltpu.VMEM((CONV_STEPS, SUBLANES, COL_TILE), _F32)]
    else:
        nseq = tm // seq_len
        st_shape = jax.ShapeDtypeStruct((n // seq_len, 2, CONV_CH), _F32)
        st_spec = pl.BlockSpec((nseq, 2, CONV_CH), lambda i: (i, 0, 0))
        in_specs.append(st_spec)
        args.append(state)
    out_shape = [
        jax.ShapeDtypeStruct((n, ATTN_WIDTH), q_dtype),
        jax.ShapeDtypeStruct((n, 2 * KV_WIDTH), _F32),
        jax.ShapeDtypeStruct((n, CONV_CH), _BF),
        st_shape,
    ]
    out_specs = [
        pl.BlockSpec((tm, ATTN_WIDTH), lambda i: (i, 0)),
        pl.BlockSpec((tm, 2 * KV_WIDTH), lambda i: (i, 0)),
        pl.BlockSpec((tm, CONV_CH), lambda i: (i, 0)),
        st_spec,
    ]
    cast_in, cast_out, cast_shapes = _cast_specs(cast_weights, nt, lambda i: i)
    return pl.pallas_call(
        functools.partial(_inproj_kernel, seq_tiles=seq_tiles, q_dtype=q_dtype, n_cast=len(cast_weights)),
        grid=(nt,),
        in_specs=in_specs + cast_in,
        out_specs=out_specs + cast_out,
        out_shape=out_shape + cast_shapes,
        scratch_shapes=scratch,
        compiler_params=pltpu.CompilerParams(
            dimension_semantics=("arbitrary",), vmem_limit_bytes=VMEM_LIMIT),
        name="inproj_carry" if carry_mode else "inproj_seqs",
    )(*args, *[w for w, _ in cast_weights])


def _softmax_sink(s, sink):
    m = jnp.maximum(jnp.max(s, axis=-1, keepdims=True), sink)
    p = jnp.exp(s - m)
    return p, jnp.sum(p, axis=-1, keepdims=True) + jnp.exp(sink - m)


def _half_lane_operands(slab, low_half):
    lo = lax.broadcasted_iota(jnp.int32, slab.shape, 1) < HEAD_DIM
    moved = pltpu.roll(slab, HEAD_DIM, 1)
    in_low = jnp.where(lo, slab if low_half else moved, 0.0).astype(_BF)
    in_high = jnp.where(lo, 0.0, moved if low_half else slab).astype(_BF)
    return in_low, in_high


def _attn_prompt_kernel(sinks_ref, q_ref, kvc_ref, kvp_ref, g_ref, o_ref, acc_ref, cast_refs):
    i = pl.program_id(1)
    pair_w = 2 * HEAD_DIM
    qi = lax.broadcasted_iota(jnp.int32, (BLOCK, 2 * BLOCK), 0)
    kj = lax.broadcasted_iota(jnp.int32, (BLOCK, 2 * BLOCK), 1)
    dist_i = qi - kj + BLOCK
    in_window = (dist_i >= 0) & (dist_i <= WINDOW)
    far = -NEG / min(_SLOPES)
    dist_mid = jnp.where(in_window, dist_i.astype(_F32), far)
    dist_first = jnp.where(in_window & ((kj >= BLOCK) | (i > 0)), dist_i.astype(_F32), far)
    low = lax.broadcasted_iota(jnp.int32, (BLOCK, pair_w), 1) < HEAD_DIM

    k_ops, v_ops = [], []
    for m in range(N_KV_HEADS // 2):
        ksl = slice(m * pair_w, (m + 1) * pair_w)
        vsl = slice(KV_WIDTH + m * pair_w, KV_WIDTH + (m + 1) * pair_w)
        kslab = jnp.concatenate([kvp_ref[:, ksl], kvc_ref[:, ksl]], axis=0)
        vslab = jnp.concatenate([kvp_ref[:, vsl], kvc_ref[:, vsl]], axis=0)
        for low_half in (True, False):
            k_ops.append(_half_lane_operands(kslab, low_half))
            v_ops.append(_half_lane_operands(vslab, low_half))

    n_pairs = N_HEADS // 2
    work = [(u, c) for u in range(Q_BLOCKS) for c in range(n_pairs)]

    def scores(u, c):
        qc = q_ref[u * BLOCK:(u + 1) * BLOCK, c * pair_w:(c + 1) * pair_w]
        keys = slice(u * BLOCK, (u + 2) * BLOCK)
        k_low, k_high = k_ops[c // 2]
        return _dot_nt(qc, k_low[keys]), _dot_nt(qc, k_high[keys])

    s_next = scores(*work[0])
    for n, (u, c) in enumerate(work):
        s_pair = s_next
        if n + 1 < len(work):
            s_next = scores(*work[n + 1])
        keys = slice(u * BLOCK, (u + 2) * BLOCK)
        v_low, v_high = v_ops[c // 2]
        dist = dist_first if u == 0 else dist_mid
        p0, d0 = _softmax_sink(s_pair[0] - _SLOPES[2 * c] * dist, sinks_ref[0, 2 * c])
        p1, d1 = _softmax_sink(s_pair[1] - _SLOPES[2 * c + 1] * dist, sinks_ref[0, 2 * c + 1])
        o = _dot(p0.astype(_BF), v_low[keys]) + _dot(p1.astype(_BF), v_high[keys])
        acc_ref[u * BLOCK:(u + 1) * BLOCK, c * pair_w:(c + 1) * pair_w] = o / jnp.where(low, d0, d1)
    o_ref[...] = _rms(acc_ref[...], g_ref[...]).astype(_BF)
    _cast_rows(cast_refs)


def _cast_rows(cast_refs):
    n = len(cast_refs) // 2
    for src, dst in zip(cast_refs[:n], cast_refs[n:]):
        if len(dst.shape) == 2:
            dst[...] = src[...].astype(_BF)
        else:
            tw = dst.shape[2]
            for t in range(dst.shape[0]):
                dst[t] = src[:, t * tw:(t + 1) * tw].astype(_BF)


def _cast_specs(weights, steps, index):
    in_specs, out_specs, shapes = [], [], []
    for w, col_tile in weights:
        k, n = w.shape
        rows = k // steps
        assert rows * steps == k and rows % BF16_ROWS == 0
        in_specs.append(pl.BlockSpec((rows, n), lambda *ids: (index(*ids), 0)))
        if col_tile is None:
            out_specs.append(pl.BlockSpec((rows, n), lambda *ids: (index(*ids), 0)))
            shapes.append(jax.ShapeDtypeStruct((k, n), _BF))
        else:
            out_specs.append(pl.BlockSpec((n // col_tile, rows, col_tile), lambda *ids: (0, index(*ids), 0)))
            shapes.append(jax.ShapeDtypeStruct((n // col_tile, k, col_tile), _BF))
    return in_specs, out_specs, shapes


def _attn_prompt_body(sinks_ref, q_ref, kvc_ref, kvp_ref, g_ref, *rest, n_cast):
    cast_in, (o_ref,), cast_out, (acc_ref,) = (rest[:n_cast], rest[n_cast:n_cast + 1],
                                               rest[n_cast + 1:2 * n_cast + 1], rest[2 * n_cast + 1:])
    _attn_prompt_kernel(sinks_ref, q_ref, kvc_ref, kvp_ref, g_ref, o_ref, acc_ref, cast_in + cast_out)


def _attn_prompt(q, kv, sinks, g, cast_weights, *, batch, seq):
    rows = Q_BLOCKS * BLOCK
    nb = seq // rows
    cast_in, cast_out, cast_shapes = _cast_specs(cast_weights, batch * nb, lambda b, i: b * nb + i)
    return pl.pallas_call(
        functools.partial(_attn_prompt_body, n_cast=len(cast_weights)),
        grid=(batch, nb),
        in_specs=[
            pl.BlockSpec(memory_space=pltpu.SMEM),
            pl.BlockSpec((rows, ATTN_WIDTH), lambda b, i: (b * nb + i, 0)),
            pl.BlockSpec((rows, 2 * KV_WIDTH), lambda b, i: (b * nb + i, 0)),
            pl.BlockSpec((BLOCK, 2 * KV_WIDTH), lambda b, i: (jnp.maximum((b * nb + i) * Q_BLOCKS - 1, 0), 0)),
            pl.BlockSpec((1, ATTN_WIDTH), lambda b, i: (0, 0)),
        ] + cast_in,
        out_specs=[pl.BlockSpec((rows, ATTN_WIDTH), lambda b, i: (b * nb + i, 0))] + cast_out,
        out_shape=[jax.ShapeDtypeStruct((batch * seq, ATTN_WIDTH), _BF)] + cast_shapes,
        scratch_shapes=[pltpu.VMEM((rows, ATTN_WIDTH), _F32)],
        compiler_params=pltpu.CompilerParams(
            dimension_semantics=("arbitrary", "arbitrary"), vmem_limit_bytes=VMEM_LIMIT),
        name="attn_prompt",
    )(sinks, q, kv, kv, g, *[w for w, _ in cast_weights])


def _round_bf16(x):
    return x.astype(_BF).astype(_F32)


def _attn_sample_kernel(sinks_ref, q_ref, kvn_ref, ck_ref, cv_ref, g_ref,
                        o_ref, kwin_ref, vwin_ref, acc_ref, s_scr, *, nseq, t_new):
    rows = GROUP * t_new
    nkeys = WINDOW + t_new
    ksl = [slice(kvh * HEAD_DIM, (kvh + 1) * HEAD_DIM) for kvh in range(N_KV_HEADS)]
    vsl = [slice(KV_WIDTH + kvh * HEAD_DIM, KV_WIDTH + (kvh + 1) * HEAD_DIM) for kvh in range(N_KV_HEADS)]

    for kvh in range(N_KV_HEADS):
        for b in range(nseq):
            rs = slice(b * t_new, (b + 1) * t_new)
            qs = _round_bf16(jnp.concatenate(
                [q_ref[rs, (kvh * GROUP + gq) * HEAD_DIM:(kvh * GROUP + gq + 1) * HEAD_DIM]
                 for gq in range(GROUP)], axis=0))
            s_scr[kvh, b * rows:(b + 1) * rows, 0:WINDOW] = _dot(qs.astype(_BF), ck_ref[b, ksl[kvh], :].astype(_BF))
            s_scr[kvh, b * rows:(b + 1) * rows, WINDOW:] = _dot_nt(qs, _round_bf16(kvn_ref[rs, ksl[kvh]]))

    r = lax.broadcasted_iota(jnp.int32, (nseq * rows, 1), 0)
    gq_of_row = (r // t_new) % GROUP
    t = lax.broadcasted_iota(jnp.int32, (nseq * rows, nkeys), 0) % t_new
    kj = lax.broadcasted_iota(jnp.int32, (nseq * rows, nkeys), 1)
    dist_i = t + WINDOW - kj
    valid = (dist_i >= 0) & (dist_i <= WINDOW)
    dist = jnp.where(valid, dist_i.astype(_F32), -NEG / min(_SLOPES))
    for kvh in range(N_KV_HEADS):
        slope = jnp.full((nseq * rows, 1), _SLOPES[kvh * GROUP], _F32)
        sink = jnp.full((nseq * rows, 1), sinks_ref[0, kvh * GROUP], _F32)
        for gq in range(1, GROUP):
            slope = jnp.where(gq_of_row == gq, _SLOPES[kvh * GROUP + gq], slope)
            sink = jnp.where(gq_of_row == gq, sinks_ref[0, kvh * GROUP + gq], sink)
        p, denom = _softmax_sink(s_scr[kvh] - slope * dist, sink)
        s_scr[kvh] = _round_bf16(p / denom)

    for kvh in range(N_KV_HEADS):
        for b in range(nseq):
            rs = slice(b * t_new, (b + 1) * t_new)
            w = s_scr[kvh, b * rows:(b + 1) * rows, :]
            o = (_dot_nt(w[:, 0:WINDOW].astype(_BF), cv_ref[b, ksl[kvh], :].astype(_BF))
                 + _dot(w[:, WINDOW:], _round_bf16(kvn_ref[rs, vsl[kvh]])))
            for gq in range(GROUP):
                h = kvh * GROUP + gq
                acc_ref[rs, h * HEAD_DIM:(h + 1) * HEAD_DIM] = o[gq * t_new:(gq + 1) * t_new, :]
    o_ref[...] = _rms(acc_ref[...], g_ref[...]).astype(_BF)
    keep = lax.broadcasted_iota(jnp.int32, (KV_WIDTH, WINDOW), 1) < WINDOW - t_new
    per_col = WINDOW // t_new
    for cache_ref, win_ref, cols in ((ck_ref, kwin_ref, slice(0, KV_WIDTH)),
                                     (cv_ref, vwin_ref, slice(KV_WIDTH, 2 * KV_WIDTH))):
        new_t = jnp.transpose(kvn_ref[:, cols])
        for b in range(nseq):
            col = new_t[:, (b // per_col) * WINDOW:(b // per_col + 1) * WINDOW]
            shift = (WINDOW - t_new - (b % per_col) * t_new) % WINDOW
            placed = pltpu.roll(col, shift, 1) if shift else col
            win_ref[b] = jnp.where(keep, pltpu.roll(cache_ref[b], WINDOW - t_new, 1), placed)


def _attn_sample(q, kv, cache_k, cache_v, sinks, g, *, nseq_total, t_new, nseq=SAMPLE_SEQS):
    rows = nseq * t_new
    cache_spec = pl.BlockSpec((nseq, KV_WIDTH, WINDOW), lambda i: (i, 0, 0))
    return pl.pallas_call(
        functools.partial(_attn_sample_kernel, nseq=nseq, t_new=t_new),
        grid=(nseq_total // nseq,),
        in_specs=[
            pl.BlockSpec(memory_space=pltpu.SMEM),
            pl.BlockSpec((rows, ATTN_WIDTH), lambda i: (i, 0)),
            pl.BlockSpec((rows, 2 * KV_WIDTH), lambda i: (i, 0)),
            cache_spec,
            cache_spec,
            pl.BlockSpec((1, ATTN_WIDTH), lambda i: (0, 0)),
        ],
        out_specs=[pl.BlockSpec((rows, ATTN_WIDTH), lambda i: (i, 0)), cache_spec, cache_spec],
        out_shape=[
            jax.ShapeDtypeStruct((nseq_total * t_new, ATTN_WIDTH), _BF),
            jax.ShapeDtypeStruct(cache_k.shape, _F32),
            jax.ShapeDtypeStruct(cache_v.shape, _F32),
        ],
        scratch_shapes=[pltpu.VMEM((rows, ATTN_WIDTH), _F32),
                        pltpu.VMEM((N_KV_HEADS, nseq * GROUP * t_new, WINDOW + t_new), _F32)],
        compiler_params=pltpu.CompilerParams(dimension_semantics=("arbitrary",), vmem_limit_bytes=VMEM_LIMIT),
        name="attn_sample",
    )(sinks, q, kv, cache_k, cache_v, g)


def _ffn_kernel(*refs, seq_tiles):
    carry_mode = seq_tiles is not None
    if carry_mode:
        (x_ref, na_ref, nb_ref, woa_ref, wob_ref, g_ref, wg_ref, wu_ref, cw_ref, cb_ref, wd_ref, gf_ref,
         o_ref, st_ref, h_scr, acc_scr, ext_scr, carry_scr) = refs
    else:
        (x_ref, na_ref, nb_ref, woa_ref, wob_ref, g_ref, wg_ref, wu_ref, cw_ref, cb_ref, wd_ref, gf_ref, state_ref,
         o_ref, st_ref, h_scr, acc_scr) = refs
    i = pl.program_id(0)
    j = pl.program_id(1)

    @pl.when(j == 0)
    def _():
        x1 = x_ref[...] + _dot(na_ref[...], woa_ref[...]) + _dot(nb_ref[...], wob_ref[...])
        acc_scr[...] = x1
        h_scr[...] = _rms(x1, g_ref[...]).astype(_BF)

    h = h_scr[...]
    gate = _dot(h, wg_ref[...])
    if carry_mode:
        conv = _conv3_carry(gate, cw_ref, ext_scr, carry_scr.at[j], i % seq_tiles == 0)
        st_ref[0] = gate[gate.shape[0] - 2:, :]
    else:
        conv, last2 = _conv3_seqs(gate, cw_ref, state_ref)
        st_ref[...] = last2
    a = conv + cb_ref[...]
    act = ((a / (1.0 + jnp.exp(-a))) * _dot(h, wu_ref[...])).astype(_BF)
    acc_scr[...] += _dot(act, wd_ref[...])

    @pl.when(j == FF_STEPS - 1)
    def _():
        o_ref[...] = _rms(acc_scr[...], gf_ref[...])


def _ffn(x, na, nb, w_out, g, w_gate, w_up, conv_w, conv_b, w_down, g_final, state, *, seq_len):
    n = x.shape[0]
    tm = FF_ROWS
    nt = n // tm
    carry_mode = seq_len >= tm
    seq_tiles = seq_len // tm if carry_mode else None

    def st_map(i, j):
        return (i, 0, j)

    in_specs = [
        pl.BlockSpec((tm, D_MODEL), lambda i, j: (i, 0)),
        pl.BlockSpec((tm, ATTN_WIDTH), lambda i, j: (i, 0)),
        pl.BlockSpec((tm, CONV_CH), lambda i, j: (i, 0)),
        pl.BlockSpec((ATTN_WIDTH, D_MODEL), lambda i, j: (0, 0), pipeline_mode=pl.Buffered(1)),
        pl.BlockSpec((CONV_CH, D_MODEL), lambda i, j: (1, 0), pipeline_mode=pl.Buffered(1)),
        pl.BlockSpec((1, D_MODEL), lambda i, j: (0, 0)),
        pl.BlockSpec((None, D_MODEL, FF_TILE), lambda i, j: (j, 0, 0)),
        pl.BlockSpec((None, D_MODEL, FF_TILE), lambda i, j: (j, 0, 0)),
        pl.BlockSpec((3, FF_TILE), lambda i, j: (0, j)),
        pl.BlockSpec((1, FF_TILE), lambda i, j: (0, j)),
        pl.BlockSpec((FF_TILE, D_MODEL), lambda i, j: (j, 0)),
        pl.BlockSpec((1, D_MODEL), lambda i, j: (0, 0)),
    ]
    args = [x, na, nb, w_out, w_out, g, w_gate, w_up, conv_w, conv_b, w_down, g_final]
    scratch = [pltpu.VMEM((tm, D_MODEL), _BF), pltpu.VMEM((tm, D_MODEL), _F32)]
    if carry_mode:
        st_shape = jax.ShapeDtypeStruct((nt, 2, D_FF), _F32)
        st_spec = pl.BlockSpec((1, 2, FF_TILE), st_map)
        scratch += [pltpu.VMEM((tm + SUBLANES, FF_TILE), _F32),
                    pltpu.VMEM((FF_STEPS, SUBLANES, FF_TILE), _F32)]
    else:
        nseq = tm // seq_len
        st_shape = jax.ShapeDtypeStruct((n // seq_len, 2, D_FF), _F32)
        st_spec = pl.BlockSpec((nseq, 2, FF_TILE), st_map)
        in_specs.append(st_spec)
        args.append(state)
    return pl.pallas_call(
        functools.partial(_ffn_kernel, seq_tiles=seq_tiles),
        grid=(nt, FF_STEPS),
        in_specs=in_specs,
        out_specs=[pl.BlockSpec((tm, D_MODEL), lambda i, j: (i, 0)), st_spec],
        out_shape=[jax.ShapeDtypeStruct((n, D_MODEL), _F32), st_shape],
        scratch_shapes=scratch,
        compiler_params=pltpu.CompilerParams(
            dimension_semantics=("arbitrary", "arbitrary"), vmem_limit_bytes=VMEM_LIMIT),
        name="ffn_carry" if carry_mode else "ffn_seqs",
    )(*args)


def _tile_cast_kernel(w_ref, o_ref):
    o_ref[...] = w_ref[...].astype(_BF)


def _tile_cast(w, col_tile):
    k, n = w.shape
    return pl.pallas_call(
        _tile_cast_kernel,
        grid=(n // col_tile,),
        in_specs=[pl.BlockSpec((k, col_tile), lambda t: (0, t))],
        out_specs=pl.BlockSpec((None, k, col_tile), lambda t: (t, 0, 0)),
        out_shape=jax.ShapeDtypeStruct((n // col_tile, k, col_tile), _BF),
        compiler_params=pltpu.CompilerParams(dimension_semantics=("arbitrary",)),
        name="tile_cast",
    )(w)


def _keys_minor(cache):
    return jnp.transpose(cache, (0, 2, 3, 1)).reshape(cache.shape[0], KV_WIDTH, WINDOW)


def _keys_major(win):
    return jnp.transpose(win.reshape(win.shape[0], N_KV_HEADS, HEAD_DIM, WINDOW), (0, 3, 1, 2))[None]


def kernel(x_prompt, x_sample, cache_k_window, cache_v_window, state_conv, state_ffn_conv, g_attn_norm, w_in, attn_sinks, conv_w, g_out_attn, g_out_conv, w_out, g_ffn_norm, w_gate, w_up, ffn_conv_w, ffn_conv_b, w_down, g_final):
    assert w_in.shape[0] == 1, "single layer"
    bp, sp, _ = x_prompt.shape
    bs, ts, _ = x_sample.shape
    assert sp % IN_ROWS == 0 and (bs * ts) % IN_ROWS == 0 and ts == SUBLANES

    w_in_b = _tile_cast(w_in[0], COL_TILE)
    g_attn = g_attn_norm[0].reshape(1, D_MODEL)
    g_ffn = g_ffn_norm[0].reshape(1, D_MODEL)
    g_fin = g_final.reshape(1, D_MODEL)
    g_oa = g_out_attn[0].reshape(1, ATTN_WIDTH)
    g_oc = g_out_conv[0].reshape(1, CONV_CH)
    sinks = attn_sinks[0].reshape(1, N_HEADS)
    cw = conv_w[0]
    fcw = ffn_conv_w[0]
    fcb = ffn_conv_b[0].reshape(1, D_FF)

    xp = x_prompt.reshape(bp * sp, D_MODEL)
    q_p, kv_p, nb_p, cst_p, w_out_b, w_down_b = _inproj(
        xp, g_attn, w_in_b, cw, g_oc, None, [(w_out[0], None), (w_down[0], None)], seq_len=sp, q_dtype=_BF)
    na_p, w_gate_b, w_up_b = _attn_prompt(
        q_p, kv_p, sinks, g_oa, [(w_gate[0], FF_TILE), (w_up[0], FF_TILE)], batch=bp, seq=sp)
    y_p, fst_p = _ffn(xp, na_p, nb_p, w_out_b, g_ffn, w_gate_b, w_up_b, fcw, fcb, w_down_b, g_fin, None, seq_len=sp)

    seq_tiles = sp // min(IN_ROWS, sp)
    ff_seq_tiles = sp // FF_ROWS
    kv_p3 = kv_p.reshape(bp, sp, 2 * KV_WIDTH)[:, sp - WINDOW:, :]
    k_win_p = kv_p3[:, :, :KV_WIDTH].reshape(1, bp, WINDOW, N_KV_HEADS, HEAD_DIM)
    v_win_p = kv_p3[:, :, KV_WIDTH:].reshape(1, bp, WINDOW, N_KV_HEADS, HEAD_DIM)
    conv_p = cst_p[seq_tiles - 1::seq_tiles][None]
    ffn_conv_p = fst_p[ff_seq_tiles - 1::ff_seq_tiles][None]

    xs = x_sample.reshape(bs * ts, D_MODEL)
    q_s, kv_s, nb_s, cst_s = _inproj(xs, g_attn, w_in_b, cw, g_oc, state_conv[0], [], seq_len=ts, q_dtype=_F32)
    ck = _keys_minor(cache_k_window[0])
    cv = _keys_minor(cache_v_window[0])
    na_s, k_win_s, v_win_s = _attn_sample(q_s, kv_s, ck, cv, sinks, g_oa, nseq_total=bs, t_new=ts)
    y_s, fst_s = _ffn(xs, na_s, nb_s, w_out_b, g_ffn, w_gate_b, w_up_b, fcw, fcb, w_down_b, g_fin,
                      state_ffn_conv[0], seq_len=ts)

    win_shape = (1, bs, WINDOW, N_KV_HEADS, HEAD_DIM)
    return (y_p.reshape(bp, sp, D_MODEL), y_s.reshape(bs, ts, D_MODEL),
            k_win_p, v_win_p, conv_p, ffn_conv_p,
            _keys_major(k_win_s), _keys_major(v_win_s), cst_s[None], fst_s[None])
```
